```python
import math
import jax, jax.numpy as jnp
from jax import lax
import numpy as np

D_MODEL = 1024
BATCH = 2
SEQ = 16384
DEPTH = 4

HEAD_DIM = 64
BLK = 128
NEG_INF = -1e30
REL_BUCKETS = 32
REL_MAX_DIST = 2048
A_HEADS = 8
A_KV_HEADS = 2
A_WINDOW = 128
B_HEADS = 8
B_PATTERNS = ((128, 1), (512, 4), (2048, 16))
C_HEADS = 4
C_QK_DIM = 256
C_V_DIM = 512
C_CHUNK = 128
D_HEADS = 16
D_KV_HEADS = 2
D_CMP_LEN = 32
D_CMP_STRIDE = 16
D_CMP_HIDDEN = 128
D_SEL_LEN = 64
D_SEL_COUNT = 16
D_WINDOW = 512
D_FF = -(-8 * D_MODEL // (3 * 256)) * 256

N_BIAS_HEADS = A_HEADS + B_HEADS
A_Q = A_HEADS * HEAD_DIM
A_KV = A_KV_HEADS * HEAD_DIM
B_W = B_HEADS * HEAD_DIM
EVEN_SIZES = (A_Q, A_KV, A_KV, B_W, B_W, B_W)
EVEN_IN = sum(EVEN_SIZES)
EVEN_OUT = A_Q + B_W
C_QK = C_HEADS * C_QK_DIM
C_V = C_HEADS * C_V_DIM
D_Q = D_HEADS * HEAD_DIM
D_KV = D_KV_HEADS * HEAD_DIM
ODD_SIZES = (C_QK, C_QK, C_V, C_V, D_Q, D_KV, D_KV, D_KV, D_KV, D_KV, D_KV, 3 * D_HEADS)
ODD_IN = sum(ODD_SIZES)
ODD_OUT = C_V + D_Q

kernel_name = "hybrid_swa_dilated_retention_nsa_trunk"


def split_cols(x, sizes):
    idx = [int(i) for i in np.cumsum(sizes)[:-1]]
    return jnp.split(x, idx, axis=-1)


def rms_norm(x, w, eps=1e-6):
    xf = x.astype(jnp.float32)
    y = xf * lax.rsqrt(jnp.mean(xf * xf, axis=-1, keepdims=True) + eps)
    return (y * w.astype(jnp.float32)).astype(x.dtype)


def t5_bucket(dist):
    max_exact = REL_BUCKETS // 2
    d = jnp.maximum(dist, 0)
    df = jnp.maximum(d, 1).astype(jnp.float32)
    large = max_exact + (jnp.log(df / max_exact) / math.log(REL_MAX_DIST / max_exact)
                         * (REL_BUCKETS - max_exact)).astype(jnp.int32)
    large = jnp.minimum(large, REL_BUCKETS - 1)
    return jnp.where(d < max_exact, d, large)


def masked_softmax(s, mask):
    s = jnp.where(mask, s, NEG_INF)
    m = jnp.max(s, axis=-1, keepdims=True)
    p = jnp.where(mask, jnp.exp(s - m), 0.0)
    return p / jnp.maximum(jnp.sum(p, axis=-1, keepdims=True), 1e-30)


def banded_attention(q, k, v, max_dist, dist_scale, bias_table, sinks=None):
    B, L, H, D = q.shape
    G = k.shape[2]
    hpg = H // G
    nb = -(-max_dist // BLK)
    n_blk = -(-L // BLK)
    Lp = n_blk * BLK
    qp = jnp.pad(q, ((0, 0), (0, Lp - L), (0, 0), (0, 0)))
    kp = jnp.pad(k, ((0, 0), (nb * BLK, Lp - L), (0, 0), (0, 0)))
    vp = jnp.pad(v, ((0, 0), (nb * BLK, Lp - L), (0, 0), (0, 0)))
    kw = (nb + 1) * BLK
    rel = jnp.arange(BLK)[:, None] + nb * BLK - jnp.arange(kw)[None, :]
    band = (rel >= 0) & (rel <= max_dist)
    bias = jnp.transpose(bias_table.astype(jnp.float32)[t5_bucket(rel * dist_scale)], (2, 0, 1))
    bias = bias.reshape(G, hpg, BLK, kw)
    scale = D ** -0.5

    def one_block(i):
        qb = lax.dynamic_slice_in_dim(qp, i * BLK, BLK, axis=1).reshape(B, BLK, G, hpg, D)
        kb = lax.dynamic_slice_in_dim(kp, i * BLK, kw, axis=1)
        vb = lax.dynamic_slice_in_dim(vp, i * BLK, kw, axis=1)
        s = jnp.einsum('btghd,bsgd->bghts', qb, kb).astype(jnp.float32) * scale + bias
        kpos = i * BLK - nb * BLK + jnp.arange(kw)
        mask = band & (kpos >= 0)[None, :]
        s = jnp.where(mask, s, NEG_INF)
        m = jnp.max(s, axis=-1, keepdims=True)
        if sinks is not None:
            sk = sinks.astype(jnp.float32).reshape(1, G, hpg, 1, 1)
            m = jnp.maximum(m, sk)
        p = jnp.exp(s - m)
        denom = jnp.sum(p, axis=-1, keepdims=True)
        if sinks is not None:
            denom = denom + jnp.exp(sk - m)
        o = jnp.einsum('bghts,bsgd->btghd', (p / denom).astype(v.dtype), vb)
        lse = (m + jnp.log(denom))[..., 0]
        return o.reshape(B, BLK, H, D), jnp.transpose(lse, (0, 3, 1, 2)).reshape(B, BLK, H)

    o, lse = lax.map(one_block, jnp.arange(n_blk))
    o = jnp.transpose(o, (1, 0, 2, 3, 4)).reshape(B, Lp, H, D)[:, :L]
    lse = jnp.transpose(lse, (1, 0, 2, 3)).reshape(B, Lp, H)[:, :L]
    return o, lse


def dilated_attention(q, k, v, bias_table):
    B, S, H, D = q.shape
    outs, lses = [], []
    for window, dil in B_PATTERNS:
        L = S // dil

        def to_sub(a):
            return jnp.transpose(a.reshape(B, L, dil, H, D), (0, 2, 1, 3, 4)).reshape(B * dil, L, H, D)

        o, lse = banded_attention(to_sub(q), to_sub(k), to_sub(v), window // dil, dil, bias_table)
        outs.append(jnp.transpose(o.reshape(B, dil, L, H, D), (0, 2, 1, 3, 4)).reshape(B, S, H, D))
        lses.append(jnp.transpose(lse.reshape(B, dil, L, H), (0, 2, 1, 3)).reshape(B, S, H))
    w = jax.nn.softmax(jnp.stack(lses, axis=0), axis=0)
    out = jnp.einsum('pbsh,pbshd->bshd', w, jnp.stack(outs, axis=0).astype(jnp.float32))
    return out.astype(q.dtype)


def rotary(x, pos):
    d = x.shape[-1]
    inv = 1.0 / (10000.0 ** (jnp.arange(0, d, 2, dtype=jnp.float32) / d))
    ang = pos.astype(jnp.float32)[:, None] * inv[None, :]
    cos = jnp.cos(ang)[:, None, :]
    sin = jnp.sin(ang)[:, None, :]
    x1, x2 = jnp.split(x, 2, axis=-1)
    return jnp.concatenate([x1 * cos - x2 * sin, x1 * sin + x2 * cos], axis=-1)


def retention(q, k, v):
    B, S, H, dk = q.shape
    dv = v.shape[-1]
    pos = jnp.arange(S)
    q = rotary(q, pos)
    k = rotary(k, pos) * dk ** -0.5
    log_g = jnp.log(1.0 - 2.0 ** (-5.0 - jnp.arange(H, dtype=jnp.float32)))
    C = C_CHUNK
    nC = S // C
    j = jnp.arange(C, dtype=jnp.float32)
    diff = j[:, None] - j[None, :]
    dmask = jnp.where(diff >= 0, jnp.exp(diff[None] * log_g[:, None, None]), 0.0)
    q_dec = jnp.exp((j[None, :] + 1.0) * log_g[:, None])[None, :, :, None]
    k_dec = jnp.exp((C - 1.0 - j[None, :]) * log_g[:, None])[None, :, :, None]
    chunk_dec = jnp.exp(C * log_g)[None, :, None, None]

    def chunks(a):
        return jnp.transpose(a.reshape(B, nC, C, H, a.shape[-1]), (1, 0, 3, 2, 4))

    def step(R, xs):
        qc, kc, vc = xs
        inner = jnp.einsum('bhid,bhjd->bhij', qc, kc) * dmask
        o = jnp.einsum('bhij,bhjv->bhiv', inner, vc) + jnp.einsum('bhid,bhdv->bhiv', qc, R) * q_dec
        R = R * chunk_dec + jnp.einsum('bhjd,bhjv->bhdv', kc * k_dec, vc)
        return R, o

    R0 = jnp.zeros((B, H, dk, dv), jnp.float32)
    _, o = lax.scan(step, R0, (chunks(q), chunks(k), chunks(v)))
    return jnp.transpose(o, (1, 0, 3, 2, 4)).reshape(B, S, H, dv)


def nsa_attention(q, k_cmp, v_cmp, k_slc, v_slc, k_win, v_win, gate_logits,
                  pos_k, pos_v, k_w1, k_w2, v_w1, v_w2, rel_table):
    B, S, H, D = q.shape
    G = k_cmp.shape[2]
    hpg = H // G
    rc = D_CMP_LEN // D_CMP_STRIDE
    rs = D_SEL_LEN // D_CMP_STRIDE
    n_cmp = S // D_CMP_STRIDE - (rc - 1)

    def compress(a, pos, w1, w2):
        xs = a.reshape(B, S // D_CMP_STRIDE, D_CMP_STRIDE, G, D)
        blocks = jnp.concatenate([xs[:, j:j + n_cmp] for j in range(rc)], axis=2)
        blocks = blocks + pos[None, None, :, None, :].astype(a.dtype)
        flat = jnp.transpose(blocks, (0, 1, 3, 2, 4)).reshape(B, n_cmp, G, D_CMP_LEN * D)
        return jax.nn.gelu(flat @ w1) @ w2

    kc = compress(k_cmp, pos_k, k_w1, k_w2)
    vc = compress(v_cmp, pos_v, v_w1, v_w2)
    cmp_end = jnp.arange(n_cmp) * D_CMP_STRIDE + D_CMP_LEN - 1
    n_sb = S // D_SEL_LEN
    n_sel = min(D_SEL_COUNT, n_sb)
    k_blk = jnp.transpose(k_slc.reshape(B, n_sb, D_SEL_LEN, G, D), (0, 3, 1, 2, 4))
    v_blk = jnp.transpose(v_slc.reshape(B, n_sb, D_SEL_LEN, G, D), (0, 3, 1, 2, 4))
    table = rel_table.astype(jnp.float32)
    table_g = jnp.transpose(table.reshape(REL_BUCKETS, G, hpg), (1, 0, 2))
    gidx = jnp.arange(G)[None, :, None, None, None]
    scale = D ** -0.5
    gather = jax.vmap(jax.vmap(lambda kb, ix: kb[ix]))

    def one_block(i):
        t = i * BLK + jnp.arange(BLK)
        qb = lax.dynamic_slice_in_dim(q, i * BLK, BLK, axis=1).reshape(B, BLK, G, hpg, D)
        bias_c = jnp.transpose(table[t5_bucket(t[:, None] - cmp_end[None, :])], (2, 0, 1))
        bias_c = bias_c.reshape(G, hpg, BLK, n_cmp)
        s_c = jnp.einsum('btghd,bngd->bghtn', qb, kc).astype(jnp.float32) * scale + bias_c
        p_c = masked_softmax(s_c, cmp_end[None, :] <= t[:, None])
        o_c = jnp.einsum('bghtn,bngd->btghd', p_c.astype(vc.dtype), vc)
        imp = jnp.pad(p_c.sum(axis=2), ((0, 0), (0, 0), (0, 0), (rc - 1, rc - 1)))
        imp_sel = sum(imp[..., u:u + rs * (n_sb - 1) + 1:rs] for u in range(rs + rc - 1))
        cur = t // D_SEL_LEN
        jb = jnp.arange(n_sb)[None, :]
        forced = (jb == 0) | (jb == cur[:, None]) | (jb == cur[:, None] - 1)
        valid = jb <= cur[:, None]
        score = jnp.where(forced, 1e9, jnp.where(valid, imp_sel, -1e9))
        _, idx = lax.top_k(score, n_sel)
        ks = gather(k_blk, idx)
        vs = gather(v_blk, idx)
        kpos = idx[..., None] * D_SEL_LEN + jnp.arange(D_SEL_LEN)
        dist = t[None, None, :, None, None] - kpos
        bias_s = jnp.moveaxis(table_g[gidx, t5_bucket(dist)], -1, 2)
        s_s = jnp.einsum('btghd,bgtnld->bghtnl', qb, ks).astype(jnp.float32) * scale + bias_s
        nl = n_sel * D_SEL_LEN
        p_s = masked_softmax(s_s.reshape(B, G, hpg, BLK, nl),
                             (dist >= 0).reshape(B, G, 1, BLK, nl)).reshape(s_s.shape)
        o_s = jnp.einsum('bghtnl,bgtnld->btghd', p_s.astype(vs.dtype), vs)
        return o_c.reshape(B, BLK, H, D), o_s.reshape(B, BLK, H, D)

    o_c, o_s = lax.map(one_block, jnp.arange(S // BLK))
    o_c = jnp.transpose(o_c, (1, 0, 2, 3, 4)).reshape(B, S, H, D)
    o_s = jnp.transpose(o_s, (1, 0, 2, 3, 4)).reshape(B, S, H, D)
    o_w, _ = banded_attention(q, k_win, v_win, D_WINDOW - 1, 1, rel_table)
    g = jax.nn.sigmoid(gate_logits.astype(jnp.float32))
    out = g[..., 0:1] * o_c + g[..., 1:2] * o_s + g[..., 2:3] * o_w
    return out.astype(q.dtype)


def even_mixer(h, w_in, sinks, w_out, rel_table):
    B, S, _ = h.shape
    qa, ka, va, qb, kb, vb = split_cols(h @ w_in, EVEN_SIZES)
    qa = qa.reshape(B, S, A_HEADS, HEAD_DIM)
    ka = ka.reshape(B, S, A_KV_HEADS, HEAD_DIM)
    va = va.reshape(B, S, A_KV_HEADS, HEAD_DIM)
    oa, _ = banded_attention(qa, ka, va, A_WINDOW - 1, 1, rel_table[:, :A_HEADS], sinks)
    qb = qb.reshape(B, S, B_HEADS, HEAD_DIM)
    kb = kb.reshape(B, S, B_HEADS, HEAD_DIM)
    vb = vb.reshape(B, S, B_HEADS, HEAD_DIM)
    ob = dilated_attention(qb, kb, vb, rel_table[:, A_HEADS:A_HEADS + B_HEADS])
    o = jnp.concatenate([oa.reshape(B, S, A_Q), ob.reshape(B, S, B_W)], axis=-1)
    return o @ w_out


def odd_mixer(h, w_in, ret_gn, cmp_pos_k, cmp_pos_v, cmp_k_w1, cmp_k_w2, cmp_v_w1, cmp_v_w2,
              w_out, rel_table):
    B, S, _ = h.shape
    (qc, kc, vc, gc, qd, k_cmp, v_cmp, k_slc, v_slc, k_win, v_win, gd) = split_cols(h @ w_in, ODD_SIZES)
    oc = retention(qc.reshape(B, S, C_HEADS, C_QK_DIM).astype(jnp.float32),
                   kc.reshape(B, S, C_HEADS, C_QK_DIM).astype(jnp.float32),
                   vc.reshape(B, S, C_HEADS, C_V_DIM).astype(jnp.float32))
    mu = jnp.mean(oc, axis=-1, keepdims=True)
    var = jnp.mean(jnp.square(oc - mu), axis=-1, keepdims=True)
    oc = ((oc - mu) * lax.rsqrt(var + 1e-5)).reshape(B, S, C_V)
    oc = (oc * ret_gn.astype(jnp.float32) * jax.nn.silu(gc.astype(jnp.float32))).astype(h.dtype)
    kv = lambda a: a.reshape(B, S, D_KV_HEADS, HEAD_DIM)
    od = nsa_attention(qd.reshape(B, S, D_HEADS, HEAD_DIM), kv(k_cmp), kv(v_cmp), kv(k_slc), kv(v_slc),
                       kv(k_win), kv(v_win), gd.reshape(B, S, D_HEADS, 3),
                       cmp_pos_k, cmp_pos_v, cmp_k_w1, cmp_k_w2, cmp_v_w1, cmp_v_w2, rel_table)
    o = jnp.concatenate([oc, od.reshape(B, S, D_Q)], axis=-1)
    return o @ w_out


def swiglu(h, w_gate, w_up, w_down):
    return (jax.nn.silu(h @ w_gate) * (h @ w_up)) @ w_down


def setup_inputs(seed: int = 0) -> dict:
    key = jax.random.key(seed)
    ks = jax.random.split(key, 24)
    n_even = (DEPTH + 1) // 2
    n_odd = DEPTH // 2

    def nrm(k, shape, scale):
        return jax.random.normal(k, shape, jnp.float32) * scale

    return {
        "x": nrm(ks[0], (BATCH, SEQ, D_MODEL), 1.0),
        "rel_table": nrm(ks[1], (REL_BUCKETS, N_BIAS_HEADS), 0.5),
        "norm_mix": 1.0 + nrm(ks[2], (DEPTH, D_MODEL), 0.02),
        "norm_ffn": 1.0 + nrm(ks[3], (DEPTH, D_MODEL), 0.02),
        "norm_final": 1.0 + nrm(ks[4], (D_MODEL,), 0.02),
        "even_w_in": nrm(ks[5], (n_even, D_MODEL, EVEN_IN), D_MODEL ** -0.5),
        "even_sinks": nrm(ks[6], (n_even, A_HEADS), 0.5),
        "even_w_out": nrm(ks[7], (n_even, EVEN_OUT, D_MODEL), EVEN_OUT ** -0.5),
        "odd_w_in": nrm(ks[8], (n_odd, D_MODEL, ODD_IN), D_MODEL ** -0.5),
        "odd_ret_gn": 1.0 + nrm(ks[9], (n_odd, C_V), 0.02),
        "odd_cmp_pos_k": nrm(ks[10], (n_odd, D_CMP_LEN, HEAD_DIM), 0.1),
        "odd_cmp_pos_v": nrm(ks[11], (n_odd, D_CMP_LEN, HEAD_DIM), 0.1),
        "odd_cmp_k_w1": nrm(ks[12], (n_odd, D_CMP_LEN * HEAD_DIM, D_CMP_HIDDEN), (D_CMP_LEN * HEAD_DIM) ** -0.5),
        "odd_cmp_k_w2": nrm(ks[13], (n_odd, D_CMP_HIDDEN, HEAD_DIM), D_CMP_HIDDEN ** -0.5),
        "odd_cmp_v_w1": nrm(ks[14], (n_odd, D_CMP_LEN * HEAD_DIM, D_CMP_HIDDEN), (D_CMP_LEN * HEAD_DIM) ** -0.5),
        "odd_cmp_v_w2": nrm(ks[15], (n_odd, D_CMP_HIDDEN, HEAD_DIM), D_CMP_HIDDEN ** -0.5),
        "odd_w_out": nrm(ks[16], (n_odd, ODD_OUT, D_MODEL), ODD_OUT ** -0.5),
        "ffn_w_gate": nrm(ks[17], (DEPTH, D_MODEL, D_FF), D_MODEL ** -0.5),
        "ffn_w_up": nrm(ks[18], (DEPTH, D_MODEL, D_FF), D_MODEL ** -0.5),
        "ffn_w_down": nrm(ks[19], (DEPTH, D_FF, D_MODEL), D_FF ** -0.5),
    }


def reference(x, rel_table, norm_mix, norm_ffn, norm_final, even_w_in, even_sinks, even_w_out,
              odd_w_in, odd_ret_gn, odd_cmp_pos_k, odd_cmp_pos_v, odd_cmp_k_w1, odd_cmp_k_w2,
              odd_cmp_v_w1, odd_cmp_v_w2, odd_w_out, ffn_w_gate, ffn_w_up, ffn_w_down):
    h = x
    for layer in range(DEPTH):
        hn = rms_norm(h, norm_mix[layer])
        li = layer // 2
        if layer % 2 == 0:
            h = h + even_mixer(hn, even_w_in[li], even_sinks[li], even_w_out[li], rel_table)
        else:
            h = h + odd_mixer(hn, odd_w_in[li], odd_ret_gn[li], odd_cmp_pos_k[li], odd_cmp_pos_v[li],
                              odd_cmp_k_w1[li], odd_cmp_k_w2[li], odd_cmp_v_w1[li], odd_cmp_v_w2[li],
                              odd_w_out[li], rel_table)
        h = h + swiglu(rms_norm(h, norm_ffn[layer]), ffn_w_gate[layer], ffn_w_up[layer], ffn_w_down[layer])
    return rms_norm(h, norm_final)
```

```python
import functools
import math

import numpy as np
import jax
import jax.numpy as jnp
from jax import lax
from jax.experimental import pallas as pl
from jax.experimental.pallas import tpu as pltpu

F32 = jnp.float32
BF16 = jnp.bfloat16

D_MODEL = 1024
DEPTH = 4
HEAD_DIM = 64
BLK = 128
NEG_INF = -1e30
REL_BUCKETS = 32
REL_MAX_DIST = 2048
A_HEADS = 8
A_KV_HEADS = 2
A_WINDOW = 128
B_HEADS = 8
B_PATTERNS = ((128, 1), (512, 4), (2048, 16))
C_HEADS = 4
C_QK_DIM = 256
C_V_DIM = 512
C_CHUNK = 128
D_HEADS = 16
D_KV_HEADS = 2
D_CMP_LEN = 32
D_CMP_STRIDE = 16
D_CMP_HIDDEN = 128
D_SEL_LEN = 64
D_SEL_COUNT = 16
D_WINDOW = 512
D_FF = 2816

A_Q = A_HEADS * HEAD_DIM
A_KV = A_KV_HEADS * HEAD_DIM
B_W = B_HEADS * HEAD_DIM
C_QK = C_HEADS * C_QK_DIM
C_V = C_HEADS * C_V_DIM
D_Q = D_HEADS * HEAD_DIM
D_KV = D_KV_HEADS * HEAD_DIM
ODD_IN = 2 * C_QK + 2 * C_V + D_Q + 6 * D_KV + 3 * D_HEADS

LANES = 128
VMEM_LIMIT = 56 * 1024 * 1024

EVEN_SLAB = 2560
EV_QA, EV_QB, EV_KB, EV_VB, EV_KA, EV_VA = 0, 512, 1024, 1536, 2048, 2176
ODD_SLAB = 8192
OD_QC, OD_KC, OD_VC, OD_GC, OD_QD = 0, 1024, 2048, 4096, 6144
OD_KCMP, OD_VCMP, OD_KSLC, OD_VSLC, OD_KWIN, OD_VWIN, OD_GD = 7168, 7296, 7424, 7552, 7680, 7808, 7936

SEL_FAR = 25
SEL_NONE = 26
CMP_WIN = 1024
REL_BLOCKS = 256


def _cparams(sem):
    return pltpu.CompilerParams(dimension_semantics=sem, vmem_limit_bytes=VMEM_LIMIT)


def _t5_bucket(dist):
    max_exact = REL_BUCKETS // 2
    d = jnp.maximum(dist, 0)
    df = jnp.maximum(d, 1).astype(jnp.float32)
    large = max_exact + (jnp.log(df / max_exact) / math.log(REL_MAX_DIST / max_exact)
                         * (REL_BUCKETS - max_exact)).astype(jnp.int32)
    large = jnp.minimum(large, REL_BUCKETS - 1)
    return jnp.where(d < max_exact, d, large)


def _rms(x, w, eps=1e-6):
    return x * lax.rsqrt(jnp.mean(x * x, axis=-1, keepdims=True) + eps) * w


def _norm_matmul_kernel(h_ref, nw_ref, w_ref, o_ref, hn_ref):
    @pl.when(pl.program_id(1) == 0)
    def _():
        hn_ref[...] = _rms(h_ref[...], nw_ref[...]).astype(BF16)

    o_ref[...] = jnp.dot(hn_ref[...], w_ref[...], preferred_element_type=F32).astype(o_ref.dtype)


def _norm_matmul(h, nw, w, tm, tn):
    m, d = h.shape
    n = w.shape[1]
    return pl.pallas_call(
        _norm_matmul_kernel,
        grid=(m // tm, n // tn),
        in_specs=[pl.BlockSpec((tm, d), lambda i, j: (i, 0)),
                  pl.BlockSpec((1, d), lambda i, j: (0, 0)),
                  pl.BlockSpec((d, tn), lambda i, j: (0, j))],
        out_specs=pl.BlockSpec((tm, tn), lambda i, j: (i, j)),
        out_shape=jax.ShapeDtypeStruct((m, n), BF16),
        scratch_shapes=[pltpu.VMEM((tm, d), BF16)],
        compiler_params=_cparams(("parallel", "arbitrary")),
        name="norm_matmul",
    )(h, nw.reshape(1, d), w)


def _proj_residual_kernel(*refs, n_in):
    h_ref, out_ref = refs[0], refs[-1]
    acc = h_ref[...]
    for o_ref, w_ref in zip(refs[1:1 + n_in], refs[1 + n_in:1 + 2 * n_in]):
        acc = acc + jnp.dot(o_ref[...], w_ref[...], preferred_element_type=F32)
    out_ref[...] = acc


def _proj_residual(h, outs, ws, tm):
    m, d = h.shape
    n_in = len(outs)
    in_specs = [pl.BlockSpec((tm, d), lambda i: (i, 0))]
    in_specs += [pl.BlockSpec((tm, o.shape[1]), lambda i: (i, 0)) for o in outs]
    in_specs += [pl.BlockSpec(w.shape, lambda i: (0, 0)) for w in ws]
    return pl.pallas_call(
        functools.partial(_proj_residual_kernel, n_in=n_in),
        grid=(m // tm,),
        in_specs=in_specs,
        out_specs=pl.BlockSpec((tm, d), lambda i: (i, 0)),
        out_shape=jax.ShapeDtypeStruct((m, d), F32),
        compiler_params=_cparams(("parallel",)),
        name="proj_residual",
    )(h, *outs, *ws)


def _ffn_kernel(h_ref, nw_ref, wg_ref, wu_ref, wd_ref, fw_ref, o_ref, hn_ref, acc_ref, *, final):
    j = pl.program_id(1)

    @pl.when(j == 0)
    def _():
        hn_ref[...] = _rms(h_ref[...], nw_ref[...]).astype(BF16)
        acc_ref[...] = jnp.zeros_like(acc_ref)

    hn = hn_ref[...]
    g = jnp.dot(hn, wg_ref[...], preferred_element_type=F32)
    u = jnp.dot(hn, wu_ref[...], preferred_element_type=F32)
    a = (jax.nn.silu(g) * u).astype(BF16)
    acc_ref[...] += jnp.dot(a, wd_ref[...], preferred_element_type=F32)

    @pl.when(j == pl.num_programs(1) - 1)
    def _():
        y = h_ref[...] + acc_ref[...]
        if final:
            y = _rms(y, fw_ref[...])
        o_ref[...] = y


def _ffn(h, nw, wg, wu, wd, fw, final, tm, tf):
    m, d = h.shape
    ff = wg.shape[1]
    return pl.pallas_call(
        functools.partial(_ffn_kernel, final=final),
        grid=(m // tm, ff // tf),
        in_specs=[pl.BlockSpec((tm, d), lambda i, j: (i, 0)),
                  pl.BlockSpec((1, d), lambda i, j: (0, 0)),
                  pl.BlockSpec((d, tf), lambda i, j: (0, j)),
                  pl.BlockSpec((d, tf), lambda i, j: (0, j)),
                  pl.BlockSpec((tf, d), lambda i, j: (j, 0)),
                  pl.BlockSpec((1, d), lambda i, j: (0, 0))],
        out_specs=pl.BlockSpec((tm, d), lambda i, j: (i, 0)),
        out_shape=jax.ShapeDtypeStruct((m, d), F32),
        scratch_shapes=[pltpu.VMEM((tm, d), BF16), pltpu.VMEM((tm, d), F32)],
        compiler_params=_cparams(("parallel", "arbitrary")),
        name="ffn",
    )(h, nw.reshape(1, d), wg, wu, wd, fw.reshape(1, d))


def _band_bias(table, max_dist, dist_scale, nb):
    kw = (nb + 1) * BLK
    rel = jnp.arange(BLK)[:, None] + nb * BLK - jnp.arange(kw)[None, :]
    band = (rel >= 0) & (rel <= max_dist)
    bias = jnp.transpose(table.astype(F32)[_t5_bucket(rel * dist_scale)], (2, 0, 1))
    return jnp.where(band[None], bias, NEG_INF)


def _swap_halves(x):
    return jnp.concatenate([x[:, HEAD_DIM:], x[:, :HEAD_DIM]], axis=1)


def _banded_kernel(*refs, n_heads, n_groups, nb, has_sinks, want_lse):
    pos = 0
    if has_sinks:
        sink_ref = refs[0]
        pos = 1
    q_ref = refs[pos]
    k_refs = refs[pos + 1:pos + 2 + nb]
    v_refs = refs[pos + 2 + nb:pos + 3 + 2 * nb]
    bias_ref = refs[pos + 3 + 2 * nb]
    o_ref = refs[pos + 4 + 2 * nb]
    lse_ref = refs[pos + 5 + 2 * nb] if want_lse else None

    i = pl.program_id(2)
    kw = (nb + 1) * BLK
    hpg = n_heads // n_groups
    kcat = jnp.concatenate([k_refs[nb - jj][0] for jj in range(nb + 1)], axis=0)
    vcat = jnp.concatenate([v_refs[nb - jj][0] for jj in range(nb + 1)], axis=0)
    lane = lax.broadcasted_iota(jnp.int32, (1, LANES), 1)
    lo = lane < HEAD_DIM
    half_mask = (jnp.where(lo, 1.0, 0.0).astype(BF16), jnp.where(lo, 0.0, 1.0).astype(BF16))
    col = lax.broadcasted_iota(jnp.int32, (1, kw), 1)
    edge = jnp.where(col < (nb - i) * BLK, NEG_INF, 0.0).astype(F32)

    if hpg > 1:
        k_sw = _swap_halves(kcat)
        v_sw = _swap_halves(vcat)

    for pair in range(n_heads // 2):
        qp = q_ref[0, :, pair * LANES:(pair + 1) * LANES]
        ms, dens, outs = [], [], []
        for par in range(2):
            h = 2 * pair + par
            if hpg == 1:
                kh = kcat[:, pair * LANES:(pair + 1) * LANES]
                vh = vcat[:, pair * LANES:(pair + 1) * LANES]
            else:
                g = h // hpg
                kh = kcat if g == par else k_sw
                vh = vcat if g == par else v_sw
            qz = qp * half_mask[par]
            s = lax.dot_general(qz, kh, (((1,), (1,)), ((), ())), preferred_element_type=F32)
            s = s * (HEAD_DIM ** -0.5) + bias_ref[h] + edge
            m = jnp.max(s, axis=-1, keepdims=True)
            if has_sinks:
                m = jnp.maximum(m, sink_ref[h])
            p = jnp.exp(s - m)
            den = jnp.sum(p, axis=-1, keepdims=True)
            if has_sinks:
                den = den + jnp.exp(sink_ref[h] - m)
            outs.append(jnp.dot(p.astype(BF16), vh, preferred_element_type=F32))
            ms.append(m)
            dens.append(den)
        inv = jnp.where(lo, 1.0 / dens[0], 1.0 / dens[1])
        o_pair = jnp.where(lo, outs[0], outs[1]) * inv
        o_ref[0, :, pair * LANES:(pair + 1) * LANES] = o_pair.astype(o_ref.dtype)
        if want_lse:
            lse_ref[0, :, pair * LANES:(pair + 1) * LANES] = jnp.where(
                lo, ms[0] + jnp.log(dens[0]), ms[1] + jnp.log(dens[1]))


def _banded_attention(slab, *, seq, dil, width, q_off, k_off, v_off, n_heads, n_groups,
                      max_dist, table, sinks=None, want_lse=False, out_dtype=BF16):
    b = slab.shape[0]
    length = seq // dil
    n_blk = length // BLK
    nb = -(-max_dist // BLK)
    kw = (nb + 1) * BLK
    hd = n_heads * HEAD_DIM
    gd = n_groups * HEAD_DIM
    view = slab.reshape(b, length, dil * width)
    bias = _band_bias(table, max_dist, dil, nb)

    def q_map(bi, r, i):
        return (bi, i, (r * width + q_off) // hd)

    def kv_map(off, j):
        return lambda bi, r, i: (bi, jnp.maximum(i - j, 0), (r * width + off) // gd)

    in_specs, args = [], []
    if sinks is not None:
        in_specs.append(pl.BlockSpec(memory_space=pltpu.SMEM))
        args.append(sinks.astype(F32))
    in_specs.append(pl.BlockSpec((1, BLK, hd), q_map))
    args.append(view)
    for off in (k_off, v_off):
        for j in range(nb + 1):
            in_specs.append(pl.BlockSpec((1, BLK, gd), kv_map(off, j)))
            args.append(view)
    in_specs.append(pl.BlockSpec((n_heads, BLK, kw), lambda bi, r, i: (0, 0, 0)))
    args.append(bias)

    out_spec = pl.BlockSpec((1, BLK, hd), lambda bi, r, i: (bi, i, r))
    out_shape = [jax.ShapeDtypeStruct((b, length, dil * hd), out_dtype)]
    out_specs = [out_spec]
    if want_lse:
        out_shape.append(jax.ShapeDtypeStruct((b, length, dil * hd), F32))
        out_specs.append(out_spec)

    res = pl.pallas_call(
        functools.partial(_banded_kernel, n_heads=n_heads, n_groups=n_groups, nb=nb,
                          has_sinks=sinks is not None, want_lse=want_lse),
        grid=(b, dil, n_blk),
        in_specs=in_specs,
        out_specs=out_specs,
        out_shape=out_shape,
        compiler_params=_cparams(("parallel", "parallel", "arbitrary")),
        name="banded_attention",
    )(*args)
    return [r.reshape(b, seq, hd) for r in res]


def _dilated_mix_kernel(o0, o1, o2, l0, l1, l2, out_ref):
    ls = [l0[...], l1[...], l2[...]]
    m = jnp.maximum(jnp.maximum(ls[0], ls[1]), ls[2])
    es = [jnp.exp(l - m) for l in ls]
    den = es[0] + es[1] + es[2]
    acc = (es[0] / den) * o0[...] + (es[1] / den) * o1[...] + (es[2] / den) * o2[...]
    out_ref[...] = acc.astype(out_ref.dtype)


def _dilated_mix(outs, lses, tm):
    m, c = outs[0].shape
    spec = pl.BlockSpec((tm, c), lambda i: (i, 0))
    return pl.pallas_call(
        _dilated_mix_kernel,
        grid=(m // tm,),
        in_specs=[spec] * 6,
        out_specs=spec,
        out_shape=jax.ShapeDtypeStruct((m, c), BF16),
        compiler_params=_cparams(("parallel",)),
        name="dilated_mix",
    )(*outs, *lses)


def _retention_kernel(q_ref, k_ref, v_ref, g_ref, cos_ref, sin_ref, dmask_ref, qdec_ref, kdec_ref,
                      cdec_ref, gn_ref, o_ref, state_ref):
    hd = pl.program_id(1)

    @pl.when(pl.program_id(2) == 0)
    def _():
        state_ref[...] = jnp.zeros_like(state_ref)

    cos = cos_ref[...]
    sin = sin_ref[...]
    half = C_QK_DIM // 2

    def rot(x):
        x1, x2 = x[:, :half], x[:, half:]
        return jnp.concatenate([x1 * cos - x2 * sin, x1 * sin + x2 * cos], axis=1)

    q = rot(q_ref[0].astype(F32))
    k = rot(k_ref[0].astype(F32)) * (C_QK_DIM ** -0.5)
    v = v_ref[0]
    qb = q.astype(BF16)
    inner = lax.dot_general(qb, k.astype(BF16), (((1,), (1,)), ((), ())),
                            preferred_element_type=F32) * dmask_ref[0]
    state = state_ref[...]
    o = jnp.dot(inner.astype(BF16), v, preferred_element_type=F32)
    o = o + jnp.dot(qb, state.astype(BF16), preferred_element_type=F32) * qdec_ref[0]
    kd_t = jnp.transpose(k * kdec_ref[0]).astype(BF16)
    state_ref[...] = state * cdec_ref[hd] + jnp.dot(kd_t, v, preferred_element_type=F32)

    mu = jnp.mean(o, axis=-1, keepdims=True)
    oc = o - mu
    var = jnp.mean(oc * oc, axis=-1, keepdims=True)
    on = oc * lax.rsqrt(var + 1e-5)
    o_ref[0] = (on * gn_ref[...] * jax.nn.silu(g_ref[0].astype(F32))).astype(o_ref.dtype)


def _retention(slab, gn, seq):
    b = slab.shape[0]
    n_chunks = seq // C_CHUNK
    half = C_QK_DIM // 2
    pos = jnp.arange(seq, dtype=F32)
    inv = 1.0 / (10000.0 ** (jnp.arange(0, C_QK_DIM, 2, dtype=F32) / C_QK_DIM))
    ang = pos[:, None] * inv[None, :]
    cos, sin = jnp.cos(ang), jnp.sin(ang)
    log_g = jnp.log(1.0 - 2.0 ** (-5.0 - jnp.arange(C_HEADS, dtype=F32)))
    j = jnp.arange(C_CHUNK, dtype=F32)
    diff = j[:, None] - j[None, :]
    dmask = jnp.where(diff >= 0, jnp.exp(diff[None] * log_g[:, None, None]), 0.0)
    q_dec = jnp.exp((j[None, :] + 1.0) * log_g[:, None])[:, :, None]
    k_dec = jnp.exp((C_CHUNK - 1.0 - j[None, :]) * log_g[:, None])[:, :, None]
    chunk_dec = jnp.exp(C_CHUNK * log_g)

    def col(off, w):
        return lambda bi, h, c: (bi, c, off // w + h)

    return pl.pallas_call(
        _retention_kernel,
        grid=(b, C_HEADS, n_chunks),
        in_specs=[pl.BlockSpec((1, C_CHUNK, C_QK_DIM), col(OD_QC, C_QK_DIM)),
                  pl.BlockSpec((1, C_CHUNK, C_QK_DIM), col(OD_KC, C_QK_DIM)),
                  pl.BlockSpec((1, C_CHUNK, C_V_DIM), col(OD_VC, C_V_DIM)),
                  pl.BlockSpec((1, C_CHUNK, C_V_DIM), col(OD_GC, C_V_DIM)),
                  pl.BlockSpec((C_CHUNK, half), lambda bi, h, c: (c, 0)),
                  pl.BlockSpec((C_CHUNK, half), lambda bi, h, c: (c, 0)),
                  pl.BlockSpec((1, C_CHUNK, C_CHUNK), lambda bi, h, c: (h, 0, 0)),
                  pl.BlockSpec((1, C_CHUNK, 1), lambda bi, h, c: (h, 0, 0)),
                  pl.BlockSpec((1, C_CHUNK, 1), lambda bi, h, c: (h, 0, 0)),
                  pl.BlockSpec(memory_space=pltpu.SMEM),
                  pl.BlockSpec((1, C_V_DIM), lambda bi, h, c: (0, h))],
        out_specs=pl.BlockSpec((1, C_CHUNK, C_V_DIM), lambda bi, h, c: (bi, c, h)),
        out_shape=jax.ShapeDtypeStruct((b, seq, C_V), BF16),
        scratch_shapes=[pltpu.VMEM((C_QK_DIM, C_V_DIM), F32)],
        compiler_params=_cparams(("parallel", "parallel", "arbitrary")),
        name="retention",
    )(slab, slab, slab, slab, cos, sin, dmask, q_dec, k_dec, chunk_dec, gn.reshape(1, C_V).astype(F32))


def _compress_kernel(x_ref, pos_ref, w1_ref, w2_ref, o_ref):
    x = (x_ref[0].astype(F32) + pos_ref[0]).astype(BF16)
    hid = jax.nn.gelu(jnp.dot(x, w1_ref[0], preferred_element_type=F32))
    o_ref[0] = jnp.dot(hid.astype(BF16), w2_ref[0], preferred_element_type=F32)


def _compress(flat, pos, w1, w2):
    _, bg, rows, width = flat.shape
    tr = min(rows, 256)
    return pl.pallas_call(
        _compress_kernel,
        grid=(2, bg, rows // tr),
        in_specs=[pl.BlockSpec((None, 1, tr, width), lambda s, i, r: (s, i, r, 0)),
                  pl.BlockSpec((1, 1, width), lambda s, i, r: (s, 0, 0)),
                  pl.BlockSpec((1, width, D_CMP_HIDDEN), lambda s, i, r: (s, 0, 0)),
                  pl.BlockSpec((1, D_CMP_HIDDEN, HEAD_DIM), lambda s, i, r: (s, 0, 0))],
        out_specs=pl.BlockSpec((None, 1, tr, HEAD_DIM), lambda s, i, r: (s, i, r, 0)),
        out_shape=jax.ShapeDtypeStruct((2, bg, rows, HEAD_DIM), F32),
        compiler_params=_cparams(("parallel", "parallel", "parallel")),
        name="nsa_compress",
    )(flat, pos, w1, w2)


def _cmp_bias(table):
    tt = jnp.arange(BLK)[:, None]
    m = jnp.arange(CMP_WIN)[None, :] + (BLK // D_CMP_STRIDE) - CMP_WIN
    dist = tt - D_CMP_STRIDE * m - (D_CMP_LEN - 1)
    bias = jnp.transpose(table.astype(F32)[_t5_bucket(dist)], (2, 0, 1))
    return jnp.where((dist >= 0)[None], bias, NEG_INF)


def _sel_matrix():
    c_rel = np.arange(CMP_WIN)[:, None] + (BLK // D_CMP_STRIDE) - CMP_WIN
    j_rel = np.arange(REL_BLOCKS)[None, :] - (REL_BLOCKS - 2)
    return ((c_rel >= 4 * j_rel - 1) & (c_rel <= 4 * j_rel + 3)).astype(np.float32)


def _cmp_attn_kernel(q_ref, kc_ref, vc_ref, bias_ref, sel_ref, o_ref, idx_ref):
    i = pl.program_id(1)
    hpg = D_HEADS // D_KV_HEADS
    lane = lax.broadcasted_iota(jnp.int32, (1, LANES), 1)
    lo = lane < HEAD_DIM
    half_mask = (jnp.where(lo, 1.0, 0.0).astype(BF16), jnp.where(lo, 0.0, 1.0).astype(BF16))
    start = pl.multiple_of(8 * i + 8, 8)
    ucol = lax.broadcasted_iota(jnp.int32, (1, CMP_WIN), 1)
    edge = jnp.where(ucol < CMP_WIN - 8 - 8 * i, NEG_INF, 0.0).astype(F32)

    row = lax.broadcasted_iota(jnp.int32, (BLK, REL_BLOCKS), 0)
    jj = lax.broadcasted_iota(jnp.int32, (BLK, REL_BLOCKS), 1)
    cur = (REL_BLOCKS - 2) + (row >= D_SEL_LEN).astype(jnp.int32)
    first = (REL_BLOCKS - 2) - 2 * i
    exists = jj >= first
    forced = exists & ((jj == first) | (jj == cur) | (jj == cur - 1))
    valid = exists & (jj <= cur)
    jjf = jj.astype(F32)
    out_lane = lax.broadcasted_iota(jnp.int32, (BLK, LANES), 1)

    for g in range(D_KV_HEADS):
        kwin = kc_ref[0, g, pl.ds(start, CMP_WIN), :].astype(BF16)
        vwin = vc_ref[0, g, pl.ds(start, CMP_WIN), :].astype(BF16)
        imp = jnp.zeros((BLK, CMP_WIN), F32)
        outs = []
        for hh in range(hpg):
            h = g * hpg + hh
            pair, par = h // 2, h % 2
            qz = q_ref[0, :, pair * LANES:(pair + 1) * LANES] * half_mask[par]
            s = lax.dot_general(qz, kwin, (((1,), (1,)), ((), ())), preferred_element_type=F32)
            s = s * (HEAD_DIM ** -0.5) + bias_ref[h] + edge
            ok = s > 0.5 * NEG_INF
            m = jnp.max(s, axis=-1, keepdims=True)
            p = jnp.where(ok, jnp.exp(s - m), 0.0)
            den = jnp.maximum(jnp.sum(p, axis=-1, keepdims=True), 1e-30)
            pn = p * (1.0 / den)
            imp = imp + pn
            outs.append(jnp.dot(pn.astype(BF16), vwin, preferred_element_type=F32))
        for pr in range(hpg // 2):
            pair = (g * hpg) // 2 + pr
            o_ref[0, :, pair * LANES:(pair + 1) * LANES] = jnp.where(
                lo, outs[2 * pr], outs[2 * pr + 1]).astype(o_ref.dtype)

        sel = sel_ref[...]
        hi = imp.astype(BF16)
        r1 = imp - hi.astype(F32)
        mid = r1.astype(BF16)
        low = (r1 - mid.astype(F32)).astype(BF16)
        imp_sel = (jnp.dot(hi, sel, preferred_element_type=F32)
                   + jnp.dot(mid, sel, preferred_element_type=F32)
                   + jnp.dot(low, sel, preferred_element_type=F32))
        score = jnp.where(forced, 1e9, jnp.where(valid, imp_sel, -1e9))
        picked = jnp.full((BLK, LANES), -1, jnp.int32)
        for r in range(D_SEL_COUNT):
            m = jnp.max(score, axis=-1, keepdims=True)
            am = jnp.min(jnp.where(score == m, jjf, float(REL_BLOCKS)), axis=-1, keepdims=True)
            blk = jnp.where(m > -5e8, am.astype(jnp.int32) - first, -1)
            picked = jnp.where(out_lane == r, blk, picked)
            score = jnp.where(jjf == am, -jnp.inf, score)
        idx_ref[0, g] = picked


def _cmp_attention(slab, kc_pad, vc_pad, table, seq):
    b = slab.shape[0]
    n_blk = seq // BLK
    rows = kc_pad.shape[2]
    bias = _cmp_bias(table)
    sel = jnp.asarray(_sel_matrix(), BF16)
    return pl.pallas_call(
        _cmp_attn_kernel,
        grid=(b, n_blk),
        in_specs=[pl.BlockSpec((1, BLK, D_Q), lambda bi, i: (bi, i, OD_QD // D_Q)),
                  pl.BlockSpec((1, D_KV_HEADS, rows, LANES), lambda bi, i: (bi, 0, 0, 0)),
                  pl.BlockSpec((1, D_KV_HEADS, rows, LANES), lambda bi, i: (bi, 0, 0, 0)),
                  pl.BlockSpec((D_HEADS, BLK, CMP_WIN), lambda bi, i: (0, 0, 0)),
                  pl.BlockSpec((CMP_WIN, REL_BLOCKS), lambda bi, i: (0, 0))],
        out_specs=[pl.BlockSpec((1, BLK, D_Q), lambda bi, i: (bi, i, 0)),
                   pl.BlockSpec((1, D_KV_HEADS, BLK, LANES), lambda bi, i: (bi, 0, i, 0))],
        out_shape=[jax.ShapeDtypeStruct((b, seq, D_Q), BF16),
                   jax.ShapeDtypeStruct((b, D_KV_HEADS, seq, LANES), jnp.int32)],
        compiler_params=_cparams(("parallel", "arbitrary")),
        name="nsa_cmp_attention",
    )(slab, kc_pad, vc_pad, bias, sel)


def _sel_bias(table):
    hpg = D_HEADS // D_KV_HEADS
    delta = jnp.arange(SEL_FAR)[:, None, None]
    tt = jnp.arange(D_SEL_LEN)[None, :, None]
    l = jnp.arange(D_SEL_LEN)[None, None, :]
    dist = D_SEL_LEN * delta + tt - l
    tab = table.astype(F32)
    near = jnp.where((dist >= 0)[..., None], tab[_t5_bucket(dist)], NEG_INF)
    far = jnp.broadcast_to(tab[REL_BUCKETS - 1], (1, D_SEL_LEN, D_SEL_LEN, D_HEADS))
    none = jnp.full((1, D_SEL_LEN, D_SEL_LEN, D_HEADS), NEG_INF, F32)
    rows = jnp.concatenate([near, far, none], axis=0)
    rows = rows.reshape(SEL_NONE + 1, D_SEL_LEN, D_SEL_LEN, D_KV_HEADS, hpg)
    rows = jnp.transpose(rows, (3, 0, 1, 4, 2))
    return jnp.concatenate([rows, rows], axis=-1)


def _sel_attn_kernel(idx_hbm, q_ref, k_ref, v_ref, bias_ref, o_ref, idx_smem, sem):
    bi, g, i = pl.program_id(0), pl.program_id(1), pl.program_id(2)
    copy = pltpu.make_async_copy(idx_hbm.at[bi, g, i], idx_smem, sem)
    copy.start()
    copy.wait()
    lane = lax.broadcasted_iota(jnp.int32, (1, LANES), 1)
    lo = lane < HEAD_DIM
    n_pairs = D_SEL_COUNT // 2

    def body(tl, carry):
        cur = 2 * i + tl // D_SEL_LEN
        tt = tl % D_SEL_LEN
        q = q_ref[0, 0, tl].astype(BF16)
        ks, vs, bs = [], [], []
        for n in range(D_SEL_COUNT):
            j = idx_smem[tl * D_SEL_COUNT + n]
            jc = jnp.maximum(j, 0)
            off = pl.multiple_of(jc * D_SEL_LEN, D_SEL_LEN)
            ks.append(k_ref[0, 0, pl.ds(off, D_SEL_LEN), :])
            vs.append(v_ref[0, 0, pl.ds(off, D_SEL_LEN), :])
            rowi = jnp.where(j < 0, SEL_NONE, jnp.minimum(cur - jc, SEL_FAR))
            bs.append(bias_ref[0, rowi, tt])
        kall = jnp.concatenate(ks, axis=0)
        vall = jnp.concatenate(vs, axis=0)
        s = lax.dot_general(q, kall, (((1,), (1,)), ((), ())), preferred_element_type=F32)
        bias = jnp.concatenate([jnp.where(lo, bs[2 * pr], bs[2 * pr + 1]) for pr in range(n_pairs)], axis=1)
        s = s * (HEAD_DIM ** -0.5) + bias
        m = jnp.max(s, axis=-1, keepdims=True)
        p = jnp.exp(s - m)
        den = jnp.sum(p, axis=-1, keepdims=True)
        o = jnp.dot(p.astype(BF16), vall, preferred_element_type=F32) * (1.0 / den)
        o_ref[0, 0, tl] = o.astype(o_ref.dtype)
        return carry

    lax.fori_loop(0, BLK, body, 0)


def _sel_attention(idx, q_sel, k_sel, v_sel, table, seq):
    b = q_sel.shape[0]
    hpg = D_HEADS // D_KV_HEADS
    n_blk = seq // BLK
    bias = _sel_bias(table)
    return pl.pallas_call(
        _sel_attn_kernel,
        grid=(b, D_KV_HEADS, n_blk),
        in_specs=[pl.BlockSpec(memory_space=pl.ANY),
                  pl.BlockSpec((1, 1, BLK, hpg, HEAD_DIM), lambda bi, g, i: (bi, g, i, 0, 0)),
                  pl.BlockSpec((1, 1, seq, HEAD_DIM), lambda bi, g, i: (bi, g, 0, 0)),
                  pl.BlockSpec((1, 1, seq, HEAD_DIM), lambda bi, g, i: (bi, g, 0, 0)),
                  pl.BlockSpec((1, SEL_NONE + 1, D_SEL_LEN, hpg, LANES), lambda bi, g, i: (g, 0, 0, 0, 0))],
        out_specs=pl.BlockSpec((1, 1, BLK, hpg, HEAD_DIM), lambda bi, g, i: (bi, g, i, 0, 0)),
        out_shape=jax.ShapeDtypeStruct((b, D_KV_HEADS, seq, hpg, HEAD_DIM), F32),
        scratch_shapes=[pltpu.SMEM((BLK * D_SEL_COUNT,), jnp.int32), pltpu.SemaphoreType.DMA],
        compiler_params=_cparams(("parallel", "parallel", "arbitrary")),
        name="nsa_sel_attention",
    )(idx, q_sel, k_sel, v_sel, bias)


def _gate_expand():
    e = np.zeros((3, LANES, D_Q), np.float32)
    for h in range(D_HEADS):
        for c in range(3):
            e[c, 3 * h + c, h * HEAD_DIM:(h + 1) * HEAD_DIM] = 1.0
    return e


def _nsa_gate_kernel(gd_ref, e_ref, oc_ref, os_ref, ow_ref, out_ref):
    sg = jax.nn.sigmoid(gd_ref[...].astype(F32))
    hi = sg.astype(BF16)
    low = (sg - hi.astype(F32)).astype(BF16)
    acc = None
    for c, o_ref in enumerate((oc_ref, os_ref, ow_ref)):
        gate = (jnp.dot(hi, e_ref[c], preferred_element_type=F32)
                + jnp.dot(low, e_ref[c], preferred_element_type=F32))
        term = gate * o_ref[...].astype(F32)
        acc = term if acc is None else acc + term
    out_ref[...] = acc.astype(out_ref.dtype)


def _nsa_gate(slab2d, o_c, o_s, o_w, tm):
    m = slab2d.shape[0]
    e = jnp.asarray(_gate_expand(), BF16)
    spec = pl.BlockSpec((tm, D_Q), lambda i: (i, 0))
    return pl.pallas_call(
        _nsa_gate_kernel,
        grid=(m // tm,),
        in_specs=[pl.BlockSpec((tm, LANES), lambda i: (i, OD_GD // LANES)),
                  pl.BlockSpec((3, LANES, D_Q), lambda i: (0, 0, 0)),
                  spec, spec, spec],
        out_specs=spec,
        out_shape=jax.ShapeDtypeStruct((m, D_Q), BF16),
        compiler_params=_cparams(("parallel",)),
        name="nsa_gate",
    )(slab2d, e, o_c, o_s, o_w)


def _even_mixer(h, nw, w_in, sinks, w_out, rel_table, b, seq):
    qa, ka, va, qb, kb, vb = jnp.split(w_in, [int(c) for c in np.cumsum([A_Q, A_KV, A_KV, B_W, B_W])], axis=1)
    pad = jnp.zeros((D_MODEL, EVEN_SLAB - w_in.shape[1]), w_in.dtype)
    w_slab = jnp.concatenate([qa, qb, kb, vb, ka, va, pad], axis=1).astype(BF16)
    slab = _norm_matmul(h, nw, w_slab, 1024, 512).reshape(b, seq, EVEN_SLAB)

    (oa,) = _banded_attention(slab, seq=seq, dil=1, width=EVEN_SLAB, q_off=EV_QA, k_off=EV_KA, v_off=EV_VA,
                              n_heads=A_HEADS, n_groups=A_KV_HEADS, max_dist=A_WINDOW - 1,
                              table=rel_table[:, :A_HEADS], sinks=sinks)
    outs, lses = [], []
    for window, dil in B_PATTERNS:
        o, lse = _banded_attention(slab, seq=seq, dil=dil, width=EVEN_SLAB, q_off=EV_QB, k_off=EV_KB,
                                   v_off=EV_VB, n_heads=B_HEADS, n_groups=B_HEADS, max_dist=window // dil,
                                   table=rel_table[:, A_HEADS:A_HEADS + B_HEADS], want_lse=True, out_dtype=F32)
        outs.append(o.reshape(b * seq, B_W))
        lses.append(lse.reshape(b * seq, B_W))
    ob = _dilated_mix(outs, lses, 1024)
    w_out = w_out.astype(BF16)
    return _proj_residual(h, [oa.reshape(b * seq, A_Q), ob], [w_out[:A_Q], w_out[A_Q:]], 512)


def _nsa(slab, pos_k, pos_v, k_w1, k_w2, v_w1, v_w2, rel_table, b, seq):
    g_kv = D_KV_HEADS
    hpg = D_HEADS // g_kv
    slab2d = slab.reshape(b * seq, ODD_SLAB)
    n_rows = seq // D_CMP_STRIDE

    def rows16(off):
        a = slab[:, :, off:off + D_KV].reshape(b, n_rows, D_CMP_STRIDE, g_kv, HEAD_DIM)
        a = jnp.transpose(a, (0, 3, 1, 2, 4)).reshape(b * g_kv, n_rows, D_CMP_STRIDE * HEAD_DIM)
        nxt = jnp.concatenate([a[:, 1:], jnp.zeros_like(a[:, :1])], axis=1)
        return jnp.concatenate([a, nxt], axis=-1)

    flat = jnp.stack([rows16(OD_KCMP), rows16(OD_VCMP)])
    pos = jnp.stack([pos_k.reshape(1, -1), pos_v.reshape(1, -1)]).astype(F32)
    w1 = jnp.stack([k_w1, v_w1]).astype(BF16)
    w2 = jnp.stack([k_w2, v_w2]).astype(BF16)
    cmp = _compress(flat, pos, w1, w2).reshape(2, b, g_kv, n_rows, HEAD_DIM)
    cmp = jnp.pad(cmp, ((0, 0), (0, 0), (0, 0), (CMP_WIN, 0), (0, 0)))
    cmp = jnp.concatenate([cmp, cmp], axis=-1)
    o_c, idx = _cmp_attention(slab, cmp[0], cmp[1], rel_table, seq)

    idx = idx[..., :D_SEL_COUNT].reshape(b, g_kv, seq // BLK, BLK * D_SEL_COUNT)

    def per_group(off):
        a = slab[:, :, off:off + D_KV].reshape(b, seq, g_kv, HEAD_DIM)
        return jnp.transpose(a, (0, 2, 1, 3))

    q_sel = slab[:, :, OD_QD:OD_QD + D_Q].reshape(b, seq, g_kv, hpg, HEAD_DIM)
    q_sel = jnp.transpose(q_sel, (0, 2, 1, 3, 4)).astype(F32)
    o_s = _sel_attention(idx, q_sel, per_group(OD_KSLC), per_group(OD_VSLC), rel_table, seq)
    o_s = jnp.transpose(o_s, (0, 2, 1, 3, 4)).reshape(b * seq, D_Q).astype(BF16)

    (o_w,) = _banded_attention(slab, seq=seq, dil=1, width=ODD_SLAB, q_off=OD_QD, k_off=OD_KWIN, v_off=OD_VWIN,
                               n_heads=D_HEADS, n_groups=D_KV_HEADS, max_dist=D_WINDOW - 1, table=rel_table)
    return _nsa_gate(slab2d, o_c.reshape(b * seq, D_Q), o_s, o_w.reshape(b * seq, D_Q), 1024)


def _odd_mixer(h, nw, w_in, ret_gn, pos_k, pos_v, k_w1, k_w2, v_w1, v_w2, w_out, rel_table, b, seq):
    pad = jnp.zeros((D_MODEL, ODD_SLAB - ODD_IN), w_in.dtype)
    w_slab = jnp.concatenate([w_in, pad], axis=1).astype(BF16)
    slab = _norm_matmul(h, nw, w_slab, 1024, 512).reshape(b, seq, ODD_SLAB)
    oc = _retention(slab, ret_gn, seq)
    od = _nsa(slab, pos_k, pos_v, k_w1, k_w2, v_w1, v_w2, rel_table, b, seq)
    w_out = w_out.astype(BF16)
    return _proj_residual(h, [oc.reshape(b * seq, C_V), od], [w_out[:C_V], w_out[C_V:]], 512)


def kernel(x, rel_table, norm_mix, norm_ffn, norm_final, even_w_in, even_sinks, even_w_out, odd_w_in, odd_ret_gn, odd_cmp_pos_k, odd_cmp_pos_v, odd_cmp_k_w1, odd_cmp_k_w2, odd_cmp_v_w1, odd_cmp_v_w2, odd_w_out, ffn_w_gate, ffn_w_up, ffn_w_down):
    b, seq, d = x.shape
    h = x.reshape(b * seq, d)
    for layer in range(DEPTH):
        li = layer // 2
        if layer % 2 == 0:
            h = _even_mixer(h, norm_mix[layer], even_w_in[li], even_sinks[li], even_w_out[li], rel_table, b, seq)
        else:
            h = _odd_mixer(h, norm_mix[layer], odd_w_in[li], odd_ret_gn[li], odd_cmp_pos_k[li],
                           odd_cmp_pos_v[li], odd_cmp_k_w1[li], odd_cmp_k_w2[li], odd_cmp_v_w1[li],
                           odd_cmp_v_w2[li], odd_w_out[li], rel_table, b, seq)
        h = _ffn(h, norm_ffn[layer], ffn_w_gate[layer].astype(BF16), ffn_w_up[layer].astype(BF16),
                 ffn_w_down[layer].astype(BF16), norm_final, layer == DEPTH - 1, 1024, 256)
    return h.reshape(b, seq, d)
```

```python
import functools
import math

import numpy as np
import jax
import jax.numpy as jnp
from jax import lax
from jax.experimental import pallas as pl
from jax.experimental.pallas import tpu as pltpu

F32 = jnp.float32
BF16 = jnp.bfloat16

D_MODEL = 1024
DEPTH = 4
HEAD_DIM = 64
BLK = 128
NEG_INF = -1e30
REL_BUCKETS = 32
REL_MAX_DIST = 2048
A_HEADS = 8
A_KV_HEADS = 2
A_WINDOW = 128
B_HEADS = 8
B_PATTERNS = ((128, 1), (512, 4), (2048, 16))
C_HEADS = 4
C_QK_DIM = 256
C_V_DIM = 512
C_CHUNK = 128
D_HEADS = 16
D_KV_HEADS = 2
D_CMP_LEN = 32
D_CMP_STRIDE = 16
D_CMP_HIDDEN = 128
D_SEL_LEN = 64
D_SEL_COUNT = 16
D_WINDOW = 512
D_FF = 2816

A_Q = A_HEADS * HEAD_DIM
A_KV = A_KV_HEADS * HEAD_DIM
B_W = B_HEADS * HEAD_DIM
C_QK = C_HEADS * C_QK_DIM
C_V = C_HEADS * C_V_DIM
D_Q = D_HEADS * HEAD_DIM
D_KV = D_KV_HEADS * HEAD_DIM
ODD_IN = 2 * C_QK + 2 * C_V + D_Q + 6 * D_KV + 3 * D_HEADS

LANES = 128
VMEM_LIMIT = 56 * 1024 * 1024

EVEN_SLAB = 2560
EV_QA, EV_QB, EV_KB, EV_VB, EV_KA, EV_VA = 0, 512, 1024, 1536, 2048, 2176
ODD_SLAB = 8192
OD_QC, OD_KC, OD_VC, OD_GC, OD_QD = 0, 1024, 2048, 4096, 6144
OD_KCMP, OD_VCMP, OD_KSLC, OD_VSLC, OD_KWIN, OD_VWIN, OD_GD = 7168, 7296, 7424, 7552, 7680, 7808, 7936

SEL_FAR = 25
SEL_NONE = 26
CMP_WIN = 1024
REL_BLOCKS = 256


def _cparams(sem):
    return pltpu.CompilerParams(dimension_semantics=sem, vmem_limit_bytes=VMEM_LIMIT)


def _t5_bucket(dist):
    max_exact = REL_BUCKETS // 2
    d = jnp.maximum(dist, 0)
    df = jnp.maximum(d, 1).astype(jnp.float32)
    large = max_exact + (jnp.log(df / max_exact) / math.log(REL_MAX_DIST / max_exact)
                         * (REL_BUCKETS - max_exact)).astype(jnp.int32)
    large = jnp.minimum(large, REL_BUCKETS - 1)
    return jnp.where(d < max_exact, d, large)


def _bias_lookup(table, dist):
    bucket = _t5_bucket(dist)[None]
    tab = table.astype(F32)
    expand = (slice(None),) + (None,) * dist.ndim
    out = jnp.zeros((tab.shape[1],) + dist.shape, F32)
    for b in range(REL_BUCKETS):
        out = jnp.where(bucket == b, tab[b][expand], out)
    return out


def _rms(x, w, eps=1e-6):
    return x * lax.rsqrt(jnp.mean(x * x, axis=-1, keepdims=True) + eps) * w


def _norm_matmul_kernel(h_ref, nw_ref, w_ref, o_ref, hn_ref):
    @pl.when(pl.program_id(1) == 0)
    def _():
        hn_ref[...] = _rms(h_ref[...], nw_ref[...]).astype(BF16)

    o_ref[...] = jnp.dot(hn_ref[...], w_ref[...], preferred_element_type=F32).astype(o_ref.dtype)


def _norm_matmul(h, nw, w, tm, tn):
    m, d = h.shape
    n = w.shape[1]
    return pl.pallas_call(
        _norm_matmul_kernel,
        grid=(m // tm, n // tn),
        in_specs=[pl.BlockSpec((tm, d), lambda i, j: (i, 0)),
                  pl.BlockSpec((1, d), lambda i, j: (0, 0)),
                  pl.BlockSpec((d, tn), lambda i, j: (0, j))],
        out_specs=pl.BlockSpec((tm, tn), lambda i, j: (i, j)),
        out_shape=jax.ShapeDtypeStruct((m, n), BF16),
        scratch_shapes=[pltpu.VMEM((tm, d), BF16)],
        compiler_params=_cparams(("parallel", "arbitrary")),
        name="norm_matmul",
    )(h, nw.reshape(1, d), w)


def _proj_residual_kernel(*refs, n_in):
    h_ref, out_ref = refs[0], refs[-1]
    acc = h_ref[...]
    for o_ref, w_ref in zip(refs[1:1 + n_in], refs[1 + n_in:1 + 2 * n_in]):
        acc = acc + jnp.dot(o_ref[...], w_ref[...], preferred_element_type=F32)
    out_ref[...] = acc


def _proj_residual(h, outs, ws, tm):
    m, d = h.shape
    n_in = len(outs)
    in_specs = [pl.BlockSpec((tm, d), lambda i: (i, 0))]
    in_specs += [pl.BlockSpec((tm, o.shape[1]), lambda i: (i, 0)) for o in outs]
    in_specs += [pl.BlockSpec(w.shape, lambda i: (0, 0)) for w in ws]
    return pl.pallas_call(
        functools.partial(_proj_residual_kernel, n_in=n_in),
        grid=(m // tm,),
        in_specs=in_specs,
        out_specs=pl.BlockSpec((tm, d), lambda i: (i, 0)),
        out_shape=jax.ShapeDtypeStruct((m, d), F32),
        compiler_params=_cparams(("parallel",)),
        name="proj_residual",
    )(h, *outs, *ws)


def _ffn_kernel(h_ref, nw_ref, wg_ref, wu_ref, wd_ref, fw_ref, o_ref, hn_ref, acc_ref, *, final):
    j = pl.program_id(1)

    @pl.when(j == 0)
    def _():
        hn_ref[...] = _rms(h_ref[...], nw_ref[...]).astype(BF16)
        acc_ref[...] = jnp.zeros_like(acc_ref)

    hn = hn_ref[...]
    g = jnp.dot(hn, wg_ref[...], preferred_element_type=F32)
    u = jnp.dot(hn, wu_ref[...], preferred_element_type=F32)
    a = (jax.nn.silu(g) * u).astype(BF16)
    acc_ref[...] += jnp.dot(a, wd_ref[...], preferred_element_type=F32)

    @pl.when(j == pl.num_programs(1) - 1)
    def _():
        y = h_ref[...] + acc_ref[...]
        if final:
            y = _rms(y, fw_ref[...])
        o_ref[...] = y


def _ffn(h, nw, wg, wu, wd, fw, final, tm, tf):
    m, d = h.shape
    ff = wg.shape[1]
    return pl.pallas_call(
        functools.partial(_ffn_kernel, final=final),
        grid=(m // tm, ff // tf),
        in_specs=[pl.BlockSpec((tm, d), lambda i, j: (i, 0)),
                  pl.BlockSpec((1, d), lambda i, j: (0, 0)),
                  pl.BlockSpec((d, tf), lambda i, j: (0, j)),
                  pl.BlockSpec((d, tf), lambda i, j: (0, j)),
                  pl.BlockSpec((tf, d), lambda i, j: (j, 0)),
                  pl.BlockSpec((1, d), lambda i, j: (0, 0))],
        out_specs=pl.BlockSpec((tm, d), lambda i, j: (i, 0)),
        out_shape=jax.ShapeDtypeStruct((m, d), F32),
        scratch_shapes=[pltpu.VMEM((tm, d), BF16), pltpu.VMEM((tm, d), F32)],
        compiler_params=_cparams(("parallel", "arbitrary")),
        name="ffn",
    )(h, nw.reshape(1, d), wg, wu, wd, fw.reshape(1, d))


def _band_bias(table, max_dist, dist_scale, nb):
    kw = (nb + 1) * BLK
    rel = jnp.arange(BLK)[:, None] + nb * BLK - jnp.arange(kw)[None, :]
    band = (rel >= 0) & (rel <= max_dist)
    return jnp.where(band[None], _bias_lookup(table, rel * dist_scale), NEG_INF)


def _swap_halves(x):
    return jnp.concatenate([x[:, HEAD_DIM:], x[:, :HEAD_DIM]], axis=1)


def _banded_kernel(*refs, n_heads, n_groups, nb, has_sinks, want_lse):
    pos = 0
    if has_sinks:
        sink_ref = refs[0]
        pos = 1
    q_ref = refs[pos]
    k_refs = refs[pos + 1:pos + 2 + nb]
    v_refs = refs[pos + 2 + nb:pos + 3 + 2 * nb]
    bias_ref = refs[pos + 3 + 2 * nb]
    o_ref = refs[pos + 4 + 2 * nb]
    lse_ref = refs[pos + 5 + 2 * nb] if want_lse else None

    i = pl.program_id(2)
    kw = (nb + 1) * BLK
    hpg = n_heads // n_groups
    kcat = jnp.concatenate([k_refs[nb - jj][0] for jj in range(nb + 1)], axis=0)
    vcat = jnp.concatenate([v_refs[nb - jj][0] for jj in range(nb + 1)], axis=0)
    lane = lax.broadcasted_iota(jnp.int32, (1, LANES), 1)
    lo = lane < HEAD_DIM
    half_mask = (jnp.where(lo, 1.0, 0.0).astype(BF16), jnp.where(lo, 0.0, 1.0).astype(BF16))
    col = lax.broadcasted_iota(jnp.int32, (1, kw), 1)
    edge = jnp.where(col < (nb - i) * BLK, NEG_INF, 0.0).astype(F32)

    if hpg > 1:
        k_sw = _swap_halves(kcat)
        v_sw = _swap_halves(vcat)

    for pair in range(n_heads // 2):
        qp = q_ref[0, :, pair * LANES:(pair + 1) * LANES]
        ms, dens, outs = [], [], []
        for par in range(2):
            h = 2 * pair + par
            if hpg == 1:
                kh = kcat[:, pair * LANES:(pair + 1) * LANES]
                vh = vcat[:, pair * LANES:(pair + 1) * LANES]
            else:
                g = h // hpg
                kh = kcat if g == par else k_sw
                vh = vcat if g == par else v_sw
            qz = qp * half_mask[par]
            s = lax.dot_general(qz, kh, (((1,), (1,)), ((), ())), preferred_element_type=F32)
            s = s * (HEAD_DIM ** -0.5) + bias_ref[h] + edge
            m = jnp.max(s, axis=-1, keepdims=True)
            if has_sinks:
                m = jnp.maximum(m, sink_ref[h])
            p = jnp.exp(s - m)
            den = jnp.sum(p, axis=-1, keepdims=True)
            if has_sinks:
                den = den + jnp.exp(sink_ref[h] - m)
            outs.append(jnp.dot(p.astype(BF16), vh, preferred_element_type=F32))
            ms.append(m)
            dens.append(den)
        inv = jnp.where(lo, 1.0 / dens[0], 1.0 / dens[1])
        o_pair = jnp.where(lo, outs[0], outs[1]) * inv
        o_ref[0, :, pair * LANES:(pair + 1) * LANES] = o_pair.astype(o_ref.dtype)
        if want_lse:
            lse_ref[0, :, pair * LANES:(pair + 1) * LANES] = jnp.where(
                lo, ms[0] + jnp.log(dens[0]), ms[1] + jnp.log(dens[1]))


def _banded_attention(slab, *, seq, dil, width, q_off, k_off, v_off, n_heads, n_groups,
                      max_dist, table, sinks=None, want_lse=False, out_dtype=BF16):
    b = slab.shape[0]
    length = seq // dil
    n_blk = length // BLK
    nb = -(-max_dist // BLK)
    kw = (nb + 1) * BLK
    hd = n_heads * HEAD_DIM
    gd = n_groups * HEAD_DIM
    view = slab.reshape(b, length, dil * width)
    bias = _band_bias(table, max_dist, dil, nb)

    def q_map(bi, r, i):
        return (bi, i, (r * width + q_off) // hd)

    def kv_map(off, j):
        return lambda bi, r, i: (bi, jnp.maximum(i - j, 0), (r * width + off) // gd)

    in_specs, args = [], []
    if sinks is not None:
        in_specs.append(pl.BlockSpec(memory_space=pltpu.SMEM))
        args.append(sinks.astype(F32))
    in_specs.append(pl.BlockSpec((1, BLK, hd), q_map))
    args.append(view)
    for off in (k_off, v_off):
        for j in range(nb + 1):
            in_specs.append(pl.BlockSpec((1, BLK, gd), kv_map(off, j)))
            args.append(view)
    in_specs.append(pl.BlockSpec((n_heads, BLK, kw), lambda bi, r, i: (0, 0, 0)))
    args.append(bias)

    out_spec = pl.BlockSpec((1, BLK, hd), lambda bi, r, i: (bi, i, r))
    out_shape = [jax.ShapeDtypeStruct((b, length, dil * hd), out_dtype)]
    out_specs = [out_spec]
    if want_lse:
        out_shape.append(jax.ShapeDtypeStruct((b, length, dil * hd), F32))
        out_specs.append(out_spec)

    res = pl.pallas_call(
        functools.partial(_banded_kernel, n_heads=n_heads, n_groups=n_groups, nb=nb,
                          has_sinks=sinks is not None, want_lse=want_lse),
        grid=(b, dil, n_blk),
        in_specs=in_specs,
        out_specs=out_specs,
        out_shape=out_shape,
        compiler_params=_cparams(("parallel", "parallel", "arbitrary")),
        name="banded_attention",
    )(*args)
    return [r.reshape(b, seq, hd) for r in res]


def _dilated_mix_kernel(o0, o1, o2, l0, l1, l2, out_ref):
    ls = [l0[...], l1[...], l2[...]]
    m = jnp.maximum(jnp.maximum(ls[0], ls[1]), ls[2])
    es = [jnp.exp(l - m) for l in ls]
    den = es[0] + es[1] + es[2]
    acc = (es[0] / den) * o0[...] + (es[1] / den) * o1[...] + (es[2] / den) * o2[...]
    out_ref[...] = acc.astype(out_ref.dtype)


def _dilated_mix(outs, lses, tm):
    m, c = outs[0].shape
    spec = pl.BlockSpec((tm, c), lambda i: (i, 0))
    return pl.pallas_call(
        _dilated_mix_kernel,
        grid=(m // tm,),
        in_specs=[spec] * 6,
        out_specs=spec,
        out_shape=jax.ShapeDtypeStruct((m, c), BF16),
        compiler_params=_cparams(("parallel",)),
        name="dilated_mix",
    )(*outs, *lses)


def _retention_kernel(q_ref, k_ref, v_ref, g_ref, cos_ref, sin_ref, dmask_ref, qdec_ref, kdec_ref,
                      cdec_ref, gn_ref, o_ref, state_ref):
    hd = pl.program_id(1)

    @pl.when(pl.program_id(2) == 0)
    def _():
        state_ref[...] = jnp.zeros_like(state_ref)

    cos = cos_ref[...]
    sin = sin_ref[...]
    half = C_QK_DIM // 2

    def rot(x):
        x1, x2 = x[:, :half], x[:, half:]
        return jnp.concatenate([x1 * cos - x2 * sin, x1 * sin + x2 * cos], axis=1)

    q = rot(q_ref[0].astype(F32))
    k = rot(k_ref[0].astype(F32)) * (C_QK_DIM ** -0.5)
    v = v_ref[0]
    qb = q.astype(BF16)
    inner = lax.dot_general(qb, k.astype(BF16), (((1,), (1,)), ((), ())),
                            preferred_element_type=F32) * dmask_ref[0]
    state = state_ref[...]
    o = jnp.dot(inner.astype(BF16), v, preferred_element_type=F32)
    o = o + jnp.dot(qb, state.astype(BF16), preferred_element_type=F32) * qdec_ref[0]
    kd_t = jnp.transpose(k * kdec_ref[0]).astype(BF16)
    state_ref[...] = state * cdec_ref[hd] + jnp.dot(kd_t, v, preferred_element_type=F32)

    mu = jnp.mean(o, axis=-1, keepdims=True)
    oc = o - mu
    var = jnp.mean(oc * oc, axis=-1, keepdims=True)
    on = oc * lax.rsqrt(var + 1e-5)
    o_ref[0] = (on * gn_ref[...] * jax.nn.silu(g_ref[0].astype(F32))).astype(o_ref.dtype)


def _retention(slab, gn, seq):
    b = slab.shape[0]
    n_chunks = seq // C_CHUNK
    half = C_QK_DIM // 2
    pos = jnp.arange(seq, dtype=F32)
    inv = 1.0 / (10000.0 ** (jnp.arange(0, C_QK_DIM, 2, dtype=F32) / C_QK_DIM))
    ang = pos[:, None] * inv[None, :]
    cos, sin = jnp.cos(ang), jnp.sin(ang)
    log_g = jnp.log(1.0 - 2.0 ** (-5.0 - jnp.arange(C_HEADS, dtype=F32)))
    j = jnp.arange(C_CHUNK, dtype=F32)
    diff = j[:, None] - j[None, :]
    dmask = jnp.where(diff >= 0, jnp.exp(diff[None] * log_g[:, None, None]), 0.0)
    q_dec = jnp.exp((j[None, :] + 1.0) * log_g[:, None])[:, :, None]
    k_dec = jnp.exp((C_CHUNK - 1.0 - j[None, :]) * log_g[:, None])[:, :, None]
    chunk_dec = jnp.exp(C_CHUNK * log_g)

    def col(off, w):
        return lambda bi, h, c: (bi, c, off // w + h)

    return pl.pallas_call(
        _retention_kernel,
        grid=(b, C_HEADS, n_chunks),
        in_specs=[pl.BlockSpec((1, C_CHUNK, C_QK_DIM), col(OD_QC, C_QK_DIM)),
                  pl.BlockSpec((1, C_CHUNK, C_QK_DIM), col(OD_KC, C_QK_DIM)),
                  pl.BlockSpec((1, C_CHUNK, C_V_DIM), col(OD_VC, C_V_DIM)),
                  pl.BlockSpec((1, C_CHUNK, C_V_DIM), col(OD_GC, C_V_DIM)),
                  pl.BlockSpec((C_CHUNK, half), lambda bi, h, c: (c, 0)),
                  pl.BlockSpec((C_CHUNK, half), lambda bi, h, c: (c, 0)),
                  pl.BlockSpec((1, C_CHUNK, C_CHUNK), lambda bi, h, c: (h, 0, 0)),
                  pl.BlockSpec((1, C_CHUNK, 1), lambda bi, h, c: (h, 0, 0)),
                  pl.BlockSpec((1, C_CHUNK, 1), lambda bi, h, c: (h, 0, 0)),
                  pl.BlockSpec(memory_space=pltpu.SMEM),
                  pl.BlockSpec((1, C_V_DIM), lambda bi, h, c: (0, h))],
        out_specs=pl.BlockSpec((1, C_CHUNK, C_V_DIM), lambda bi, h, c: (bi, c, h)),
        out_shape=jax.ShapeDtypeStruct((b, seq, C_V), BF16),
        scratch_shapes=[pltpu.VMEM((C_QK_DIM, C_V_DIM), F32)],
        compiler_params=_cparams(("parallel", "parallel", "arbitrary")),
        name="retention",
    )(slab, slab, slab, slab, cos, sin, dmask, q_dec, k_dec, chunk_dec, gn.reshape(1, C_V).astype(F32))


def _compress_kernel(x_ref, pos_ref, w1_ref, w2_ref, o_ref):
    x = (x_ref[0].astype(F32) + pos_ref[0]).astype(BF16)
    hid = jax.nn.gelu(jnp.dot(x, w1_ref[0], preferred_element_type=F32))
    o_ref[0] = jnp.dot(hid.astype(BF16), w2_ref[0], preferred_element_type=F32)


def _compress(flat, pos, w1, w2):
    _, bg, rows, width = flat.shape
    tr = min(rows, 256)
    return pl.pallas_call(
        _compress_kernel,
        grid=(2, bg, rows // tr),
        in_specs=[pl.BlockSpec((None, 1, tr, width), lambda s, i, r: (s, i, r, 0)),
                  pl.BlockSpec((1, 1, width), lambda s, i, r: (s, 0, 0)),
                  pl.BlockSpec((1, width, D_CMP_HIDDEN), lambda s, i, r: (s, 0, 0)),
                  pl.BlockSpec((1, D_CMP_HIDDEN, HEAD_DIM), lambda s, i, r: (s, 0, 0))],
        out_specs=pl.BlockSpec((None, 1, tr, HEAD_DIM), lambda s, i, r: (s, i, r, 0)),
        out_shape=jax.ShapeDtypeStruct((2, bg, rows, HEAD_DIM), F32),
        compiler_params=_cparams(("parallel", "parallel", "parallel")),
        name="nsa_compress",
    )(flat, pos, w1, w2)


def _cmp_bias(table):
    tt = jnp.arange(BLK)[:, None]
    m = jnp.arange(CMP_WIN)[None, :] + (BLK // D_CMP_STRIDE) - CMP_WIN
    dist = tt - D_CMP_STRIDE * m - (D_CMP_LEN - 1)
    return jnp.where((dist >= 0)[None], _bias_lookup(table, dist), NEG_INF)


def _sel_matrix():
    c_rel = np.arange(CMP_WIN)[:, None] + (BLK // D_CMP_STRIDE) - CMP_WIN
    j_rel = np.arange(REL_BLOCKS)[None, :] - (REL_BLOCKS - 2)
    return ((c_rel >= 4 * j_rel - 1) & (c_rel <= 4 * j_rel + 3)).astype(np.float32)


def _cmp_attn_kernel(q_ref, kc_ref, vc_ref, bias_ref, sel_ref, o_ref, idx_ref):
    i = pl.program_id(1)
    hpg = D_HEADS // D_KV_HEADS
    lane = lax.broadcasted_iota(jnp.int32, (1, LANES), 1)
    lo = lane < HEAD_DIM
    half_mask = (jnp.where(lo, 1.0, 0.0).astype(BF16), jnp.where(lo, 0.0, 1.0).astype(BF16))
    start = pl.multiple_of(8 * i + 8, 8)
    ucol = lax.broadcasted_iota(jnp.int32, (1, CMP_WIN), 1)
    edge = jnp.where(ucol < CMP_WIN - 8 - 8 * i, NEG_INF, 0.0).astype(F32)

    row = lax.broadcasted_iota(jnp.int32, (BLK, REL_BLOCKS), 0)
    jj = lax.broadcasted_iota(jnp.int32, (BLK, REL_BLOCKS), 1)
    cur = (REL_BLOCKS - 2) + (row >= D_SEL_LEN).astype(jnp.int32)
    first = (REL_BLOCKS - 2) - 2 * i
    exists = jj >= first
    forced = exists & ((jj == first) | (jj == cur) | (jj == cur - 1))
    valid = exists & (jj <= cur)
    jjf = jj.astype(F32)
    out_lane = lax.broadcasted_iota(jnp.int32, (BLK, LANES), 1)
    out_row = lax.broadcasted_iota(jnp.int32, (BLK, LANES), 0)
    cur_abs = 2 * i + (out_row >= D_SEL_LEN).astype(jnp.int32)
    tt_in_blk = out_row % D_SEL_LEN

    for g in range(D_KV_HEADS):
        kwin = kc_ref[0, g, pl.ds(start, CMP_WIN), :].astype(BF16)
        vwin = vc_ref[0, g, pl.ds(start, CMP_WIN), :].astype(BF16)
        imp = jnp.zeros((BLK, CMP_WIN), F32)
        outs = []
        for hh in range(hpg):
            h = g * hpg + hh
            pair, par = h // 2, h % 2
            qz = q_ref[0, :, pair * LANES:(pair + 1) * LANES] * half_mask[par]
            s = lax.dot_general(qz, kwin, (((1,), (1,)), ((), ())), preferred_element_type=F32)
            s = s * (HEAD_DIM ** -0.5) + bias_ref[h] + edge
            ok = s > 0.5 * NEG_INF
            m = jnp.max(s, axis=-1, keepdims=True)
            p = jnp.where(ok, jnp.exp(s - m), 0.0)
            den = jnp.maximum(jnp.sum(p, axis=-1, keepdims=True), 1e-30)
            pn = p * (1.0 / den)
            imp = imp + pn
            outs.append(jnp.dot(pn.astype(BF16), vwin, preferred_element_type=F32))
        for pr in range(hpg // 2):
            pair = (g * hpg) // 2 + pr
            o_ref[0, :, pair * LANES:(pair + 1) * LANES] = jnp.where(
                lo, outs[2 * pr], outs[2 * pr + 1]).astype(o_ref.dtype)

        sel = sel_ref[...]
        hi = imp.astype(BF16)
        r1 = imp - hi.astype(F32)
        mid = r1.astype(BF16)
        low = (r1 - mid.astype(F32)).astype(BF16)
        imp_sel = (jnp.dot(hi, sel, preferred_element_type=F32)
                   + jnp.dot(mid, sel, preferred_element_type=F32)
                   + jnp.dot(low, sel, preferred_element_type=F32))
        score = jnp.where(forced, 1e9, jnp.where(valid, imp_sel, -1e9))
        picked = jnp.zeros((BLK, LANES), jnp.int32)
        for r in range(D_SEL_COUNT):
            m = jnp.max(score, axis=-1, keepdims=True)
            am = jnp.min(jnp.where(score == m, jjf, float(REL_BLOCKS)), axis=-1, keepdims=True)
            none = m < -5e8
            blk = am.astype(jnp.int32) - first
            key_blk = jnp.where(none, 0, blk)
            bias_row = jnp.where(none, SEL_NONE, jnp.minimum(cur_abs - blk, SEL_FAR)) * D_SEL_LEN + tt_in_blk
            picked = jnp.where(out_lane == r, key_blk, picked)
            picked = jnp.where(out_lane == D_SEL_COUNT + r, bias_row, picked)
            score = jnp.where(jjf == am, -jnp.inf, score)
        idx_ref[0, g] = picked


def _cmp_attention(slab, kc_pad, vc_pad, table, seq):
    b = slab.shape[0]
    n_blk = seq // BLK
    rows = kc_pad.shape[2]
    bias = _cmp_bias(table)
    sel = jnp.asarray(_sel_matrix(), BF16)
    return pl.pallas_call(
        _cmp_attn_kernel,
        grid=(b, n_blk),
        in_specs=[pl.BlockSpec((1, BLK, D_Q), lambda bi, i: (bi, i, OD_QD // D_Q)),
                  pl.BlockSpec((1, D_KV_HEADS, rows, LANES), lambda bi, i: (bi, 0, 0, 0)),
                  pl.BlockSpec((1, D_KV_HEADS, rows, LANES), lambda bi, i: (bi, 0, 0, 0)),
                  pl.BlockSpec((D_HEADS, BLK, CMP_WIN), lambda bi, i: (0, 0, 0)),
                  pl.BlockSpec((CMP_WIN, REL_BLOCKS), lambda bi, i: (0, 0))],
        out_specs=[pl.BlockSpec((1, BLK, D_Q), lambda bi, i: (bi, i, 0)),
                   pl.BlockSpec((1, D_KV_HEADS, BLK, LANES), lambda bi, i: (bi, 0, i, 0))],
        out_shape=[jax.ShapeDtypeStruct((b, seq, D_Q), BF16),
                   jax.ShapeDtypeStruct((b, D_KV_HEADS, seq, LANES), jnp.int32)],
        compiler_params=_cparams(("parallel", "arbitrary")),
        name="nsa_cmp_attention",
    )(slab, kc_pad, vc_pad, bias, sel)


def _sel_bias(table):
    hpg = D_HEADS // D_KV_HEADS
    delta = jnp.arange(SEL_FAR)[:, None, None]
    tt = jnp.arange(D_SEL_LEN)[None, :, None]
    l = jnp.arange(D_SEL_LEN)[None, None, :]
    dist = D_SEL_LEN * delta + tt - l
    near = jnp.where((dist >= 0)[None], _bias_lookup(table, dist), NEG_INF)
    far = jnp.broadcast_to(table.astype(F32)[REL_BUCKETS - 1][:, None, None, None],
                           (D_HEADS, 1, D_SEL_LEN, D_SEL_LEN))
    none = jnp.full((D_HEADS, 1, D_SEL_LEN, D_SEL_LEN), NEG_INF, F32)
    rows = jnp.concatenate([near, far, none], axis=1)
    half = D_SEL_LEN // 2
    rows = rows.reshape(D_KV_HEADS, hpg, SEL_NONE + 1, D_SEL_LEN, half, 2)
    rows = jnp.transpose(rows, (0, 2, 3, 5, 1, 4))
    rows = rows.reshape(D_KV_HEADS, (SEL_NONE + 1) * D_SEL_LEN, 2 * hpg, half)
    return jnp.tile(rows, (1, 1, 1, LANES // half))


SEL_IDX = 2 * D_SEL_COUNT
SEL_UNROLL = 16


def _sel_attn_kernel(idx_hbm, q_ref, kv_ref, bias_ref, o_ref, idx_smem, sem):
    n_g, n_i = pl.num_programs(1), pl.num_programs(2)
    step = (pl.program_id(0) * n_g + pl.program_id(1)) * n_i + pl.program_id(2)
    total = pl.num_programs(0) * n_g * n_i
    slot = step % 2

    tile_words = BLK * SEL_IDX

    def idx_copy(s, sl):
        dst = idx_smem.at[pl.ds(pl.multiple_of(sl * tile_words, tile_words), tile_words)]
        return pltpu.make_async_copy(idx_hbm.at[s], dst, sem.at[sl])

    @pl.when(step == 0)
    def _():
        idx_copy(0, 0).start()

    @pl.when(step + 1 < total)
    def _():
        idx_copy(step + 1, 1 - slot).start()

    idx_copy(step, slot).wait()

    lane = lax.broadcasted_iota(jnp.int32, (1, LANES), 1)
    lo = lane < HEAD_DIM
    hpg = D_HEADS // D_KV_HEADS
    quarter = D_SEL_LEN // 2
    n_quads = D_SEL_COUNT // 4

    def scores(tl):
        picks = idx_smem.at[pl.ds(slot * tile_words + tl * SEL_IDX, SEL_IDX)]
        qq = q_ref[0, 0, tl]
        qbd = jnp.concatenate([jnp.where(lo, qq, 0.0), jnp.where(lo, 0.0, qq)], axis=0).astype(BF16)
        ks, vs, bs = [], [], []
        for n in range(D_SEL_COUNT):
            kv = kv_ref[0, 0, picks[n]]
            ks.append(kv[:quarter])
            vs.append(kv[quarter:])
            bs.append(bias_ref[0, picks[D_SEL_COUNT + n]])
        s = lax.dot_general(qbd, jnp.concatenate(ks, axis=0), (((1,), (1,)), ((), ())),
                            preferred_element_type=F32)
        bias = jnp.concatenate(
            [jnp.where(lane < quarter, bs[4 * c],
                       jnp.where(lane < 2 * quarter, bs[4 * c + 1],
                                 jnp.where(lane < 3 * quarter, bs[4 * c + 2], bs[4 * c + 3])))
             for c in range(n_quads)], axis=1)
        return s * (HEAD_DIM ** -0.5) + bias, jnp.concatenate(vs, axis=0)

    def probs(s):
        m16 = jnp.max(s, axis=-1, keepdims=True)
        m = jnp.maximum(m16[:hpg], m16[hpg:])
        p = jnp.exp(s - jnp.concatenate([m, m], axis=0))
        d16 = jnp.sum(p, axis=-1, keepdims=True)
        return p.astype(BF16), d16[:hpg] + d16[hpg:]

    def body(it, carry):
        t0 = it * SEL_UNROLL
        sv = [scores(t0 + u) for u in range(SEL_UNROLL)]
        pd = [probs(s) for s, _ in sv]
        outs = []
        for (p, den), (_, vall) in zip(pd, sv):
            o2 = jnp.dot(p, vall, preferred_element_type=F32)
            outs.append(jnp.where(lo, o2[:hpg], o2[hpg:]) * (1.0 / den))
        for u in range(SEL_UNROLL):
            o_ref[0, 0, t0 + u] = outs[u]
        return carry

    lax.fori_loop(0, BLK // SEL_UNROLL, body, 0)


def _sel_attention(idx, q_sel, kv_sel, table, seq):
    b = q_sel.shape[0]
    hpg = D_HEADS // D_KV_HEADS
    n_blk = seq // BLK
    n_sb = seq // D_SEL_LEN
    bias = _sel_bias(table)
    kv_spec = pl.BlockSpec((1, 1, n_sb, D_SEL_LEN, LANES), lambda bi, g, i: (bi, g, 0, 0, 0))
    return pl.pallas_call(
        _sel_attn_kernel,
        grid=(b, D_KV_HEADS, n_blk),
        in_specs=[pl.BlockSpec(memory_space=pl.ANY),
                  pl.BlockSpec((1, 1, BLK, hpg, LANES), lambda bi, g, i: (bi, g, i, 0, 0)),
                  kv_spec,
                  pl.BlockSpec((1, (SEL_NONE + 1) * D_SEL_LEN, 2 * hpg, LANES), lambda bi, g, i: (g, 0, 0, 0))],
        out_specs=pl.BlockSpec((1, 1, BLK, hpg, LANES), lambda bi, g, i: (bi, g, i, 0, 0)),
        out_shape=jax.ShapeDtypeStruct((b, D_KV_HEADS, seq, hpg, LANES), F32),
        scratch_shapes=[pltpu.SMEM((2 * BLK * SEL_IDX,), jnp.int32), pltpu.SemaphoreType.DMA((2,))],
        compiler_params=_cparams(("arbitrary", "arbitrary", "arbitrary")),
        name="nsa_sel_attention",
    )(idx, q_sel, kv_sel, bias)


def _gate_expand():
    e = np.zeros((3, LANES, D_Q), np.float32)
    for h in range(D_HEADS):
        for c in range(3):
            e[c, 3 * h + c, h * HEAD_DIM:(h + 1) * HEAD_DIM] = 1.0
    return e


def _nsa_gate_kernel(gd_ref, e_ref, oc_ref, os_ref, ow_ref, out_ref):
    sg = jax.nn.sigmoid(gd_ref[...].astype(F32))
    hi = sg.astype(BF16)
    low = (sg - hi.astype(F32)).astype(BF16)
    acc = None
    for c, o_ref in enumerate((oc_ref, os_ref, ow_ref)):
        gate = (jnp.dot(hi, e_ref[c], preferred_element_type=F32)
                + jnp.dot(low, e_ref[c], preferred_element_type=F32))
        term = gate * o_ref[...].astype(F32)
        acc = term if acc is None else acc + term
    out_ref[...] = acc.astype(out_ref.dtype)


def _nsa_gate(slab2d, o_c, o_s, o_w, tm):
    m = slab2d.shape[0]
    e = jnp.asarray(_gate_expand(), BF16)
    spec = pl.BlockSpec((tm, D_Q), lambda i: (i, 0))
    return pl.pallas_call(
        _nsa_gate_kernel,
        grid=(m // tm,),
        in_specs=[pl.BlockSpec((tm, LANES), lambda i: (i, OD_GD // LANES)),
                  pl.BlockSpec((3, LANES, D_Q), lambda i: (0, 0, 0)),
                  spec, spec, spec],
        out_specs=spec,
        out_shape=jax.ShapeDtypeStruct((m, D_Q), BF16),
        compiler_params=_cparams(("parallel",)),
        name="nsa_gate",
    )(slab2d, e, o_c, o_s, o_w)


def _even_mixer(h, nw, w_in, sinks, w_out, rel_table, b, seq):
    qa, ka, va, qb, kb, vb = jnp.split(w_in, [int(c) for c in np.cumsum([A_Q, A_KV, A_KV, B_W, B_W])], axis=1)
    pad = jnp.zeros((D_MODEL, EVEN_SLAB - w_in.shape[1]), w_in.dtype)
    w_slab = jnp.concatenate([qa, qb, kb, vb, ka, va, pad], axis=1).astype(BF16)
    slab = _norm_matmul(h, nw, w_slab, 1024, 512).reshape(b, seq, EVEN_SLAB)

    (oa,) = _banded_attention(slab, seq=seq, dil=1, width=EVEN_SLAB, q_off=EV_QA, k_off=EV_KA, v_off=EV_VA,
                              n_heads=A_HEADS, n_groups=A_KV_HEADS, max_dist=A_WINDOW - 1,
                              table=rel_table[:, :A_HEADS], sinks=sinks)
    outs, lses = [], []
    for window, dil in B_PATTERNS:
        o, lse = _banded_attention(slab, seq=seq, dil=dil, width=EVEN_SLAB, q_off=EV_QB, k_off=EV_KB,
                                   v_off=EV_VB, n_heads=B_HEADS, n_groups=B_HEADS, max_dist=window // dil,
                                   table=rel_table[:, A_HEADS:A_HEADS + B_HEADS], want_lse=True, out_dtype=F32)
        outs.append(o.reshape(b * seq, B_W))
        lses.append(lse.reshape(b * seq, B_W))
    ob = _dilated_mix(outs, lses, 1024)
    w_out = w_out.astype(BF16)
    return _proj_residual(h, [oa.reshape(b * seq, A_Q), ob], [w_out[:A_Q], w_out[A_Q:]], 512)


def _nsa(slab, pos_k, pos_v, k_w1, k_w2, v_w1, v_w2, rel_table, b, seq):
    g_kv = D_KV_HEADS
    hpg = D_HEADS // g_kv
    slab2d = slab.reshape(b * seq, ODD_SLAB)
    n_rows = seq // D_CMP_STRIDE

    def rows16(off):
        a = slab[:, :, off:off + D_KV].reshape(b, n_rows, D_CMP_STRIDE, g_kv, HEAD_DIM)
        a = jnp.transpose(a, (0, 3, 1, 2, 4)).reshape(b * g_kv, n_rows, D_CMP_STRIDE * HEAD_DIM)
        nxt = jnp.concatenate([a[:, 1:], jnp.zeros_like(a[:, :1])], axis=1)
        return jnp.concatenate([a, nxt], axis=-1)

    flat = jnp.stack([rows16(OD_KCMP), rows16(OD_VCMP)])
    pos = jnp.stack([pos_k.reshape(1, -1), pos_v.reshape(1, -1)]).astype(F32)
    w1 = jnp.stack([k_w1, v_w1]).astype(BF16)
    w2 = jnp.stack([k_w2, v_w2]).astype(BF16)
    cmp = _compress(flat, pos, w1, w2).reshape(2, b, g_kv, n_rows, HEAD_DIM)
    cmp = jnp.pad(cmp, ((0, 0), (0, 0), (0, 0), (CMP_WIN, 0), (0, 0)))
    cmp = jnp.concatenate([cmp, cmp], axis=-1)
    o_c, idx = _cmp_attention(slab, cmp[0], cmp[1], rel_table, seq)

    idx = idx[..., :SEL_IDX].reshape(b * g_kv * (seq // BLK), BLK * SEL_IDX)

    def per_group(off):
        return slab[:, :, off:off + D_KV].reshape(b, seq, g_kv, HEAD_DIM)

    def two_per_row(off):
        a = jnp.transpose(per_group(off), (0, 2, 1, 3))
        return a.reshape(b, g_kv, seq // D_SEL_LEN, D_SEL_LEN // 2, LANES)

    q_sel = slab[:, :, OD_QD:OD_QD + D_Q].reshape(b, seq, g_kv, hpg, HEAD_DIM)
    q_sel = jnp.transpose(q_sel, (0, 2, 1, 3, 4)).astype(F32)
    q_sel = jnp.concatenate([q_sel, q_sel], axis=-1)
    kv_sel = jnp.concatenate([two_per_row(OD_KSLC), two_per_row(OD_VSLC)], axis=3)
    o_s = _sel_attention(idx, q_sel, kv_sel, rel_table, seq)
    o_s = o_s[..., :HEAD_DIM] + o_s[..., HEAD_DIM:]
    o_s = jnp.transpose(o_s, (0, 2, 1, 3, 4)).reshape(b * seq, D_Q).astype(BF16)

    (o_w,) = _banded_attention(slab, seq=seq, dil=1, width=ODD_SLAB, q_off=OD_QD, k_off=OD_KWIN, v_off=OD_VWIN,
                               n_heads=D_HEADS, n_groups=D_KV_HEADS, max_dist=D_WINDOW - 1, table=rel_table)
    return _nsa_gate(slab2d, o_c.reshape(b * seq, D_Q), o_s, o_w.reshape(b * seq, D_Q), 1024)


def _odd_mixer(h, nw, w_in, ret_gn, pos_k, pos_v, k_w1, k_w2, v_w1, v_w2, w_out, rel_table, b, seq):
    pad = jnp.zeros((D_MODEL, ODD_SLAB - ODD_IN), w_in.dtype)
    w_slab = jnp.concatenate([w_in, pad], axis=1).astype(BF16)
    slab = _norm_matmul(h, nw, w_slab, 1024, 512).reshape(b, seq, ODD_SLAB)
    oc = _retention(slab, ret_gn, seq)
    od = _nsa(slab, pos_k, pos_v, k_w1, k_w2, v_w1, v_w2, rel_table, b, seq)
    w_out = w_out.astype(BF16)
    return _proj_residual(h, [oc.reshape(b * seq, C_V), od], [w_out[:C_V], w_out[C_V:]], 512)


def kernel(x, rel_table, norm_mix, norm_ffn, norm_final, even_w_in, even_sinks, even_w_out, odd_w_in, odd_ret_gn, odd_cmp_pos_k, odd_cmp_pos_v, odd_cmp_k_w1, odd_cmp_k_w2, odd_cmp_v_w1, odd_cmp_v_w2, odd_w_out, ffn_w_gate, ffn_w_up, ffn_w_down):
    b, seq, d = x.shape
    h = x.reshape(b * seq, d)
    for layer in range(DEPTH):
        li = layer // 2
        if layer % 2 == 0:
            h = _even_mixer(h, norm_mix[layer], even_w_in[li], even_sinks[li], even_w_out[li], rel_table, b, seq)
        else:
            h = _odd_mixer(h, norm_mix[layer], odd_w_in[li], odd_ret_gn[li], odd_cmp_pos_k[li],
                           odd_cmp_pos_v[li], odd_cmp_k_w1[li], odd_cmp_k_w2[li], odd_cmp_v_w1[li],
                           odd_cmp_v_w2[li], odd_w_out[li], rel_table, b, seq)
        h = _ffn(h, norm_ffn[layer], ffn_w_gate[layer].astype(BF16), ffn_w_up[layer].astype(BF16),
                 ffn_w_down[layer].astype(BF16), norm_final, layer == DEPTH - 1, 1024, 256)
    return h.reshape(b, seq, d)
```

```python
import functools
import math

import numpy as np
import jax
import jax.numpy as jnp
from jax import lax
from jax.experimental import pallas as pl
from jax.experimental.pallas import tpu as pltpu

F32 = jnp.float32
BF16 = jnp.bfloat16

D_MODEL = 1024
DEPTH = 4
HEAD_DIM = 64
BLK = 128
NEG_INF = -1e30
REL_BUCKETS = 32
REL_MAX_DIST = 2048
A_HEADS = 8
A_KV_HEADS = 2
A_WINDOW = 128
B_HEADS = 8
B_PATTERNS = ((128, 1), (512, 4), (2048, 16))
C_HEADS = 4
C_QK_DIM = 256
C_V_DIM = 512
C_CHUNK = 128
D_HEADS = 16
D_KV_HEADS = 2
D_CMP_LEN = 32
D_CMP_STRIDE = 16
D_CMP_HIDDEN = 128
D_SEL_LEN = 64
D_SEL_COUNT = 16
D_WINDOW = 512
D_FF = 2816

A_Q = A_HEADS * HEAD_DIM
A_KV = A_KV_HEADS * HEAD_DIM
B_W = B_HEADS * HEAD_DIM
C_QK = C_HEADS * C_QK_DIM
C_V = C_HEADS * C_V_DIM
D_Q = D_HEADS * HEAD_DIM
D_KV = D_KV_HEADS * HEAD_DIM
ODD_IN = 2 * C_QK + 2 * C_V + D_Q + 6 * D_KV + 3 * D_HEADS

LANES = 128
VMEM_LIMIT = 56 * 1024 * 1024

EVEN_SLAB = 2560
EV_QA, EV_QB, EV_KB, EV_VB, EV_KA, EV_VA = 0, 512, 1024, 1536, 2048, 2176
ODD_SLAB = 8192
OD_QC, OD_KC, OD_VC, OD_GC, OD_QD = 0, 1024, 2048, 4096, 6144
OD_KCMP, OD_VCMP, OD_KSLC, OD_VSLC, OD_KWIN, OD_VWIN, OD_GD = 7168, 7296, 7424, 7552, 7680, 7808, 7936

SEL_FAR = 25
SEL_NONE = 26
CMP_WIN = 1024
CMP_WIDTHS = (256, 512, 768, 1024)
REL_BLOCKS = 256


def _cparams(sem):
    return pltpu.CompilerParams(dimension_semantics=sem, vmem_limit_bytes=VMEM_LIMIT)


def _t5_bucket(dist):
    max_exact = REL_BUCKETS // 2
    d = jnp.maximum(dist, 0)
    df = jnp.maximum(d, 1).astype(jnp.float32)
    large = max_exact + (jnp.log(df / max_exact) / math.log(REL_MAX_DIST / max_exact)
                         * (REL_BUCKETS - max_exact)).astype(jnp.int32)
    large = jnp.minimum(large, REL_BUCKETS - 1)
    return jnp.where(d < max_exact, d, large)


def _bias_lookup(table, dist):
    bucket = _t5_bucket(dist)[None]
    tab = table.astype(F32)
    expand = (slice(None),) + (None,) * dist.ndim
    out = jnp.zeros((tab.shape[1],) + dist.shape, F32)
    for b in range(REL_BUCKETS):
        out = jnp.where(bucket == b, tab[b][expand], out)
    return out


def _rms(x, w, eps=1e-6):
    return x * lax.rsqrt(jnp.mean(x * x, axis=-1, keepdims=True) + eps) * w


def _norm_matmul_kernel(h_ref, nw_ref, w_ref, o_ref, hn_ref):
    @pl.when(pl.program_id(1) == 0)
    def _():
        hn_ref[...] = _rms(h_ref[...], nw_ref[...]).astype(BF16)

    o_ref[...] = jnp.dot(hn_ref[...], w_ref[...], preferred_element_type=F32).astype(o_ref.dtype)


def _norm_matmul(h, nw, w, tm, tn):
    m, d = h.shape
    n = w.shape[1]
    return pl.pallas_call(
        _norm_matmul_kernel,
        grid=(m // tm, n // tn),
        in_specs=[pl.BlockSpec((tm, d), lambda i, j: (i, 0)),
                  pl.BlockSpec((1, d), lambda i, j: (0, 0)),
                  pl.BlockSpec((d, tn), lambda i, j: (0, j))],
        out_specs=pl.BlockSpec((tm, tn), lambda i, j: (i, j)),
        out_shape=jax.ShapeDtypeStruct((m, n), BF16),
        scratch_shapes=[pltpu.VMEM((tm, d), BF16)],
        compiler_params=_cparams(("parallel", "arbitrary")),
        name="norm_matmul",
    )(h, nw.reshape(1, d), w)


def _proj_residual_kernel(*refs, n_in):
    h_ref, out_ref = refs[0], refs[-1]
    acc = h_ref[...]
    for o_ref, w_ref in zip(refs[1:1 + n_in], refs[1 + n_in:1 + 2 * n_in]):
        acc = acc + jnp.dot(o_ref[...], w_ref[...], preferred_element_type=F32)
    out_ref[...] = acc


def _proj_residual(h, outs, ws, tm):
    m, d = h.shape
    n_in = len(outs)
    in_specs = [pl.BlockSpec((tm, d), lambda i: (i, 0))]
    in_specs += [pl.BlockSpec((tm, o.shape[1]), lambda i: (i, 0)) for o in outs]
    in_specs += [pl.BlockSpec(w.shape, lambda i: (0, 0)) for w in ws]
    return pl.pallas_call(
        functools.partial(_proj_residual_kernel, n_in=n_in),
        grid=(m // tm,),
        in_specs=in_specs,
        out_specs=pl.BlockSpec((tm, d), lambda i: (i, 0)),
        out_shape=jax.ShapeDtypeStruct((m, d), F32),
        compiler_params=_cparams(("parallel",)),
        name="proj_residual",
    )(h, *outs, *ws)


def _ffn_kernel(h_ref, nw_ref, wg_ref, wu_ref, wd_ref, fw_ref, o_ref, hn_ref, acc_ref, *, final):
    j = pl.program_id(1)

    @pl.when(j == 0)
    def _():
        hn_ref[...] = _rms(h_ref[...], nw_ref[...]).astype(BF16)
        acc_ref[...] = jnp.zeros_like(acc_ref)

    hn = hn_ref[...]
    g = jnp.dot(hn, wg_ref[...], preferred_element_type=F32)
    u = jnp.dot(hn, wu_ref[...], preferred_element_type=F32)
    a = (jax.nn.silu(g) * u).astype(BF16)
    acc_ref[...] += jnp.dot(a, wd_ref[...], preferred_element_type=F32)

    @pl.when(j == pl.num_programs(1) - 1)
    def _():
        y = h_ref[...] + acc_ref[...]
        if final:
            y = _rms(y, fw_ref[...])
        o_ref[...] = y


def _ffn(h, nw, wg, wu, wd, fw, final, tm, tf):
    m, d = h.shape
    ff = wg.shape[1]
    return pl.pallas_call(
        functools.partial(_ffn_kernel, final=final),
        grid=(m // tm, ff // tf),
        in_specs=[pl.BlockSpec((tm, d), lambda i, j: (i, 0)),
                  pl.BlockSpec((1, d), lambda i, j: (0, 0)),
                  pl.BlockSpec((d, tf), lambda i, j: (0, j)),
                  pl.BlockSpec((d, tf), lambda i, j: (0, j)),
                  pl.BlockSpec((tf, d), lambda i, j: (j, 0)),
                  pl.BlockSpec((1, d), lambda i, j: (0, 0))],
        out_specs=pl.BlockSpec((tm, d), lambda i, j: (i, 0)),
        out_shape=jax.ShapeDtypeStruct((m, d), F32),
        scratch_shapes=[pltpu.VMEM((tm, d), BF16), pltpu.VMEM((tm, d), F32)],
        compiler_params=_cparams(("parallel", "arbitrary")),
        name="ffn",
    )(h, nw.reshape(1, d), wg, wu, wd, fw.reshape(1, d))


def _band_bias(table, max_dist, dist_scale, nb):
    kw = (nb + 1) * BLK
    rel = jnp.arange(BLK)[:, None] + nb * BLK - jnp.arange(kw)[None, :]
    band = (rel >= 0) & (rel <= max_dist)
    return jnp.where(band[None], _bias_lookup(table, rel * dist_scale), NEG_INF)


def _swap_halves(x):
    return jnp.concatenate([x[:, HEAD_DIM:], x[:, :HEAD_DIM]], axis=1)


def _banded_kernel(*refs, n_heads, n_groups, nb, has_sinks, want_lse):
    pos = 0
    if has_sinks:
        sink_ref = refs[0]
        pos = 1
    q_ref = refs[pos]
    k_refs = refs[pos + 1:pos + 2 + nb]
    v_refs = refs[pos + 2 + nb:pos + 3 + 2 * nb]
    bias_ref = refs[pos + 3 + 2 * nb]
    o_ref = refs[pos + 4 + 2 * nb]
    lse_ref = refs[pos + 5 + 2 * nb] if want_lse else None

    i = pl.program_id(2)
    kw = (nb + 1) * BLK
    hpg = n_heads // n_groups
    kcat = jnp.concatenate([k_refs[nb - jj][0] for jj in range(nb + 1)], axis=0)
    vcat = jnp.concatenate([v_refs[nb - jj][0] for jj in range(nb + 1)], axis=0)
    lane = lax.broadcasted_iota(jnp.int32, (1, LANES), 1)
    lo = lane < HEAD_DIM
    half_mask = (jnp.where(lo, 1.0, 0.0).astype(BF16), jnp.where(lo, 0.0, 1.0).astype(BF16))
    col = lax.broadcasted_iota(jnp.int32, (1, kw), 1)
    edge = jnp.where(col < (nb - i) * BLK, NEG_INF, 0.0).astype(F32)

    if hpg == 1:
        operands = [(kcat[:, p * LANES:(p + 1) * LANES], vcat[:, p * LANES:(p + 1) * LANES], [2 * p, 2 * p + 1])
                    for p in range(n_heads // 2)]
    else:
        k_sw = _swap_halves(kcat)
        v_sw = _swap_halves(vcat)
        operands = []
        for g in range(n_groups):
            for par in range(2):
                heads = [h for h in range(g * hpg, (g + 1) * hpg) if h % 2 == par]
                operands.append((kcat if g == par else k_sw, vcat if g == par else v_sw, heads))

    scores = []
    for kh, _, heads in operands:
        qz = jnp.concatenate([q_ref[0, :, (h // 2) * LANES:(h // 2 + 1) * LANES] * half_mask[h % 2]
                              for h in heads], axis=0)
        scores.append(lax.dot_general(qz, kh, (((1,), (1,)), ((), ())), preferred_element_type=F32))
    ms, dens, probs = {}, {}, []
    for (_, _, heads), s_all in zip(operands, scores):
        ps = []
        for r, h in enumerate(heads):
            s = s_all[r * BLK:(r + 1) * BLK] * (HEAD_DIM ** -0.5) + bias_ref[h] + edge
            m = jnp.max(s, axis=-1, keepdims=True)
            if has_sinks:
                m = jnp.maximum(m, sink_ref[h])
            p = jnp.exp(s - m)
            den = jnp.sum(p, axis=-1, keepdims=True)
            if has_sinks:
                den = den + jnp.exp(sink_ref[h] - m)
            ps.append(p.astype(BF16))
            ms[h], dens[h] = m, den
        probs.append(jnp.concatenate(ps, axis=0))
    outs = {}
    for (_, vh, heads), p_all in zip(operands, probs):
        o_all = jnp.dot(p_all, vh, preferred_element_type=F32)
        for r, h in enumerate(heads):
            outs[h] = o_all[r * BLK:(r + 1) * BLK]
    for pair in range(n_heads // 2):
        h0, h1 = 2 * pair, 2 * pair + 1
        inv = jnp.where(lo, 1.0 / dens[h0], 1.0 / dens[h1])
        o_pair = jnp.where(lo, outs[h0], outs[h1]) * inv
        o_ref[0, :, pair * LANES:(pair + 1) * LANES] = o_pair.astype(o_ref.dtype)
        if want_lse:
            lse_ref[0, :, pair * LANES:(pair + 1) * LANES] = jnp.where(
                lo, ms[h0] + jnp.log(dens[h0]), ms[h1] + jnp.log(dens[h1]))


def _banded_attention(slab, *, seq, dil, width, q_off, k_off, v_off, n_heads, n_groups,
                      max_dist, table, sinks=None, want_lse=False, out_dtype=BF16):
    b = slab.shape[0]
    length = seq // dil
    n_blk = length // BLK
    nb = -(-max_dist // BLK)
    kw = (nb + 1) * BLK
    hd = n_heads * HEAD_DIM
    gd = n_groups * HEAD_DIM
    view = slab.reshape(b, length, dil * width)
    bias = _band_bias(table, max_dist, dil, nb)

    def q_map(bi, r, i):
        return (bi, i, (r * width + q_off) // hd)

    def kv_map(off, j):
        return lambda bi, r, i: (bi, jnp.maximum(i - j, 0), (r * width + off) // gd)

    in_specs, args = [], []
    if sinks is not None:
        in_specs.append(pl.BlockSpec(memory_space=pltpu.SMEM))
        args.append(sinks.astype(F32))
    in_specs.append(pl.BlockSpec((1, BLK, hd), q_map))
    args.append(view)
    for off in (k_off, v_off):
        for j in range(nb + 1):
            in_specs.append(pl.BlockSpec((1, BLK, gd), kv_map(off, j)))
            args.append(view)
    in_specs.append(pl.BlockSpec((n_heads, BLK, kw), lambda bi, r, i: (0, 0, 0)))
    args.append(bias)

    out_spec = pl.BlockSpec((1, BLK, hd), lambda bi, r, i: (bi, i, r))
    out_shape = [jax.ShapeDtypeStruct((b, length, dil * hd), out_dtype)]
    out_specs = [out_spec]
    if want_lse:
        out_shape.append(jax.ShapeDtypeStruct((b, length, dil * hd), F32))
        out_specs.append(out_spec)

    res = pl.pallas_call(
        functools.partial(_banded_kernel, n_heads=n_heads, n_groups=n_groups, nb=nb,
                          has_sinks=sinks is not None, want_lse=want_lse),
        grid=(b, dil, n_blk),
        in_specs=in_specs,
        out_specs=out_specs,
        out_shape=out_shape,
        compiler_params=_cparams(("parallel", "parallel", "arbitrary")),
        name="banded_attention",
    )(*args)
    return [r.reshape(b, seq, hd) for r in res]


def _dilated_mix_kernel(o0, o1, o2, l0, l1, l2, out_ref):
    ls = [l0[...], l1[...], l2[...]]
    m = jnp.maximum(jnp.maximum(ls[0], ls[1]), ls[2])
    es = [jnp.exp(l - m) for l in ls]
    den = es[0] + es[1] + es[2]
    acc = (es[0] / den) * o0[...] + (es[1] / den) * o1[...] + (es[2] / den) * o2[...]
    out_ref[...] = acc.astype(out_ref.dtype)


def _dilated_mix(outs, lses, tm):
    m, c = outs[0].shape
    spec = pl.BlockSpec((tm, c), lambda i: (i, 0))
    return pl.pallas_call(
        _dilated_mix_kernel,
        grid=(m // tm,),
        in_specs=[spec] * 6,
        out_specs=spec,
        out_shape=jax.ShapeDtypeStruct((m, c), BF16),
        compiler_params=_cparams(("parallel",)),
        name="dilated_mix",
    )(*outs, *lses)


def _retention_kernel(q_ref, k_ref, v_ref, g_ref, cos_ref, sin_ref, dmask_ref, qdec_ref, kdec_ref,
                      cdec_ref, gn_ref, o_ref, state_ref):
    hd = pl.program_id(0)

    @pl.when(pl.program_id(1) == 0)
    def _():
        state_ref[...] = jnp.zeros_like(state_ref)

    cos = cos_ref[...]
    sin = sin_ref[...]
    half = C_QK_DIM // 2

    def rot(x):
        x1, x2 = x[:, :half], x[:, half:]
        return jnp.concatenate([x1 * cos - x2 * sin, x1 * sin + x2 * cos], axis=1)

    for bi in range(q_ref.shape[0]):
        q = rot(q_ref[bi].astype(F32))
        k = rot(k_ref[bi].astype(F32)) * (C_QK_DIM ** -0.5)
        v = v_ref[bi]
        qb = q.astype(BF16)
        inner = lax.dot_general(qb, k.astype(BF16), (((1,), (1,)), ((), ())),
                                preferred_element_type=F32) * dmask_ref[0]
        state = state_ref[bi]
        o = jnp.dot(inner.astype(BF16), v, preferred_element_type=F32)
        o = o + jnp.dot(qb, state.astype(BF16), preferred_element_type=F32) * qdec_ref[0]
        kd_t = jnp.transpose(k * kdec_ref[0]).astype(BF16)
        state_ref[bi] = state * cdec_ref[hd] + jnp.dot(kd_t, v, preferred_element_type=F32)

        mu = jnp.mean(o, axis=-1, keepdims=True)
        oc = o - mu
        var = jnp.mean(oc * oc, axis=-1, keepdims=True)
        on = oc * lax.rsqrt(var + 1e-5)
        o_ref[bi] = (on * gn_ref[...] * jax.nn.silu(g_ref[bi].astype(F32))).astype(o_ref.dtype)


def _retention(slab, gn, seq):
    b = slab.shape[0]
    n_chunks = seq // C_CHUNK
    half = C_QK_DIM // 2
    pos = jnp.arange(seq, dtype=F32)
    inv = 1.0 / (10000.0 ** (jnp.arange(0, C_QK_DIM, 2, dtype=F32) / C_QK_DIM))
    ang = pos[:, None] * inv[None, :]
    cos, sin = jnp.cos(ang), jnp.sin(ang)
    log_g = jnp.log(1.0 - 2.0 ** (-5.0 - jnp.arange(C_HEADS, dtype=F32)))
    j = jnp.arange(C_CHUNK, dtype=F32)
    diff = j[:, None] - j[None, :]
    dmask = jnp.where(diff >= 0, jnp.exp(diff[None] * log_g[:, None, None]), 0.0)
    q_dec = jnp.exp((j[None, :] + 1.0) * log_g[:, None])[:, :, None]
    k_dec = jnp.exp((C_CHUNK - 1.0 - j[None, :]) * log_g[:, None])[:, :, None]
    chunk_dec = jnp.exp(C_CHUNK * log_g)

    def col(off, w):
        return lambda h, c: (0, c, off // w + h)

    return pl.pallas_call(
        _retention_kernel,
        grid=(C_HEADS, n_chunks),
        in_specs=[pl.BlockSpec((b, C_CHUNK, C_QK_DIM), col(OD_QC, C_QK_DIM)),
                  pl.BlockSpec((b, C_CHUNK, C_QK_DIM), col(OD_KC, C_QK_DIM)),
                  pl.BlockSpec((b, C_CHUNK, C_V_DIM), col(OD_VC, C_V_DIM)),
                  pl.BlockSpec((b, C_CHUNK, C_V_DIM), col(OD_GC, C_V_DIM)),
                  pl.BlockSpec((C_CHUNK, half), lambda h, c: (c, 0)),
                  pl.BlockSpec((C_CHUNK, half), lambda h, c: (c, 0)),
                  pl.BlockSpec((1, C_CHUNK, C_CHUNK), lambda h, c: (h, 0, 0)),
                  pl.BlockSpec((1, C_CHUNK, 1), lambda h, c: (h, 0, 0)),
                  pl.BlockSpec((1, C_CHUNK, 1), lambda h, c: (h, 0, 0)),
                  pl.BlockSpec(memory_space=pltpu.SMEM),
                  pl.BlockSpec((1, C_V_DIM), lambda h, c: (0, h))],
        out_specs=pl.BlockSpec((b, C_CHUNK, C_V_DIM), lambda h, c: (0, c, h)),
        out_shape=jax.ShapeDtypeStruct((b, seq, C_V), BF16),
        scratch_shapes=[pltpu.VMEM((b, C_QK_DIM, C_V_DIM), F32)],
        compiler_params=_cparams(("parallel", "arbitrary")),
        name="retention",
    )(slab, slab, slab, slab, cos, sin, dmask, q_dec, k_dec, chunk_dec, gn.reshape(1, C_V).astype(F32))


def _compress_kernel(x_ref, pos_ref, w1_ref, w2_ref, o_ref):
    x = (x_ref[0].astype(F32) + pos_ref[0]).astype(BF16)
    hid = jax.nn.gelu(jnp.dot(x, w1_ref[0], preferred_element_type=F32))
    o_ref[0] = jnp.dot(hid.astype(BF16), w2_ref[0], preferred_element_type=F32)


def _compress(flat, pos, w1, w2):
    _, bg, rows, width = flat.shape
    tr = min(rows, 256)
    return pl.pallas_call(
        _compress_kernel,
        grid=(2, bg, rows // tr),
        in_specs=[pl.BlockSpec((None, 1, tr, width), lambda s, i, r: (s, i, r, 0)),
                  pl.BlockSpec((1, 1, width), lambda s, i, r: (s, 0, 0)),
                  pl.BlockSpec((1, width, D_CMP_HIDDEN), lambda s, i, r: (s, 0, 0)),
                  pl.BlockSpec((1, D_CMP_HIDDEN, HEAD_DIM), lambda s, i, r: (s, 0, 0))],
        out_specs=pl.BlockSpec((None, 1, tr, HEAD_DIM), lambda s, i, r: (s, i, r, 0)),
        out_shape=jax.ShapeDtypeStruct((2, bg, rows, HEAD_DIM), F32),
        compiler_params=_cparams(("parallel", "parallel", "parallel")),
        name="nsa_compress",
    )(flat, pos, w1, w2)


def _cmp_bias(table):
    tt = jnp.arange(BLK)[:, None]
    m = jnp.arange(CMP_WIN)[None, :] + (BLK // D_CMP_STRIDE) - CMP_WIN
    dist = tt - D_CMP_STRIDE * m - (D_CMP_LEN - 1)
    return jnp.where((dist >= 0)[None], _bias_lookup(table, dist), NEG_INF)


def _sel_matrix():
    c_rel = np.arange(CMP_WIN)[:, None] + (BLK // D_CMP_STRIDE) - CMP_WIN
    j_rel = np.arange(REL_BLOCKS)[None, :] - (REL_BLOCKS - 2)
    return ((c_rel >= 4 * j_rel - 1) & (c_rel <= 4 * j_rel + 3)).astype(np.float32)


def _cmp_attn_kernel(q_ref, kc_ref, vc_ref, bias_ref, sel_ref, o_ref, idx_ref):
    i = pl.program_id(1)
    hpg = D_HEADS // D_KV_HEADS
    lane = lax.broadcasted_iota(jnp.int32, (1, LANES), 1)
    lo = lane < HEAD_DIM
    half_mask = (jnp.where(lo, 1.0, 0.0).astype(BF16), jnp.where(lo, 0.0, 1.0).astype(BF16))
    row = lax.broadcasted_iota(jnp.int32, (BLK, REL_BLOCKS), 0)
    jj = lax.broadcasted_iota(jnp.int32, (BLK, REL_BLOCKS), 1)
    cur = (REL_BLOCKS - 2) + (row >= D_SEL_LEN).astype(jnp.int32)
    first = (REL_BLOCKS - 2) - 2 * i
    exists = jj >= first
    forced = exists & ((jj == first) | (jj == cur) | (jj == cur - 1))
    valid = exists & (jj <= cur)
    jjf = jj.astype(F32)
    out_lane = lax.broadcasted_iota(jnp.int32, (BLK, LANES), 1)
    out_row = lax.broadcasted_iota(jnp.int32, (BLK, LANES), 0)
    cur_abs = 2 * i + (out_row >= D_SEL_LEN).astype(jnp.int32)
    tt_in_blk = out_row % D_SEL_LEN

    for width in CMP_WIDTHS:
        lo_w = width - CMP_WIDTHS[0] if width > CMP_WIDTHS[0] else -1

        @pl.when((8 * i + 8 <= width) & (8 * i + 8 > lo_w))
        def _(width=width):
            _cmp_attn_body(q_ref, kc_ref, vc_ref, bias_ref, sel_ref, o_ref, idx_ref, i, width, lo, half_mask,
                           forced, valid, jjf, first, out_lane, cur_abs, tt_in_blk)


def _cmp_attn_body(q_ref, kc_ref, vc_ref, bias_ref, sel_ref, o_ref, idx_ref, i, width, lo, half_mask,
                   forced, valid, jjf, first, out_lane, cur_abs, tt_in_blk):
    hpg = D_HEADS // D_KV_HEADS
    off = CMP_WIN - width
    start = pl.multiple_of(8 * i + 8 + off, 8)
    ucol = lax.broadcasted_iota(jnp.int32, (1, width), 1) + off
    edge = jnp.where(ucol < CMP_WIN - 8 - 8 * i, NEG_INF, 0.0).astype(F32)

    for g in range(D_KV_HEADS):
        kwin = kc_ref[0, g, pl.ds(start, width), :].astype(BF16)
        vwin = vc_ref[0, g, pl.ds(start, width), :].astype(BF16)
        qs = []
        for hh in range(hpg):
            h = g * hpg + hh
            qs.append(q_ref[0, :, (h // 2) * LANES:(h // 2 + 1) * LANES] * half_mask[h % 2])
        s_all = lax.dot_general(jnp.concatenate(qs, axis=0), kwin, (((1,), (1,)), ((), ())),
                                preferred_element_type=F32)
        imp = jnp.zeros((BLK, width), F32)
        pns = []
        for hh in range(hpg):
            h = g * hpg + hh
            s = s_all[hh * BLK:(hh + 1) * BLK] * (HEAD_DIM ** -0.5) + bias_ref[h, :, off:] + edge
            ok = s > 0.5 * NEG_INF
            m = jnp.max(s, axis=-1, keepdims=True)
            p = jnp.where(ok, jnp.exp(s - m), 0.0)
            den = jnp.maximum(jnp.sum(p, axis=-1, keepdims=True), 1e-30)
            pn = p * (1.0 / den)
            imp = imp + pn
            pns.append(pn.astype(BF16))
        o_all = jnp.dot(jnp.concatenate(pns, axis=0), vwin, preferred_element_type=F32)
        for pr in range(hpg // 2):
            pair = (g * hpg) // 2 + pr
            o_ref[0, :, pair * LANES:(pair + 1) * LANES] = jnp.where(
                lo, o_all[2 * pr * BLK:(2 * pr + 1) * BLK], o_all[(2 * pr + 1) * BLK:(2 * pr + 2) * BLK]
            ).astype(o_ref.dtype)

        sel = sel_ref[off:, :]
        hi = imp.astype(BF16)
        r1 = imp - hi.astype(F32)
        mid = r1.astype(BF16)
        low = (r1 - mid.astype(F32)).astype(BF16)
        imp_sel = (jnp.dot(hi, sel, preferred_element_type=F32)
                   + jnp.dot(mid, sel, preferred_element_type=F32)
                   + jnp.dot(low, sel, preferred_element_type=F32))
        score = jnp.where(forced, 1e9, jnp.where(valid, imp_sel, -1e9))
        picked = jnp.zeros((BLK, LANES), jnp.int32)
        for r in range(D_SEL_COUNT):
            m = jnp.max(score, axis=-1, keepdims=True)
            am = jnp.min(jnp.where(score == m, jjf, float(REL_BLOCKS)), axis=-1, keepdims=True)
            none = m < -5e8
            blk = am.astype(jnp.int32) - first
            key_blk = jnp.where(none, 0, blk)
            bias_row = jnp.where(none, SEL_NONE, jnp.minimum(cur_abs - blk, SEL_FAR)) * D_SEL_LEN + tt_in_blk
            picked = jnp.where(out_lane == r, key_blk, picked)
            picked = jnp.where(out_lane == D_SEL_COUNT + r, bias_row, picked)
            score = jnp.where(jjf == am, -jnp.inf, score)
        idx_ref[0, g] = picked


def _cmp_attention(slab, kc_pad, vc_pad, table, seq):
    b = slab.shape[0]
    n_blk = seq // BLK
    rows = kc_pad.shape[2]
    bias = _cmp_bias(table)
    sel = jnp.asarray(_sel_matrix(), BF16)
    return pl.pallas_call(
        _cmp_attn_kernel,
        grid=(b, n_blk),
        in_specs=[pl.BlockSpec((1, BLK, D_Q), lambda bi, i: (bi, i, OD_QD // D_Q)),
                  pl.BlockSpec((1, D_KV_HEADS, rows, LANES), lambda bi, i: (bi, 0, 0, 0)),
                  pl.BlockSpec((1, D_KV_HEADS, rows, LANES), lambda bi, i: (bi, 0, 0, 0)),
                  pl.BlockSpec((D_HEADS, BLK, CMP_WIN), lambda bi, i: (0, 0, 0)),
                  pl.BlockSpec((CMP_WIN, REL_BLOCKS), lambda bi, i: (0, 0))],
        out_specs=[pl.BlockSpec((1, BLK, D_Q), lambda bi, i: (bi, i, 0)),
                   pl.BlockSpec((1, D_KV_HEADS, BLK, LANES), lambda bi, i: (bi, 0, i, 0))],
        out_shape=[jax.ShapeDtypeStruct((b, seq, D_Q), BF16),
                   jax.ShapeDtypeStruct((b, D_KV_HEADS, seq, LANES), jnp.int32)],
        compiler_params=_cparams(("parallel", "arbitrary")),
        name="nsa_cmp_attention",
    )(slab, kc_pad, vc_pad, bias, sel)


def _sel_bias(table):
    hpg = D_HEADS // D_KV_HEADS
    delta = jnp.arange(SEL_FAR)[:, None, None]
    tt = jnp.arange(D_SEL_LEN)[None, :, None]
    l = jnp.arange(D_SEL_LEN)[None, None, :]
    dist = D_SEL_LEN * delta + tt - l
    near = jnp.where((dist >= 0)[None], _bias_lookup(table, dist), NEG_INF)
    far = jnp.broadcast_to(table.astype(F32)[REL_BUCKETS - 1][:, None, None, None],
                           (D_HEADS, 1, D_SEL_LEN, D_SEL_LEN))
    none = jnp.full((D_HEADS, 1, D_SEL_LEN, D_SEL_LEN), NEG_INF, F32)
    rows = jnp.concatenate([near, far, none], axis=1)
    half = D_SEL_LEN // 2
    rows = rows.reshape(D_KV_HEADS, hpg, SEL_NONE + 1, D_SEL_LEN, half, 2)
    rows = jnp.transpose(rows, (0, 2, 3, 5, 1, 4))
    rows = rows.reshape(D_KV_HEADS, (SEL_NONE + 1) * D_SEL_LEN, 2 * hpg, half)
    return jnp.tile(rows, (1, 1, 1, LANES // half))


SEL_IDX = 2 * D_SEL_COUNT
SEL_UNROLL = 16


def _sel_attn_kernel(idx_hbm, q_ref, kv_ref, bias_ref, o_ref, idx_smem, sem):
    n_g, n_i = pl.num_programs(1), pl.num_programs(2)
    step = (pl.program_id(0) * n_g + pl.program_id(1)) * n_i + pl.program_id(2)
    total = pl.num_programs(0) * n_g * n_i
    slot = step % 2

    tile_words = BLK * SEL_IDX

    def idx_copy(s, sl):
        dst = idx_smem.at[pl.ds(pl.multiple_of(sl * tile_words, tile_words), tile_words)]
        return pltpu.make_async_copy(idx_hbm.at[s], dst, sem.at[sl])

    @pl.when(step == 0)
    def _():
        idx_copy(0, 0).start()

    @pl.when(step + 1 < total)
    def _():
        idx_copy(step + 1, 1 - slot).start()

    idx_copy(step, slot).wait()

    lane = lax.broadcasted_iota(jnp.int32, (1, LANES), 1)
    lo = lane < HEAD_DIM
    hpg = D_HEADS // D_KV_HEADS
    quarter = D_SEL_LEN // 2
    n_quads = D_SEL_COUNT // 4

    def scores(tl):
        picks = idx_smem.at[pl.ds(slot * tile_words + tl * SEL_IDX, SEL_IDX)]
        qq = q_ref[0, 0, tl]
        qbd = jnp.concatenate([jnp.where(lo, qq, 0.0), jnp.where(lo, 0.0, qq)], axis=0).astype(BF16)
        ks, vs, bs = [], [], []
        for n in range(D_SEL_COUNT):
            kv = kv_ref[0, 0, picks[n]]
            ks.append(kv[:quarter])
            vs.append(kv[quarter:])
            bs.append(bias_ref[0, picks[D_SEL_COUNT + n]])
        s = lax.dot_general(qbd, jnp.concatenate(ks, axis=0), (((1,), (1,)), ((), ())),
                            preferred_element_type=F32)
        bias = jnp.concatenate(
            [jnp.where(lane < quarter, bs[4 * c],
                       jnp.where(lane < 2 * quarter, bs[4 * c + 1],
                                 jnp.where(lane < 3 * quarter, bs[4 * c + 2], bs[4 * c + 3])))
             for c in range(n_quads)], axis=1)
        return s * (HEAD_DIM ** -0.5) + bias, jnp.concatenate(vs, axis=0)

    def probs(s):
        m16 = jnp.max(s, axis=-1, keepdims=True)
        m = jnp.maximum(m16[:hpg], m16[hpg:])
        p = jnp.exp(s - jnp.concatenate([m, m], axis=0))
        d16 = jnp.sum(p, axis=-1, keepdims=True)
        return p.astype(BF16), d16[:hpg] + d16[hpg:]

    def body(it, carry):
        t0 = it * SEL_UNROLL
        sv = [scores(t0 + u) for u in range(SEL_UNROLL)]
        pd = [probs(s) for s, _ in sv]
        outs = []
        for (p, den), (_, vall) in zip(pd, sv):
            o2 = jnp.dot(p, vall, preferred_element_type=F32)
            outs.append(jnp.where(lo, o2[:hpg], o2[hpg:]) * (1.0 / den))
        for u in range(SEL_UNROLL):
            o_ref[0, 0, t0 + u] = outs[u]
        return carry

    lax.fori_loop(0, BLK // SEL_UNROLL, body, 0)


def _sel_attention(idx, q_sel, kv_sel, table, seq):
    b = q_sel.shape[0]
    hpg = D_HEADS // D_KV_HEADS
    n_blk = seq // BLK
    n_sb = seq // D_SEL_LEN
    bias = _sel_bias(table)
    kv_spec = pl.BlockSpec((1, 1, n_sb, D_SEL_LEN, LANES), lambda bi, g, i: (bi, g, 0, 0, 0))
    return pl.pallas_call(
        _sel_attn_kernel,
        grid=(b, D_KV_HEADS, n_blk),
        in_specs=[pl.BlockSpec(memory_space=pl.ANY),
                  pl.BlockSpec((1, 1, BLK, hpg, LANES), lambda bi, g, i: (bi, g, i, 0, 0)),
                  kv_spec,
                  pl.BlockSpec((1, (SEL_NONE + 1) * D_SEL_LEN, 2 * hpg, LANES), lambda bi, g, i: (g, 0, 0, 0))],
        out_specs=pl.BlockSpec((1, 1, BLK, hpg, LANES), lambda bi, g, i: (bi, g, i, 0, 0)),
        out_shape=jax.ShapeDtypeStruct((b, D_KV_HEADS, seq, hpg, LANES), F32),
        scratch_shapes=[pltpu.SMEM((2 * BLK * SEL_IDX,), jnp.int32), pltpu.SemaphoreType.DMA((2,))],
        compiler_params=_cparams(("arbitrary", "arbitrary", "arbitrary")),
        name="nsa_sel_attention",
    )(idx, q_sel, kv_sel, bias)


def _gate_expand():
    e = np.zeros((3, LANES, D_Q), np.float32)
    for h in range(D_HEADS):
        for c in range(3):
            e[c, 3 * h + c, h * HEAD_DIM:(h + 1) * HEAD_DIM] = 1.0
    return e


def _nsa_gate_kernel(gd_ref, e_ref, oc_ref, os_ref, ow_ref, out_ref):
    sg = jax.nn.sigmoid(gd_ref[...].astype(F32))
    hi = sg.astype(BF16)
    low = (sg - hi.astype(F32)).astype(BF16)
    acc = None
    for c, o_ref in enumerate((oc_ref, os_ref, ow_ref)):
        gate = (jnp.dot(hi, e_ref[c], preferred_element_type=F32)
                + jnp.dot(low, e_ref[c], preferred_element_type=F32))
        term = gate * o_ref[...].astype(F32)
        acc = term if acc is None else acc + term
    out_ref[...] = acc.astype(out_ref.dtype)


def _nsa_gate(slab2d, o_c, o_s, o_w, tm):
    m = slab2d.shape[0]
    e = jnp.asarray(_gate_expand(), BF16)
    spec = pl.BlockSpec((tm, D_Q), lambda i: (i, 0))
    return pl.pallas_call(
        _nsa_gate_kernel,
        grid=(m // tm,),
        in_specs=[pl.BlockSpec((tm, LANES), lambda i: (i, OD_GD // LANES)),
                  pl.BlockSpec((3, LANES, D_Q), lambda i: (0, 0, 0)),
                  spec, spec, spec],
        out_specs=spec,
        out_shape=jax.ShapeDtypeStruct((m, D_Q), BF16),
        compiler_params=_cparams(("parallel",)),
        name="nsa_gate",
    )(slab2d, e, o_c, o_s, o_w)


def _even_mixer(h, nw, w_in, sinks, w_out, rel_table, b, seq):
    qa, ka, va, qb, kb, vb = jnp.split(w_in, [int(c) for c in np.cumsum([A_Q, A_KV, A_KV, B_W, B_W])], axis=1)
    pad = jnp.zeros((D_MODEL, EVEN_SLAB - w_in.shape[1]), w_in.dtype)
    w_slab = jnp.concatenate([qa, qb, kb, vb, ka, va, pad], axis=1).astype(BF16)
    slab = _norm_matmul(h, nw, w_slab, 1024, 512).reshape(b, seq, EVEN_SLAB)

    (oa,) = _banded_attention(slab, seq=seq, dil=1, width=EVEN_SLAB, q_off=EV_QA, k_off=EV_KA, v_off=EV_VA,
                              n_heads=A_HEADS, n_groups=A_KV_HEADS, max_dist=A_WINDOW - 1,
                              table=rel_table[:, :A_HEADS], sinks=sinks)
    outs, lses = [], []
    for window, dil in B_PATTERNS:
        o, lse = _banded_attention(slab, seq=seq, dil=dil, width=EVEN_SLAB, q_off=EV_QB, k_off=EV_KB,
                                   v_off=EV_VB, n_heads=B_HEADS, n_groups=B_HEADS, max_dist=window // dil,
                                   table=rel_table[:, A_HEADS:A_HEADS + B_HEADS], want_lse=True, out_dtype=F32)
        outs.append(o.reshape(b * seq, B_W))
        lses.append(lse.reshape(b * seq, B_W))
    ob = _dilated_mix(outs, lses, 1024)
    w_out = w_out.astype(BF16)
    return _proj_residual(h, [oa.reshape(b * seq, A_Q), ob], [w_out[:A_Q], w_out[A_Q:]], 512)


def _nsa(slab, pos_k, pos_v, k_w1, k_w2, v_w1, v_w2, rel_table, b, seq):
    g_kv = D_KV_HEADS
    hpg = D_HEADS // g_kv
    slab2d = slab.reshape(b * seq, ODD_SLAB)
    n_rows = seq // D_CMP_STRIDE

    def rows16(off):
        a = slab[:, :, off:off + D_KV].reshape(b, n_rows, D_CMP_STRIDE, g_kv, HEAD_DIM)
        a = jnp.transpose(a, (0, 3, 1, 2, 4)).reshape(b * g_kv, n_rows, D_CMP_STRIDE * HEAD_DIM)
        nxt = jnp.concatenate([a[:, 1:], jnp.zeros_like(a[:, :1])], axis=1)
        return jnp.concatenate([a, nxt], axis=-1)

    flat = jnp.stack([rows16(OD_KCMP), rows16(OD_VCMP)])
    pos = jnp.stack([pos_k.reshape(1, -1), pos_v.reshape(1, -1)]).astype(F32)
    w1 = jnp.stack([k_w1, v_w1]).astype(BF16)
    w2 = jnp.stack([k_w2, v_w2]).astype(BF16)
    cmp = _compress(flat, pos, w1, w2).reshape(2, b, g_kv, n_rows, HEAD_DIM)
    cmp = jnp.pad(cmp, ((0, 0), (0, 0), (0, 0), (CMP_WIN, 0), (0, 0)))
    cmp = jnp.concatenate([cmp, cmp], axis=-1)
    o_c, idx = _cmp_attention(slab, cmp[0], cmp[1], rel_table, seq)

    idx = idx[..., :SEL_IDX].reshape(b * g_kv * (seq // BLK), BLK * SEL_IDX)

    def per_group(off):
        return slab[:, :, off:off + D_KV].reshape(b, seq, g_kv, HEAD_DIM)

    def two_per_row(off):
        a = jnp.transpose(per_group(off), (0, 2, 1, 3))
        return a.reshape(b, g_kv, seq // D_SEL_LEN, D_SEL_LEN // 2, LANES)

    q_sel = slab[:, :, OD_QD:OD_QD + D_Q].reshape(b, seq, g_kv, hpg, HEAD_DIM)
    q_sel = jnp.transpose(q_sel, (0, 2, 1, 3, 4)).astype(F32)
    q_sel = jnp.concatenate([q_sel, q_sel], axis=-1)
    kv_sel = jnp.concatenate([two_per_row(OD_KSLC), two_per_row(OD_VSLC)], axis=3)
    o_s = _sel_attention(idx, q_sel, kv_sel, rel_table, seq)
    o_s = o_s[..., :HEAD_DIM] + o_s[..., HEAD_DIM:]
    o_s = jnp.transpose(o_s, (0, 2, 1, 3, 4)).reshape(b * seq, D_Q).astype(BF16)

    (o_w,) = _banded_attention(slab, seq=seq, dil=1, width=ODD_SLAB, q_off=OD_QD, k_off=OD_KWIN, v_off=OD_VWIN,
                               n_heads=D_HEADS, n_groups=D_KV_HEADS, max_dist=D_WINDOW - 1, table=rel_table)
    return _nsa_gate(slab2d, o_c.reshape(b * seq, D_Q), o_s, o_w.reshape(b * seq, D_Q), 1024)


def _odd_mixer(h, nw, w_in, ret_gn, pos_k, pos_v, k_w1, k_w2, v_w1, v_w2, w_out, rel_table, b, seq):
    pad = jnp.zeros((D_MODEL, ODD_SLAB - ODD_IN), w_in.dtype)
    w_slab = jnp.concatenate([w_in, pad], axis=1).astype(BF16)
    slab = _norm_matmul(h, nw, w_slab, 1024, 512).reshape(b, seq, ODD_SLAB)
    oc = _retention(slab, ret_gn, seq)
    od = _nsa(slab, pos_k, pos_v, k_w1, k_w2, v_w1, v_w2, rel_table, b, seq)
    w_out = w_out.astype(BF16)
    return _proj_residual(h, [oc.reshape(b * seq, C_V), od], [w_out[:C_V], w_out[C_V:]], 512)


def kernel(x, rel_table, norm_mix, norm_ffn, norm_final, even_w_in, even_sinks, even_w_out, odd_w_in, odd_ret_gn, odd_cmp_pos_k, odd_cmp_pos_v, odd_cmp_k_w1, odd_cmp_k_w2, odd_cmp_v_w1, odd_cmp_v_w2, odd_w_out, ffn_w_gate, ffn_w_up, ffn_w_down):
    b, seq, d = x.shape
    h = x.reshape(b * seq, d)
    for layer in range(DEPTH):
        li = layer // 2
        if layer % 2 == 0:
            h = _even_mixer(h, norm_mix[layer], even_w_in[li], even_sinks[li], even_w_out[li], rel_table, b, seq)
        else:
            h = _odd_mixer(h, norm_mix[layer], odd_w_in[li], odd_ret_gn[li], odd_cmp_pos_k[li],
                           odd_cmp_pos_v[li], odd_cmp_k_w1[li], odd_cmp_k_w2[li], odd_cmp_v_w1[li],
                           odd_cmp_v_w2[li], odd_w_out[li], rel_table, b, seq)
        h = _ffn(h, norm_ffn[layer], ffn_w_gate[layer].astype(BF16), ffn_w_up[layer].astype(BF16),
                 ffn_w_down[layer].astype(BF16), norm_final, layer == DEPTH - 1, 1024, 256)
    return h.reshape(b, seq, d)
```

```python
import functools
import math

import numpy as np
import jax
import jax.numpy as jnp
from jax import lax
from jax.experimental import pallas as pl
from jax.experimental.pallas import tpu as pltpu

F32 = jnp.float32
BF16 = jnp.bfloat16

D_MODEL = 1024
DEPTH = 4
HEAD_DIM = 64
BLK = 128
NEG_INF = -1e30
REL_BUCKETS = 32
REL_MAX_DIST = 2048
A_HEADS = 8
A_KV_HEADS = 2
A_WINDOW = 128
B_HEADS = 8
B_PATTERNS = ((128, 1), (512, 4), (2048, 16))
C_HEADS = 4
C_QK_DIM = 256
C_V_DIM = 512
C_CHUNK = 128
D_HEADS = 16
D_KV_HEADS = 2
D_CMP_LEN = 32
D_CMP_STRIDE = 16
D_CMP_HIDDEN = 128
D_SEL_LEN = 64
D_SEL_COUNT = 16
D_WINDOW = 512
D_FF = 2816

A_Q = A_HEADS * HEAD_DIM
A_KV = A_KV_HEADS * HEAD_DIM
B_W = B_HEADS * HEAD_DIM
C_QK = C_HEADS * C_QK_DIM
C_V = C_HEADS * C_V_DIM
D_Q = D_HEADS * HEAD_DIM
D_KV = D_KV_HEADS * HEAD_DIM
ODD_IN = 2 * C_QK + 2 * C_V + D_Q + 6 * D_KV + 3 * D_HEADS

LANES = 128
VMEM_LIMIT = 56 * 1024 * 1024

EVEN_SLAB = 2560
EV_QA, EV_QB, EV_KB, EV_VB, EV_KA, EV_VA = 0, 512, 1024, 1536, 2048, 2176
ODD_SLAB = 8192
OD_QC, OD_KC, OD_VC, OD_GC, OD_QD = 0, 1024, 2048, 4096, 6144
OD_KCMP, OD_VCMP, OD_KSLC, OD_VSLC, OD_KWIN, OD_VWIN, OD_GD = 7168, 7296, 7424, 7552, 7680, 7808, 7936

SEL_FAR = 25
SEL_NONE = 26
CMP_WIN = 1024
CMP_WIDTHS = (256, 512, 768, 1024)
REL_BLOCKS = 256


def _cparams(sem):
    return pltpu.CompilerParams(dimension_semantics=sem, vmem_limit_bytes=VMEM_LIMIT)


def _t5_bucket(dist):
    max_exact = REL_BUCKETS // 2
    d = jnp.maximum(dist, 0)
    df = jnp.maximum(d, 1).astype(jnp.float32)
    large = max_exact + (jnp.log(df / max_exact) / math.log(REL_MAX_DIST / max_exact)
                         * (REL_BUCKETS - max_exact)).astype(jnp.int32)
    large = jnp.minimum(large, REL_BUCKETS - 1)
    return jnp.where(d < max_exact, d, large)


def _bias_lookup(table, dist):
    bucket = _t5_bucket(dist)[None]
    tab = table.astype(F32)
    expand = (slice(None),) + (None,) * dist.ndim
    out = jnp.zeros((tab.shape[1],) + dist.shape, F32)
    for b in range(REL_BUCKETS):
        out = jnp.where(bucket == b, tab[b][expand], out)
    return out


def _rms(x, w, eps=1e-6):
    return x * lax.rsqrt(jnp.mean(x * x, axis=-1, keepdims=True) + eps) * w


def _norm_matmul_kernel(h_ref, nw_ref, w_ref, o_ref, hn_ref):
    @pl.when(pl.program_id(1) == 0)
    def _():
        hn_ref[...] = _rms(h_ref[...], nw_ref[...]).astype(BF16)

    o_ref[...] = jnp.dot(hn_ref[...], w_ref[...], preferred_element_type=F32).astype(o_ref.dtype)


def _norm_matmul(h, nw, w, tm, tn):
    m, d = h.shape
    n = w.shape[1]
    return pl.pallas_call(
        _norm_matmul_kernel,
        grid=(m // tm, n // tn),
        in_specs=[pl.BlockSpec((tm, d), lambda i, j: (i, 0)),
                  pl.BlockSpec((1, d), lambda i, j: (0, 0)),
                  pl.BlockSpec((d, tn), lambda i, j: (0, j))],
        out_specs=pl.BlockSpec((tm, tn), lambda i, j: (i, j)),
        out_shape=jax.ShapeDtypeStruct((m, n), BF16),
        scratch_shapes=[pltpu.VMEM((tm, d), BF16)],
        compiler_params=_cparams(("parallel", "arbitrary")),
        name="norm_matmul",
    )(h, nw.reshape(1, d), w)


def _proj_residual_kernel(*refs, n_in):
    h_ref, out_ref = refs[0], refs[-1]
    acc = h_ref[...]
    for o_ref, w_ref in zip(refs[1:1 + n_in], refs[1 + n_in:1 + 2 * n_in]):
        acc = acc + jnp.dot(o_ref[...], w_ref[...], preferred_element_type=F32)
    out_ref[...] = acc


def _proj_residual(h, outs, ws, tm):
    m, d = h.shape
    n_in = len(outs)
    in_specs = [pl.BlockSpec((tm, d), lambda i: (i, 0))]
    in_specs += [pl.BlockSpec((tm, o.shape[1]), lambda i: (i, 0)) for o in outs]
    in_specs += [pl.BlockSpec(w.shape, lambda i: (0, 0)) for w in ws]
    return pl.pallas_call(
        functools.partial(_proj_residual_kernel, n_in=n_in),
        grid=(m // tm,),
        in_specs=in_specs,
        out_specs=pl.BlockSpec((tm, d), lambda i: (i, 0)),
        out_shape=jax.ShapeDtypeStruct((m, d), F32),
        compiler_params=_cparams(("parallel",)),
        name="proj_residual",
    )(h, *outs, *ws)


def _ffn_kernel(h_ref, nw_ref, wg_ref, wu_ref, wd_ref, fw_ref, o_ref, *, final):
    h = h_ref[...]
    hn = _rms(h, nw_ref[...]).astype(BF16)
    g = jnp.dot(hn, wg_ref[...], preferred_element_type=F32)
    u = jnp.dot(hn, wu_ref[...], preferred_element_type=F32)
    a = (jax.nn.silu(g) * u).astype(BF16)
    y = h + jnp.dot(a, wd_ref[...], preferred_element_type=F32)
    if final:
        y = _rms(y, fw_ref[...])
    o_ref[...] = y


def _ffn(h, nw, wg, wu, wd, fw, final, tm):
    m, d = h.shape
    ff = wg.shape[1]
    resident = dict(pipeline_mode=pl.Buffered(1))
    return pl.pallas_call(
        functools.partial(_ffn_kernel, final=final),
        grid=(m // tm,),
        in_specs=[pl.BlockSpec((tm, d), lambda i: (i, 0)),
                  pl.BlockSpec((1, d), lambda i: (0, 0)),
                  pl.BlockSpec((d, ff), lambda i: (0, 0), **resident),
                  pl.BlockSpec((d, ff), lambda i: (0, 0), **resident),
                  pl.BlockSpec((ff, d), lambda i: (0, 0), **resident),
                  pl.BlockSpec((1, d), lambda i: (0, 0))],
        out_specs=pl.BlockSpec((tm, d), lambda i: (i, 0)),
        out_shape=jax.ShapeDtypeStruct((m, d), F32),
        compiler_params=_cparams(("parallel",)),
        name="ffn",
    )(h, nw.reshape(1, d), wg, wu, wd, fw.reshape(1, d))


def _band_bias(table, max_dist, dist_scale, nb):
    kw = (nb + 1) * BLK
    rel = jnp.arange(BLK)[:, None] + nb * BLK - jnp.arange(kw)[None, :]
    band = (rel >= 0) & (rel <= max_dist)
    return jnp.where(band[None], _bias_lookup(table, rel * dist_scale), NEG_INF)


def _swap_halves(x):
    return jnp.concatenate([x[:, HEAD_DIM:], x[:, :HEAD_DIM]], axis=1)


def _banded_kernel(*refs, n_heads, n_groups, nb, has_sinks, want_lse):
    pos = 0
    if has_sinks:
        sink_ref = refs[0]
        pos = 1
    q_ref = refs[pos]
    k_refs = refs[pos + 1:pos + 2 + nb]
    v_refs = refs[pos + 2 + nb:pos + 3 + 2 * nb]
    bias_ref = refs[pos + 3 + 2 * nb]
    o_ref = refs[pos + 4 + 2 * nb]
    lse_ref = refs[pos + 5 + 2 * nb] if want_lse else None

    i = pl.program_id(1)
    n_batch = q_ref.shape[0]
    kw = (nb + 1) * BLK
    hpg = n_heads // n_groups
    lane = lax.broadcasted_iota(jnp.int32, (1, LANES), 1)
    lo = lane < HEAD_DIM
    half_mask = (jnp.where(lo, 1.0, 0.0).astype(BF16), jnp.where(lo, 0.0, 1.0).astype(BF16))
    col = lax.broadcasted_iota(jnp.int32, (1, kw), 1)
    edge = jnp.where(col < (nb - i) * BLK, NEG_INF, 0.0).astype(F32)

    operands = []
    for bi in range(n_batch):
        kcat = jnp.concatenate([k_refs[nb - jj][bi] for jj in range(nb + 1)], axis=0)
        vcat = jnp.concatenate([v_refs[nb - jj][bi] for jj in range(nb + 1)], axis=0)
        if hpg == 1:
            for p in range(n_heads // 2):
                operands.append((bi, kcat[:, p * LANES:(p + 1) * LANES], vcat[:, p * LANES:(p + 1) * LANES],
                                 [2 * p, 2 * p + 1]))
        else:
            k_sw = _swap_halves(kcat)
            v_sw = _swap_halves(vcat)
            for g in range(n_groups):
                for par in range(2):
                    heads = [h for h in range(g * hpg, (g + 1) * hpg) if h % 2 == par]
                    operands.append((bi, kcat if g == par else k_sw, vcat if g == par else v_sw, heads))

    scores = []
    for bi, kh, _, heads in operands:
        qz = jnp.concatenate([q_ref[bi, :, (h // 2) * LANES:(h // 2 + 1) * LANES] * half_mask[h % 2]
                              for h in heads], axis=0)
        scores.append(lax.dot_general(qz, kh, (((1,), (1,)), ((), ())), preferred_element_type=F32))
    ms, dens, probs = {}, {}, []
    for (bi, _, _, heads), s_all in zip(operands, scores):
        ps = []
        for r, h in enumerate(heads):
            s = s_all[r * BLK:(r + 1) * BLK] * (HEAD_DIM ** -0.5) + bias_ref[h] + edge
            m = jnp.max(s, axis=-1, keepdims=True)
            if has_sinks:
                m = jnp.maximum(m, sink_ref[h])
            p = jnp.exp(s - m)
            den = jnp.sum(p, axis=-1, keepdims=True)
            if has_sinks:
                den = den + jnp.exp(sink_ref[h] - m)
            ps.append(p.astype(BF16))
            ms[bi, h], dens[bi, h] = m, den
        probs.append(jnp.concatenate(ps, axis=0))
    outs = {}
    for (bi, _, vh, heads), p_all in zip(operands, probs):
        o_all = jnp.dot(p_all, vh, preferred_element_type=F32)
        for r, h in enumerate(heads):
            outs[bi, h] = o_all[r * BLK:(r + 1) * BLK]
    for bi in range(n_batch):
        for pair in range(n_heads // 2):
            h0, h1 = (bi, 2 * pair), (bi, 2 * pair + 1)
            inv = jnp.where(lo, 1.0 / dens[h0], 1.0 / dens[h1])
            o_pair = jnp.where(lo, outs[h0], outs[h1]) * inv
            o_ref[bi, :, pair * LANES:(pair + 1) * LANES] = o_pair.astype(o_ref.dtype)
            if want_lse:
                lse_ref[bi, :, pair * LANES:(pair + 1) * LANES] = jnp.where(
                    lo, ms[h0] + jnp.log(dens[h0]), ms[h1] + jnp.log(dens[h1]))


def _banded_attention(slab, *, seq, dil, width, q_off, k_off, v_off, n_heads, n_groups,
                      max_dist, table, sinks=None, want_lse=False, out_dtype=BF16):
    b = slab.shape[0]
    length = seq // dil
    n_blk = length // BLK
    nb = -(-max_dist // BLK)
    kw = (nb + 1) * BLK
    hd = n_heads * HEAD_DIM
    gd = n_groups * HEAD_DIM
    view = slab.reshape(b, length, dil * width)
    bias = _band_bias(table, max_dist, dil, nb)

    def q_map(r, i):
        return (0, i, (r * width + q_off) // hd)

    def kv_map(off, j):
        return lambda r, i: (0, jnp.maximum(i - j, 0), (r * width + off) // gd)

    in_specs, args = [], []
    if sinks is not None:
        in_specs.append(pl.BlockSpec(memory_space=pltpu.SMEM))
        args.append(sinks.astype(F32))
    in_specs.append(pl.BlockSpec((b, BLK, hd), q_map))
    args.append(view)
    for off in (k_off, v_off):
        for j in range(nb + 1):
            in_specs.append(pl.BlockSpec((b, BLK, gd), kv_map(off, j)))
            args.append(view)
    in_specs.append(pl.BlockSpec((n_heads, BLK, kw), lambda r, i: (0, 0, 0)))
    args.append(bias)

    out_spec = pl.BlockSpec((b, BLK, hd), lambda r, i: (0, i, r))
    out_shape = [jax.ShapeDtypeStruct((b, length, dil * hd), out_dtype)]
    out_specs = [out_spec]
    if want_lse:
        out_shape.append(jax.ShapeDtypeStruct((b, length, dil * hd), F32))
        out_specs.append(out_spec)

    res = pl.pallas_call(
        functools.partial(_banded_kernel, n_heads=n_heads, n_groups=n_groups, nb=nb,
                          has_sinks=sinks is not None, want_lse=want_lse),
        grid=(dil, n_blk),
        in_specs=in_specs,
        out_specs=out_specs,
        out_shape=out_shape,
        compiler_params=_cparams(("parallel", "arbitrary")),
        name="banded_attention",
    )(*args)
    return [r.reshape(b, seq, hd) for r in res]


def _dilated_mix_kernel(o0, o1, o2, l0, l1, l2, out_ref):
    ls = [l0[...], l1[...], l2[...]]
    m = jnp.maximum(jnp.maximum(ls[0], ls[1]), ls[2])
    es = [jnp.exp(l - m) for l in ls]
    den = es[0] + es[1] + es[2]
    acc = (es[0] / den) * o0[...] + (es[1] / den) * o1[...] + (es[2] / den) * o2[...]
    out_ref[...] = acc.astype(out_ref.dtype)


def _dilated_mix(outs, lses, tm):
    m, c = outs[0].shape
    spec = pl.BlockSpec((tm, c), lambda i: (i, 0))
    return pl.pallas_call(
        _dilated_mix_kernel,
        grid=(m // tm,),
        in_specs=[spec] * 6,
        out_specs=spec,
        out_shape=jax.ShapeDtypeStruct((m, c), BF16),
        compiler_params=_cparams(("parallel",)),
        name="dilated_mix",
    )(*outs, *lses)


def _retention_kernel(q_ref, k_ref, v_ref, g_ref, cos_ref, sin_ref, dmask_ref, qdec_ref, kdec_ref,
                      cdec_ref, gn_ref, o_ref, state_ref):
    hd = pl.program_id(0)

    @pl.when(pl.program_id(1) == 0)
    def _():
        state_ref[...] = jnp.zeros_like(state_ref)

    cos = cos_ref[...]
    sin = sin_ref[...]
    half = C_QK_DIM // 2

    def rot(x):
        x1, x2 = x[:, :half], x[:, half:]
        return jnp.concatenate([x1 * cos - x2 * sin, x1 * sin + x2 * cos], axis=1)

    for bi in range(q_ref.shape[0]):
        q = rot(q_ref[bi].astype(F32))
        k = rot(k_ref[bi].astype(F32)) * (C_QK_DIM ** -0.5)
        v = v_ref[bi]
        qb = q.astype(BF16)
        inner = lax.dot_general(qb, k.astype(BF16), (((1,), (1,)), ((), ())),
                                preferred_element_type=F32) * dmask_ref[0]
        state = state_ref[bi]
        o = jnp.dot(inner.astype(BF16), v, preferred_element_type=F32)
        o = o + jnp.dot(qb, state.astype(BF16), preferred_element_type=F32) * qdec_ref[0]
        kd_t = jnp.transpose(k * kdec_ref[0]).astype(BF16)
        state_ref[bi] = state * cdec_ref[hd] + jnp.dot(kd_t, v, preferred_element_type=F32)

        mu = jnp.mean(o, axis=-1, keepdims=True)
        oc = o - mu
        var = jnp.mean(oc * oc, axis=-1, keepdims=True)
        on = oc * lax.rsqrt(var + 1e-5)
        o_ref[bi] = (on * gn_ref[...] * jax.nn.silu(g_ref[bi].astype(F32))).astype(o_ref.dtype)


def _retention(slab, gn, seq):
    b = slab.shape[0]
    n_chunks = seq // C_CHUNK
    half = C_QK_DIM // 2
    pos = jnp.arange(seq, dtype=F32)
    inv = 1.0 / (10000.0 ** (jnp.arange(0, C_QK_DIM, 2, dtype=F32) / C_QK_DIM))
    ang = pos[:, None] * inv[None, :]
    cos, sin = jnp.cos(ang), jnp.sin(ang)
    log_g = jnp.log(1.0 - 2.0 ** (-5.0 - jnp.arange(C_HEADS, dtype=F32)))
    j = jnp.arange(C_CHUNK, dtype=F32)
    diff = j[:, None] - j[None, :]
    dmask = jnp.where(diff >= 0, jnp.exp(diff[None] * log_g[:, None, None]), 0.0)
    q_dec = jnp.exp((j[None, :] + 1.0) * log_g[:, None])[:, :, None]
    k_dec = jnp.exp((C_CHUNK - 1.0 - j[None, :]) * log_g[:, None])[:, :, None]
    chunk_dec = jnp.exp(C_CHUNK * log_g)

    def col(off, w):
        return lambda h, c: (0, c, off // w + h)

    return pl.pallas_call(
        _retention_kernel,
        grid=(C_HEADS, n_chunks),
        in_specs=[pl.BlockSpec((b, C_CHUNK, C_QK_DIM), col(OD_QC, C_QK_DIM)),
                  pl.BlockSpec((b, C_CHUNK, C_QK_DIM), col(OD_KC, C_QK_DIM)),
                  pl.BlockSpec((b, C_CHUNK, C_V_DIM), col(OD_VC, C_V_DIM)),
                  pl.BlockSpec((b, C_CHUNK, C_V_DIM), col(OD_GC, C_V_DIM)),
                  pl.BlockSpec((C_CHUNK, half), lambda h, c: (c, 0)),
                  pl.BlockSpec((C_CHUNK, half), lambda h, c: (c, 0)),
                  pl.BlockSpec((1, C_CHUNK, C_CHUNK), lambda h, c: (h, 0, 0)),
                  pl.BlockSpec((1, C_CHUNK, 1), lambda h, c: (h, 0, 0)),
                  pl.BlockSpec((1, C_CHUNK, 1), lambda h, c: (h, 0, 0)),
                  pl.BlockSpec(memory_space=pltpu.SMEM),
                  pl.BlockSpec((1, C_V_DIM), lambda h, c: (0, h))],
        out_specs=pl.BlockSpec((b, C_CHUNK, C_V_DIM), lambda h, c: (0, c, h)),
        out_shape=jax.ShapeDtypeStruct((b, seq, C_V), BF16),
        scratch_shapes=[pltpu.VMEM((b, C_QK_DIM, C_V_DIM), F32)],
        compiler_params=_cparams(("parallel", "arbitrary")),
        name="retention",
    )(slab, slab, slab, slab, cos, sin, dmask, q_dec, k_dec, chunk_dec, gn.reshape(1, C_V).astype(F32))


def _compress_kernel(x_ref, pos_ref, w1_ref, w2_ref, o_ref):
    x = (x_ref[0].astype(F32) + pos_ref[0]).astype(BF16)
    hid = jax.nn.gelu(jnp.dot(x, w1_ref[0], preferred_element_type=F32))
    o_ref[0] = jnp.dot(hid.astype(BF16), w2_ref[0], preferred_element_type=F32)


def _compress(flat, pos, w1, w2):
    _, bg, rows, width = flat.shape
    tr = min(rows, 256)
    return pl.pallas_call(
        _compress_kernel,
        grid=(2, bg, rows // tr),
        in_specs=[pl.BlockSpec((None, 1, tr, width), lambda s, i, r: (s, i, r, 0)),
                  pl.BlockSpec((1, 1, width), lambda s, i, r: (s, 0, 0)),
                  pl.BlockSpec((1, width, D_CMP_HIDDEN), lambda s, i, r: (s, 0, 0)),
                  pl.BlockSpec((1, D_CMP_HIDDEN, HEAD_DIM), lambda s, i, r: (s, 0, 0))],
        out_specs=pl.BlockSpec((None, 1, tr, HEAD_DIM), lambda s, i, r: (s, i, r, 0)),
        out_shape=jax.ShapeDtypeStruct((2, bg, rows, HEAD_DIM), F32),
        compiler_params=_cparams(("parallel", "parallel", "parallel")),
        name="nsa_compress",
    )(flat, pos, w1, w2)


def _cmp_bias(table):
    tt = jnp.arange(BLK)[:, None]
    m = jnp.arange(CMP_WIN)[None, :] + (BLK // D_CMP_STRIDE) - CMP_WIN
    dist = tt - D_CMP_STRIDE * m - (D_CMP_LEN - 1)
    return jnp.where((dist >= 0)[None], _bias_lookup(table, dist), NEG_INF)


def _sel_matrix():
    c_rel = np.arange(CMP_WIN)[:, None] + (BLK // D_CMP_STRIDE) - CMP_WIN
    j_rel = np.arange(REL_BLOCKS)[None, :] - (REL_BLOCKS - 2)
    return ((c_rel >= 4 * j_rel - 1) & (c_rel <= 4 * j_rel + 3)).astype(np.float32)


def _cmp_attn_kernel(q_ref, kc_ref, vc_ref, bias_ref, sel_ref, o_ref, idx_ref):
    i = pl.program_id(1)
    hpg = D_HEADS // D_KV_HEADS
    lane = lax.broadcasted_iota(jnp.int32, (1, LANES), 1)
    lo = lane < HEAD_DIM
    half_mask = (jnp.where(lo, 1.0, 0.0).astype(BF16), jnp.where(lo, 0.0, 1.0).astype(BF16))
    row = lax.broadcasted_iota(jnp.int32, (BLK, REL_BLOCKS), 0)
    jj = lax.broadcasted_iota(jnp.int32, (BLK, REL_BLOCKS), 1)
    cur = (REL_BLOCKS - 2) + (row >= D_SEL_LEN).astype(jnp.int32)
    first = (REL_BLOCKS - 2) - 2 * i
    exists = jj >= first
    forced = exists & ((jj == first) | (jj == cur) | (jj == cur - 1))
    valid = exists & (jj <= cur)
    jjf = jj.astype(F32)
    out_lane = lax.broadcasted_iota(jnp.int32, (BLK, LANES), 1)
    out_row = lax.broadcasted_iota(jnp.int32, (BLK, LANES), 0)
    cur_abs = 2 * i + (out_row >= D_SEL_LEN).astype(jnp.int32)
    tt_in_blk = out_row % D_SEL_LEN

    for width in CMP_WIDTHS:
        lo_w = width - CMP_WIDTHS[0] if width > CMP_WIDTHS[0] else -1

        @pl.when((8 * i + 8 <= width) & (8 * i + 8 > lo_w))
        def _(width=width):
            _cmp_attn_body(q_ref, kc_ref, vc_ref, bias_ref, sel_ref, o_ref, idx_ref, i, width, lo, half_mask,
                           forced, valid, jjf, first, out_lane, cur_abs, tt_in_blk)


def _cmp_attn_body(q_ref, kc_ref, vc_ref, bias_ref, sel_ref, o_ref, idx_ref, i, width, lo, half_mask,
                   forced, valid, jjf, first, out_lane, cur_abs, tt_in_blk):
    hpg = D_HEADS // D_KV_HEADS
    off = CMP_WIN - width
    start = pl.multiple_of(8 * i + 8 + off, 8)
    ucol = lax.broadcasted_iota(jnp.int32, (1, width), 1) + off
    edge = jnp.where(ucol < CMP_WIN - 8 - 8 * i, NEG_INF, 0.0).astype(F32)

    for g in range(D_KV_HEADS):
        kwin = kc_ref[0, g, pl.ds(start, width), :].astype(BF16)
        vwin = vc_ref[0, g, pl.ds(start, width), :].astype(BF16)
        qs = []
        for hh in range(hpg):
            h = g * hpg + hh
            qs.append(q_ref[0, :, (h // 2) * LANES:(h // 2 + 1) * LANES] * half_mask[h % 2])
        s_all = lax.dot_general(jnp.concatenate(qs, axis=0), kwin, (((1,), (1,)), ((), ())),
                                preferred_element_type=F32)
        imp = jnp.zeros((BLK, width), F32)
        pns = []
        for hh in range(hpg):
            h = g * hpg + hh
            s = s_all[hh * BLK:(hh + 1) * BLK] * (HEAD_DIM ** -0.5) + bias_ref[h, :, off:] + edge
            ok = s > 0.5 * NEG_INF
            m = jnp.max(s, axis=-1, keepdims=True)
            p = jnp.where(ok, jnp.exp(s - m), 0.0)
            den = jnp.maximum(jnp.sum(p, axis=-1, keepdims=True), 1e-30)
            pn = p * (1.0 / den)
            imp = imp + pn
            pns.append(pn.astype(BF16))
        o_all = jnp.dot(jnp.concatenate(pns, axis=0), vwin, preferred_element_type=F32)
        for pr in range(hpg // 2):
            pair = (g * hpg) // 2 + pr
            o_ref[0, :, pair * LANES:(pair + 1) * LANES] = jnp.where(
                lo, o_all[2 * pr * BLK:(2 * pr + 1) * BLK], o_all[(2 * pr + 1) * BLK:(2 * pr + 2) * BLK]
            ).astype(o_ref.dtype)

        sel = sel_ref[off:, :]
        hi = imp.astype(BF16)
        r1 = imp - hi.astype(F32)
        mid = r1.astype(BF16)
        low = (r1 - mid.astype(F32)).astype(BF16)
        imp_sel = (jnp.dot(hi, sel, preferred_element_type=F32)
                   + jnp.dot(mid, sel, preferred_element_type=F32)
                   + jnp.dot(low, sel, preferred_element_type=F32))
        score = jnp.where(forced, 1e9, jnp.where(valid, imp_sel, -1e9))
        picked = jnp.zeros((BLK, LANES), jnp.int32)
        for r in range(D_SEL_COUNT):
            m = jnp.max(score, axis=-1, keepdims=True)
            am = jnp.min(jnp.where(score == m, jjf, float(REL_BLOCKS)), axis=-1, keepdims=True)
            none = m < -5e8
            blk = am.astype(jnp.int32) - first
            key_blk = jnp.where(none, 0, blk)
            bias_row = jnp.where(none, SEL_NONE, jnp.minimum(cur_abs - blk, SEL_FAR)) * D_SEL_LEN + tt_in_blk
            picked = jnp.where(out_lane == r, key_blk, picked)
            picked = jnp.where(out_lane == D_SEL_COUNT + r, bias_row, picked)
            score = jnp.where(jjf == am, -jnp.inf, score)
        idx_ref[0, g] = picked


def _cmp_attention(slab, kc_pad, vc_pad, table, seq):
    b = slab.shape[0]
    n_blk = seq // BLK
    rows = kc_pad.shape[2]
    bias = _cmp_bias(table)
    sel = jnp.asarray(_sel_matrix(), BF16)
    return pl.pallas_call(
        _cmp_attn_kernel,
        grid=(b, n_blk),
        in_specs=[pl.BlockSpec((1, BLK, D_Q), lambda bi, i: (bi, i, OD_QD // D_Q)),
                  pl.BlockSpec((1, D_KV_HEADS, rows, LANES), lambda bi, i: (bi, 0, 0, 0)),
                  pl.BlockSpec((1, D_KV_HEADS, rows, LANES), lambda bi, i: (bi, 0, 0, 0)),
                  pl.BlockSpec((D_HEADS, BLK, CMP_WIN), lambda bi, i: (0, 0, 0)),
                  pl.BlockSpec((CMP_WIN, REL_BLOCKS), lambda bi, i: (0, 0))],
        out_specs=[pl.BlockSpec((1, BLK, D_Q), lambda bi, i: (bi, i, 0)),
                   pl.BlockSpec((1, D_KV_HEADS, BLK, LANES), lambda bi, i: (bi, 0, i, 0))],
        out_shape=[jax.ShapeDtypeStruct((b, seq, D_Q), BF16),
                   jax.ShapeDtypeStruct((b, D_KV_HEADS, seq, LANES), jnp.int32)],
        compiler_params=_cparams(("parallel", "arbitrary")),
        name="nsa_cmp_attention",
    )(slab, kc_pad, vc_pad, bias, sel)


def _sel_bias(table):
    hpg = D_HEADS // D_KV_HEADS
    delta = jnp.arange(SEL_FAR)[:, None, None]
    tt = jnp.arange(D_SEL_LEN)[None, :, None]
    l = jnp.arange(D_SEL_LEN)[None, None, :]
    dist = D_SEL_LEN * delta + tt - l
    near = jnp.where((dist >= 0)[None], _bias_lookup(table, dist), NEG_INF)
    far = jnp.broadcast_to(table.astype(F32)[REL_BUCKETS - 1][:, None, None, None],
                           (D_HEADS, 1, D_SEL_LEN, D_SEL_LEN))
    none = jnp.full((D_HEADS, 1, D_SEL_LEN, D_SEL_LEN), NEG_INF, F32)
    rows = jnp.concatenate([near, far, none], axis=1)
    half = D_SEL_LEN // 2
    rows = rows.reshape(D_KV_HEADS, hpg, SEL_NONE + 1, D_SEL_LEN, half, 2)
    rows = jnp.transpose(rows, (0, 2, 3, 5, 1, 4))
    rows = rows.reshape(D_KV_HEADS, (SEL_NONE + 1) * D_SEL_LEN, 2 * hpg, half)
    return jnp.tile(rows, (1, 1, 1, LANES // half))


SEL_IDX = 2 * D_SEL_COUNT
SEL_UNROLL = 16


def _sel_attn_kernel(idx_hbm, q_ref, kv_ref, bias_ref, o_ref, idx_smem, sem):
    n_g, n_i = pl.num_programs(1), pl.num_programs(2)
    step = (pl.program_id(0) * n_g + pl.program_id(1)) * n_i + pl.program_id(2)
    total = pl.num_programs(0) * n_g * n_i
    slot = step % 2

    tile_words = BLK * SEL_IDX

    def idx_copy(s, sl):
        dst = idx_smem.at[pl.ds(pl.multiple_of(sl * tile_words, tile_words), tile_words)]
        return pltpu.make_async_copy(idx_hbm.at[s], dst, sem.at[sl])

    @pl.when(step == 0)
    def _():
        idx_copy(0, 0).start()

    @pl.when(step + 1 < total)
    def _():
        idx_copy(step + 1, 1 - slot).start()

    idx_copy(step, slot).wait()

    lane = lax.broadcasted_iota(jnp.int32, (1, LANES), 1)
    lo = lane < HEAD_DIM
    hpg = D_HEADS // D_KV_HEADS
    quarter = D_SEL_LEN // 2
    n_quads = D_SEL_COUNT // 4

    def scores(tl):
        picks = idx_smem.at[pl.ds(slot * tile_words + tl * SEL_IDX, SEL_IDX)]
        qq = q_ref[0, 0, tl]
        qbd = jnp.concatenate([jnp.where(lo, qq, 0.0), jnp.where(lo, 0.0, qq)], axis=0).astype(BF16)
        ks, vs, bs = [], [], []
        for n in range(D_SEL_COUNT):
            kv = kv_ref[0, 0, picks[n]]
            ks.append(kv[:quarter])
            vs.append(kv[quarter:])
            bs.append(bias_ref[0, picks[D_SEL_COUNT + n]])
        s = lax.dot_general(qbd, jnp.concatenate(ks, axis=0), (((1,), (1,)), ((), ())),
                            preferred_element_type=F32)
        bias = jnp.concatenate(
            [jnp.where(lane < quarter, bs[4 * c],
                       jnp.where(lane < 2 * quarter, bs[4 * c + 1],
                                 jnp.where(lane < 3 * quarter, bs[4 * c + 2], bs[4 * c + 3])))
             for c in range(n_quads)], axis=1)
        return s * (HEAD_DIM ** -0.5) + bias, jnp.concatenate(vs, axis=0)

    def probs(s):
        m16 = jnp.max(s, axis=-1, keepdims=True)
        m = jnp.maximum(m16[:hpg], m16[hpg:])
        p = jnp.exp(s - jnp.concatenate([m, m], axis=0))
        d16 = jnp.sum(p, axis=-1, keepdims=True)
        return p.astype(BF16), d16[:hpg] + d16[hpg:]

    def body(it, carry):
        t0 = it * SEL_UNROLL
        sv = [scores(t0 + u) for u in range(SEL_UNROLL)]
        pd = [probs(s) for s, _ in sv]
        outs = []
        for (p, den), (_, vall) in zip(pd, sv):
            o2 = jnp.dot(p, vall, preferred_element_type=F32)
            outs.append(jnp.where(lo, o2[:hpg], o2[hpg:]) * (1.0 / den))
        for u in range(SEL_UNROLL):
            o_ref[0, 0, t0 + u] = outs[u]
        return carry

    lax.fori_loop(0, BLK // SEL_UNROLL, body, 0)


def _sel_attention(idx, q_sel, kv_sel, table, seq):
    b = q_sel.shape[0]
    hpg = D_HEADS // D_KV_HEADS
    n_blk = seq // BLK
    n_sb = seq // D_SEL_LEN
    bias = _sel_bias(table)
    kv_spec = pl.BlockSpec((1, 1, n_sb, D_SEL_LEN, LANES), lambda bi, g, i: (bi, g, 0, 0, 0))
    return pl.pallas_call(
        _sel_attn_kernel,
        grid=(b, D_KV_HEADS, n_blk),
        in_specs=[pl.BlockSpec(memory_space=pl.ANY),
                  pl.BlockSpec((1, 1, BLK, hpg, LANES), lambda bi, g, i: (bi, g, i, 0, 0)),
                  kv_spec,
                  pl.BlockSpec((1, (SEL_NONE + 1) * D_SEL_LEN, 2 * hpg, LANES), lambda bi, g, i: (g, 0, 0, 0))],
        out_specs=pl.BlockSpec((1, 1, BLK, hpg, LANES), lambda bi, g, i: (bi, g, i, 0, 0)),
        out_shape=jax.ShapeDtypeStruct((b, D_KV_HEADS, seq, hpg, LANES), F32),
        scratch_shapes=[pltpu.SMEM((2 * BLK * SEL_IDX,), jnp.int32), pltpu.SemaphoreType.DMA((2,))],
        compiler_params=_cparams(("arbitrary", "arbitrary", "arbitrary")),
        name="nsa_sel_attention",
    )(idx, q_sel, kv_sel, bias)


def _gate_expand():
    e = np.zeros((3, LANES, D_Q), np.float32)
    for h in range(D_HEADS):
        for c in range(3):
            e[c, 3 * h + c, h * HEAD_DIM:(h + 1) * HEAD_DIM] = 1.0
    return e


def _nsa_gate_kernel(gd_ref, e_ref, oc_ref, os_ref, ow_ref, out_ref):
    sg = jax.nn.sigmoid(gd_ref[...].astype(F32))
    hi = sg.astype(BF16)
    low = (sg - hi.astype(F32)).astype(BF16)
    acc = None
    for c, o_ref in enumerate((oc_ref, os_ref, ow_ref)):
        gate = (jnp.dot(hi, e_ref[c], preferred_element_type=F32)
                + jnp.dot(low, e_ref[c], preferred_element_type=F32))
        term = gate * o_ref[...].astype(F32)
        acc = term if acc is None else acc + term
    out_ref[...] = acc.astype(out_ref.dtype)


def _nsa_gate(slab2d, o_c, o_s, o_w, tm):
    m = slab2d.shape[0]
    e = jnp.asarray(_gate_expand(), BF16)
    spec = pl.BlockSpec((tm, D_Q), lambda i: (i, 0))
    return pl.pallas_call(
        _nsa_gate_kernel,
        grid=(m // tm,),
        in_specs=[pl.BlockSpec((tm, LANES), lambda i: (i, OD_GD // LANES)),
                  pl.BlockSpec((3, LANES, D_Q), lambda i: (0, 0, 0)),
                  spec, spec, spec],
        out_specs=spec,
        out_shape=jax.ShapeDtypeStruct((m, D_Q), BF16),
        compiler_params=_cparams(("parallel",)),
        name="nsa_gate",
    )(slab2d, e, o_c, o_s, o_w)


def _even_mixer(h, nw, w_in, sinks, w_out, rel_table, b, seq):
    qa, ka, va, qb, kb, vb = jnp.split(w_in, [int(c) for c in np.cumsum([A_Q, A_KV, A_KV, B_W, B_W])], axis=1)
    pad = jnp.zeros((D_MODEL, EVEN_SLAB - w_in.shape[1]), w_in.dtype)
    w_slab = jnp.concatenate([qa, qb, kb, vb, ka, va, pad], axis=1).astype(BF16)
    slab = _norm_matmul(h, nw, w_slab, 1024, EVEN_SLAB // 2).reshape(b, seq, EVEN_SLAB)

    (oa,) = _banded_attention(slab, seq=seq, dil=1, width=EVEN_SLAB, q_off=EV_QA, k_off=EV_KA, v_off=EV_VA,
                              n_heads=A_HEADS, n_groups=A_KV_HEADS, max_dist=A_WINDOW - 1,
                              table=rel_table[:, :A_HEADS], sinks=sinks)
    outs, lses = [], []
    for window, dil in B_PATTERNS:
        o, lse = _banded_attention(slab, seq=seq, dil=dil, width=EVEN_SLAB, q_off=EV_QB, k_off=EV_KB,
                                   v_off=EV_VB, n_heads=B_HEADS, n_groups=B_HEADS, max_dist=window // dil,
                                   table=rel_table[:, A_HEADS:A_HEADS + B_HEADS], want_lse=True, out_dtype=F32)
        outs.append(o.reshape(b * seq, B_W))
        lses.append(lse.reshape(b * seq, B_W))
    ob = _dilated_mix(outs, lses, 1024)
    w_out = w_out.astype(BF16)
    return _proj_residual(h, [oa.reshape(b * seq, A_Q), ob], [w_out[:A_Q], w_out[A_Q:]], 512)


def _nsa(slab, pos_k, pos_v, k_w1, k_w2, v_w1, v_w2, rel_table, b, seq):
    g_kv = D_KV_HEADS
    hpg = D_HEADS // g_kv
    slab2d = slab.reshape(b * seq, ODD_SLAB)
    n_rows = seq // D_CMP_STRIDE

    def rows16(off):
        a = slab[:, :, off:off + D_KV].reshape(b, n_rows, D_CMP_STRIDE, g_kv, HEAD_DIM)
        a = jnp.transpose(a, (0, 3, 1, 2, 4)).reshape(b * g_kv, n_rows, D_CMP_STRIDE * HEAD_DIM)
        nxt = jnp.concatenate([a[:, 1:], jnp.zeros_like(a[:, :1])], axis=1)
        return jnp.concatenate([a, nxt], axis=-1)

    flat = jnp.stack([rows16(OD_KCMP), rows16(OD_VCMP)])
    pos = jnp.stack([pos_k.reshape(1, -1), pos_v.reshape(1, -1)]).astype(F32)
    w1 = jnp.stack([k_w1, v_w1]).astype(BF16)
    w2 = jnp.stack([k_w2, v_w2]).astype(BF16)
    cmp = _compress(flat, pos, w1, w2).reshape(2, b, g_kv, n_rows, HEAD_DIM)
    cmp = jnp.pad(cmp, ((0, 0), (0, 0), (0, 0), (CMP_WIN, 0), (0, 0)))
    cmp = jnp.concatenate([cmp, cmp], axis=-1)
    o_c, idx = _cmp_attention(slab, cmp[0], cmp[1], rel_table, seq)

    idx = idx[..., :SEL_IDX].reshape(b * g_kv * (seq // BLK), BLK * SEL_IDX)

    def per_group(off):
        return slab[:, :, off:off + D_KV].reshape(b, seq, g_kv, HEAD_DIM)

    def two_per_row(off):
        a = jnp.transpose(per_group(off), (0, 2, 1, 3))
        return a.reshape(b, g_kv, seq // D_SEL_LEN, D_SEL_LEN // 2, LANES)

    q_sel = slab[:, :, OD_QD:OD_QD + D_Q].reshape(b, seq, g_kv, hpg, HEAD_DIM)
    q_sel = jnp.transpose(q_sel, (0, 2, 1, 3, 4)).astype(F32)
    q_sel = jnp.concatenate([q_sel, q_sel], axis=-1)
    kv_sel = jnp.concatenate([two_per_row(OD_KSLC), two_per_row(OD_VSLC)], axis=3)
    o_s = _sel_attention(idx, q_sel, kv_sel, rel_table, seq)
    o_s = o_s[..., :HEAD_DIM] + o_s[..., HEAD_DIM:]
    o_s = jnp.transpose(o_s, (0, 2, 1, 3, 4)).reshape(b * seq, D_Q).astype(BF16)

    (o_w,) = _banded_attention(slab, seq=seq, dil=1, width=ODD_SLAB, q_off=OD_QD, k_off=OD_KWIN, v_off=OD_VWIN,
                               n_heads=D_HEADS, n_groups=D_KV_HEADS, max_dist=D_WINDOW - 1, table=rel_table)
    return _nsa_gate(slab2d, o_c.reshape(b * seq, D_Q), o_s, o_w.reshape(b * seq, D_Q), 1024)


def _odd_mixer(h, nw, w_in, ret_gn, pos_k, pos_v, k_w1, k_w2, v_w1, v_w2, w_out, rel_table, b, seq):
    pad = jnp.zeros((D_MODEL, ODD_SLAB - ODD_IN), w_in.dtype)
    w_slab = jnp.concatenate([w_in, pad], axis=1).astype(BF16)
    slab = _norm_matmul(h, nw, w_slab, 1024, ODD_SLAB // 4).reshape(b, seq, ODD_SLAB)
    oc = _retention(slab, ret_gn, seq)
    od = _nsa(slab, pos_k, pos_v, k_w1, k_w2, v_w1, v_w2, rel_table, b, seq)
    w_out = w_out.astype(BF16)
    return _proj_residual(h, [oc.reshape(b * seq, C_V), od], [w_out[:C_V], w_out[C_V:]], 512)


def kernel(x, rel_table, norm_mix, norm_ffn, norm_final, even_w_in, even_sinks, even_w_out, odd_w_in, odd_ret_gn, odd_cmp_pos_k, odd_cmp_pos_v, odd_cmp_k_w1, odd_cmp_k_w2, odd_cmp_v_w1, odd_cmp_v_w2, odd_w_out, ffn_w_gate, ffn_w_up, ffn_w_down):
    b, seq, d = x.shape
    h = x.reshape(b * seq, d)
    for layer in range(DEPTH):
        li = layer // 2
        if layer % 2 == 0:
            h = _even_mixer(h, norm_mix[layer], even_w_in[li], even_sinks[li], even_w_out[li], rel_table, b, seq)
        else:
            h = _odd_mixer(h, norm_mix[layer], odd_w_in[li], odd_ret_gn[li], odd_cmp_pos_k[li],
                           odd_cmp_pos_v[li], odd_cmp_k_w1[li], odd_cmp_k_w2[li], odd_cmp_v_w1[li],
                           odd_cmp_v_w2[li], odd_w_out[li], rel_table, b, seq)
        h = _ffn(h, norm_ffn[layer], ffn_w_gate[layer].astype(BF16), ffn_w_up[layer].astype(BF16),
                 ffn_w_down[layer].astype(BF16), norm_final, layer == DEPTH - 1, 512)
    return h.reshape(b, seq, d)
```

```python
import functools
import math

import numpy as np
import jax
import jax.numpy as jnp
from jax import lax
from jax.experimental import pallas as pl
from jax.experimental.pallas import tpu as pltpu

F32 = jnp.float32
BF16 = jnp.bfloat16

D_MODEL = 1024
DEPTH = 4
HEAD_DIM = 64
BLK = 128
NEG_INF = -1e30
REL_BUCKETS = 32
REL_MAX_DIST = 2048
A_HEADS = 8
A_KV_HEADS = 2
A_WINDOW = 128
B_HEADS = 8
B_PATTERNS = ((128, 1), (512, 4), (2048, 16))
C_HEADS = 4
C_QK_DIM = 256
C_V_DIM = 512
C_CHUNK = 128
D_HEADS = 16
D_KV_HEADS = 2
D_CMP_LEN = 32
D_CMP_STRIDE = 16
D_CMP_HIDDEN = 128
D_SEL_LEN = 64
D_SEL_COUNT = 16
D_WINDOW = 512
D_FF = 2816

A_Q = A_HEADS * HEAD_DIM
A_KV = A_KV_HEADS * HEAD_DIM
B_W = B_HEADS * HEAD_DIM
C_QK = C_HEADS * C_QK_DIM
C_V = C_HEADS * C_V_DIM
D_Q = D_HEADS * HEAD_DIM
D_KV = D_KV_HEADS * HEAD_DIM
ODD_IN = 2 * C_QK + 2 * C_V + D_Q + 6 * D_KV + 3 * D_HEADS

LANES = 128
VMEM_LIMIT = 56 * 1024 * 1024

EVEN_SLAB = 2560
EV_QA, EV_QB, EV_KB, EV_VB, EV_KA, EV_VA = 0, 512, 1024, 1536, 2048, 2176
ODD_SLAB = 8192
OD_QC, OD_KC, OD_VC, OD_GC, OD_QD = 0, 1024, 2048, 4096, 6144
OD_KCMP, OD_VCMP, OD_KSLC, OD_VSLC, OD_KWIN, OD_VWIN, OD_GD = 7168, 7296, 7424, 7552, 7680, 7808, 7936

SEL_FAR = 25
SEL_NONE = 26
CMP_WIN = 1024
CMP_WIDTHS = (256, 512, 768, 1024)
REL_BLOCKS = 256


def _cparams(sem):
    return pltpu.CompilerParams(dimension_semantics=sem, vmem_limit_bytes=VMEM_LIMIT)


def _t5_bucket(dist):
    max_exact = REL_BUCKETS // 2
    d = jnp.maximum(dist, 0)
    df = jnp.maximum(d, 1).astype(jnp.float32)
    large = max_exact + (jnp.log(df / max_exact) / math.log(REL_MAX_DIST / max_exact)
                         * (REL_BUCKETS - max_exact)).astype(jnp.int32)
    large = jnp.minimum(large, REL_BUCKETS - 1)
    return jnp.where(d < max_exact, d, large)


def _bias_lookup(table, dist):
    bucket = _t5_bucket(dist)[None]
    tab = table.astype(F32)
    expand = (slice(None),) + (None,) * dist.ndim
    out = jnp.zeros((tab.shape[1],) + dist.shape, F32)
    for b in range(REL_BUCKETS):
        out = jnp.where(bucket == b, tab[b][expand], out)
    return out


def _rms(x, w, eps=1e-6):
    return x * lax.rsqrt(jnp.mean(x * x, axis=-1, keepdims=True) + eps) * w


def _norm_matmul_kernel(h_ref, nw_ref, w_ref, o_ref, *rest, dilations):
    @pl.when(pl.program_id(1) == 0)
    def _():
        rest[-1][...] = _rms(h_ref[...], nw_ref[...]).astype(BF16)

    acc = jnp.dot(rest[-1][...], w_ref[...], preferred_element_type=F32)
    o_ref[...] = acc.astype(o_ref.dtype)
    if dilations:
        acc_ref = rest[-2]
        n_chunks = acc_ref.shape[0]
        for c in range(n_chunks):
            acc_ref[c] = acc[:, c * LANES:(c + 1) * LANES]
        for d, ref in zip(dilations, rest):
            rows = acc_ref.shape[1] // d
            for r in range(d):
                ref[0, r] = jnp.concatenate(
                    [acc_ref[c, pl.ds(r, rows, stride=d), :] for c in range(n_chunks)], axis=1).astype(ref.dtype)


def _norm_matmul(h, nw, w, tm, tn, dilations=(), seq=None):
    m, d_model = h.shape
    n = w.shape[1]
    out_specs = [pl.BlockSpec((tm, tn), lambda i, j: (i, j))]
    out_shape = [jax.ShapeDtypeStruct((m, n), BF16)]
    scratch = []
    if dilations:
        tiles_per_seq = seq // tm
        for d in dilations:
            out_specs.append(pl.BlockSpec((1, d, tm // d, tn),
                                          lambda i, j: (i // tiles_per_seq, 0, i % tiles_per_seq, j)))
            out_shape.append(jax.ShapeDtypeStruct((m // seq, d, seq // d, n), BF16))
        scratch.append(pltpu.VMEM((tn // LANES, tm, LANES), F32))
    scratch.append(pltpu.VMEM((tm, d_model), BF16))
    res = pl.pallas_call(
        functools.partial(_norm_matmul_kernel, dilations=tuple(dilations)),
        grid=(m // tm, n // tn),
        in_specs=[pl.BlockSpec((tm, d_model), lambda i, j: (i, 0)),
                  pl.BlockSpec((1, d_model), lambda i, j: (0, 0)),
                  pl.BlockSpec((d_model, tn), lambda i, j: (0, j))],
        out_specs=out_specs,
        out_shape=out_shape,
        scratch_shapes=scratch,
        compiler_params=_cparams(("parallel", "arbitrary")),
        name="norm_matmul",
    )(h, nw.reshape(1, d_model), w)
    return res if dilations else res[0]


def _proj_residual_kernel(*refs, n_in):
    h_ref, out_ref = refs[0], refs[-1]
    acc = h_ref[...]
    for o_ref, w_ref in zip(refs[1:1 + n_in], refs[1 + n_in:1 + 2 * n_in]):
        acc = acc + jnp.dot(o_ref[...], w_ref[...], preferred_element_type=F32)
    out_ref[...] = acc


def _proj_residual(h, outs, ws, tm):
    m, d = h.shape
    n_in = len(outs)
    in_specs = [pl.BlockSpec((tm, d), lambda i: (i, 0))]
    in_specs += [pl.BlockSpec((tm, o.shape[1]), lambda i: (i, 0)) for o in outs]
    in_specs += [pl.BlockSpec(w.shape, lambda i: (0, 0)) for w in ws]
    return pl.pallas_call(
        functools.partial(_proj_residual_kernel, n_in=n_in),
        grid=(m // tm,),
        in_specs=in_specs,
        out_specs=pl.BlockSpec((tm, d), lambda i: (i, 0)),
        out_shape=jax.ShapeDtypeStruct((m, d), F32),
        compiler_params=_cparams(("parallel",)),
        name="proj_residual",
    )(h, *outs, *ws)


def _ffn_kernel(h_ref, nw_ref, wg_ref, wu_ref, wd_ref, fw_ref, o_ref, *, final):
    h = h_ref[...]
    hn = _rms(h, nw_ref[...]).astype(BF16)
    g = jnp.dot(hn, wg_ref[...], preferred_element_type=F32)
    u = jnp.dot(hn, wu_ref[...], preferred_element_type=F32)
    a = (jax.nn.silu(g) * u).astype(BF16)
    y = h + jnp.dot(a, wd_ref[...], preferred_element_type=F32)
    if final:
        y = _rms(y, fw_ref[...])
    o_ref[...] = y


def _ffn(h, nw, wg, wu, wd, fw, final, tm):
    m, d = h.shape
    ff = wg.shape[1]
    resident = dict(pipeline_mode=pl.Buffered(1))
    return pl.pallas_call(
        functools.partial(_ffn_kernel, final=final),
        grid=(m // tm,),
        in_specs=[pl.BlockSpec((tm, d), lambda i: (i, 0)),
                  pl.BlockSpec((1, d), lambda i: (0, 0)),
                  pl.BlockSpec((d, ff), lambda i: (0, 0), **resident),
                  pl.BlockSpec((d, ff), lambda i: (0, 0), **resident),
                  pl.BlockSpec((ff, d), lambda i: (0, 0), **resident),
                  pl.BlockSpec((1, d), lambda i: (0, 0))],
        out_specs=pl.BlockSpec((tm, d), lambda i: (i, 0)),
        out_shape=jax.ShapeDtypeStruct((m, d), F32),
        compiler_params=_cparams(("parallel",)),
        name="ffn",
    )(h, nw.reshape(1, d), wg, wu, wd, fw.reshape(1, d))


def _band_bias(table, max_dist, dist_scale, nb):
    kw = (nb + 1) * BLK
    rel = jnp.arange(BLK)[:, None] + nb * BLK - jnp.arange(kw)[None, :]
    band = (rel >= 0) & (rel <= max_dist)
    return jnp.where(band[None], _bias_lookup(table, rel * dist_scale), NEG_INF)


def _swap_halves(x):
    return jnp.concatenate([x[:, HEAD_DIM:], x[:, :HEAD_DIM]], axis=1)


def _banded_kernel(*refs, n_heads, n_groups, nb, has_sinks, want_lse):
    pos = 0
    if has_sinks:
        sink_ref = refs[0]
        pos = 1
    q_ref = refs[pos]
    k_refs = refs[pos + 1:pos + 2 + nb]
    v_refs = refs[pos + 2 + nb:pos + 3 + 2 * nb]
    bias_ref = refs[pos + 3 + 2 * nb]
    o_ref = refs[pos + 4 + 2 * nb]
    lse_ref = refs[pos + 5 + 2 * nb] if want_lse else None

    i = pl.program_id(1)
    n_batch = q_ref.shape[0]
    kw = (nb + 1) * BLK
    hpg = n_heads // n_groups
    lane = lax.broadcasted_iota(jnp.int32, (1, LANES), 1)
    lo = lane < HEAD_DIM
    half_mask = (jnp.where(lo, 1.0, 0.0).astype(BF16), jnp.where(lo, 0.0, 1.0).astype(BF16))
    col = lax.broadcasted_iota(jnp.int32, (1, kw), 1)
    edge = jnp.where(col < (nb - i) * BLK, NEG_INF, 0.0).astype(F32)

    operands = []
    for bi in range(n_batch):
        kcat = jnp.concatenate([k_refs[nb - jj][bi] for jj in range(nb + 1)], axis=0)
        vcat = jnp.concatenate([v_refs[nb - jj][bi] for jj in range(nb + 1)], axis=0)
        if hpg == 1:
            for p in range(n_heads // 2):
                operands.append((bi, kcat[:, p * LANES:(p + 1) * LANES], vcat[:, p * LANES:(p + 1) * LANES],
                                 [2 * p, 2 * p + 1]))
        else:
            k_sw = _swap_halves(kcat)
            v_sw = _swap_halves(vcat)
            for g in range(n_groups):
                for par in range(2):
                    heads = [h for h in range(g * hpg, (g + 1) * hpg) if h % 2 == par]
                    operands.append((bi, kcat if g == par else k_sw, vcat if g == par else v_sw, heads))

    scores = []
    for bi, kh, _, heads in operands:
        qz = jnp.concatenate([q_ref[bi, :, (h // 2) * LANES:(h // 2 + 1) * LANES] * half_mask[h % 2]
                              for h in heads], axis=0)
        scores.append(lax.dot_general(qz, kh, (((1,), (1,)), ((), ())), preferred_element_type=F32))
    ms, dens, probs = {}, {}, []
    for (bi, _, _, heads), s_all in zip(operands, scores):
        ps = []
        for r, h in enumerate(heads):
            s = s_all[r * BLK:(r + 1) * BLK] * (HEAD_DIM ** -0.5) + bias_ref[h] + edge
            m = jnp.max(s, axis=-1, keepdims=True)
            if has_sinks:
                m = jnp.maximum(m, sink_ref[h])
            p = jnp.exp(s - m)
            den = jnp.sum(p, axis=-1, keepdims=True)
            if has_sinks:
                den = den + jnp.exp(sink_ref[h] - m)
            ps.append(p.astype(BF16))
            ms[bi, h], dens[bi, h] = m, den
        probs.append(jnp.concatenate(ps, axis=0))
    outs = {}
    for (bi, _, vh, heads), p_all in zip(operands, probs):
        o_all = jnp.dot(p_all, vh, preferred_element_type=F32)
        for r, h in enumerate(heads):
            outs[bi, h] = o_all[r * BLK:(r + 1) * BLK]
    for bi in range(n_batch):
        for pair in range(n_heads // 2):
            h0, h1 = (bi, 2 * pair), (bi, 2 * pair + 1)
            inv = jnp.where(lo, 1.0 / dens[h0], 1.0 / dens[h1])
            o_pair = jnp.where(lo, outs[h0], outs[h1]) * inv
            o_ref[bi, :, pair * LANES:(pair + 1) * LANES] = o_pair.astype(o_ref.dtype)
            if want_lse:
                lse_ref[bi, :, pair * LANES:(pair + 1) * LANES] = jnp.where(
                    lo, ms[h0] + jnp.log(dens[h0]), ms[h1] + jnp.log(dens[h1]))


def _banded_attention(slab, *, q_off, k_off, v_off, n_heads, n_groups,
                      max_dist, table, sinks=None, want_lse=False, out_dtype=BF16):
    b, dil, length, width = slab.shape
    n_blk = length // BLK
    nb = -(-max_dist // BLK)
    kw = (nb + 1) * BLK
    hd = n_heads * HEAD_DIM
    gd = n_groups * HEAD_DIM
    view = slab.reshape(b, dil * length, width)
    bias = _band_bias(table, max_dist, dil, nb)

    def q_map(r, i):
        return (0, r * n_blk + i, q_off // hd)

    def kv_map(off, j):
        return lambda r, i: (0, r * n_blk + jnp.maximum(i - j, 0), off // gd)

    in_specs, args = [], []
    if sinks is not None:
        in_specs.append(pl.BlockSpec(memory_space=pltpu.SMEM))
        args.append(sinks.astype(F32))
    in_specs.append(pl.BlockSpec((b, BLK, hd), q_map))
    args.append(view)
    for off in (k_off, v_off):
        for j in range(nb + 1):
            in_specs.append(pl.BlockSpec((b, BLK, gd), kv_map(off, j)))
            args.append(view)
    in_specs.append(pl.BlockSpec((n_heads, BLK, kw), lambda r, i: (0, 0, 0)))
    args.append(bias)

    out_spec = pl.BlockSpec((b, BLK, hd), lambda r, i: (0, r * n_blk + i, 0))
    out_shape = [jax.ShapeDtypeStruct((b, dil * length, hd), out_dtype)]
    out_specs = [out_spec]
    if want_lse:
        out_shape.append(jax.ShapeDtypeStruct((b, dil * length, hd), F32))
        out_specs.append(out_spec)

    res = pl.pallas_call(
        functools.partial(_banded_kernel, n_heads=n_heads, n_groups=n_groups, nb=nb,
                          has_sinks=sinks is not None, want_lse=want_lse),
        grid=(dil, n_blk),
        in_specs=in_specs,
        out_specs=out_specs,
        out_shape=out_shape,
        compiler_params=_cparams(("parallel", "arbitrary")),
        name="banded_attention",
    )(*args)
    return [r.reshape(b, dil, length, hd) for r in res]


def _dilated_mix_kernel(*refs, n_pat):
    o_refs, l_refs = refs[:n_pat], refs[n_pat:2 * n_pat]
    out_ref = refs[2 * n_pat]
    scratch = refs[2 * n_pat + 1:]

    def natural(ref, buf):
        d = ref.shape[1]
        if d == 1:
            return ref[0, 0]
        rows = ref.shape[2]
        n_chunks = buf.shape[0]
        for r in range(d):
            for c in range(n_chunks):
                buf[c, pl.ds(r, rows, stride=d), :] = ref[0, r, :, c * LANES:(c + 1) * LANES]
        return jnp.concatenate([buf[c] for c in range(n_chunks)], axis=1)

    os_ = [natural(ref, scratch[2 * p]) for p, ref in enumerate(o_refs)]
    ls = [natural(ref, scratch[2 * p + 1]) for p, ref in enumerate(l_refs)]
    m = functools.reduce(jnp.maximum, ls)
    es = [jnp.exp(l - m) for l in ls]
    den = functools.reduce(jnp.add, es)
    acc = functools.reduce(jnp.add, [(e / den) * o for e, o in zip(es, os_)])
    out_ref[...] = acc.astype(out_ref.dtype)


def _dilated_mix(outs, lses, tm):
    b, _, _, c = outs[0].shape
    seq = outs[0].shape[1] * outs[0].shape[2]
    tiles_per_seq = seq // tm

    def spec(a):
        d = a.shape[1]
        return pl.BlockSpec((1, d, tm // d, c), lambda i: (i // tiles_per_seq, 0, i % tiles_per_seq, 0))

    return pl.pallas_call(
        functools.partial(_dilated_mix_kernel, n_pat=len(outs)),
        grid=(b * tiles_per_seq,),
        in_specs=[spec(a) for a in outs] + [spec(a) for a in lses],
        out_specs=pl.BlockSpec((tm, c), lambda i: (i, 0)),
        out_shape=jax.ShapeDtypeStruct((b * seq, c), BF16),
        scratch_shapes=[pltpu.VMEM((c // LANES, tm, LANES), F32) for _ in range(2 * len(outs))],
        compiler_params=_cparams(("parallel",)),
        name="dilated_mix",
    )(*outs, *lses)


def _retention_kernel(q_ref, k_ref, v_ref, g_ref, cos_ref, sin_ref, dmask_ref, qdec_ref, kdec_ref,
                      cdec_ref, gn_ref, o_ref, state_ref):
    hd = pl.program_id(0)

    @pl.when(pl.program_id(1) == 0)
    def _():
        state_ref[...] = jnp.zeros_like(state_ref)

    cos = cos_ref[...]
    sin = sin_ref[...]
    half = C_QK_DIM // 2

    def rot(x):
        x1, x2 = x[:, :half], x[:, half:]
        return jnp.concatenate([x1 * cos - x2 * sin, x1 * sin + x2 * cos], axis=1)

    for bi in range(q_ref.shape[0]):
        q = rot(q_ref[bi].astype(F32))
        k = rot(k_ref[bi].astype(F32)) * (C_QK_DIM ** -0.5)
        v = v_ref[bi]
        qb = q.astype(BF16)
        inner = lax.dot_general(qb, k.astype(BF16), (((1,), (1,)), ((), ())),
                                preferred_element_type=F32) * dmask_ref[0]
        state = state_ref[bi]
        o = jnp.dot(inner.astype(BF16), v, preferred_element_type=F32)
        o = o + jnp.dot(qb, state.astype(BF16), preferred_element_type=F32) * qdec_ref[0]
        kd_t = jnp.transpose(k * kdec_ref[0]).astype(BF16)
        state_ref[bi] = state * cdec_ref[hd] + jnp.dot(kd_t, v, preferred_element_type=F32)

        mu = jnp.mean(o, axis=-1, keepdims=True)
        oc = o - mu
        var = jnp.mean(oc * oc, axis=-1, keepdims=True)
        on = oc * lax.rsqrt(var + 1e-5)
        o_ref[bi] = (on * gn_ref[...] * jax.nn.silu(g_ref[bi].astype(F32))).astype(o_ref.dtype)


def _retention(slab, gn, seq):
    b = slab.shape[0]
    n_chunks = seq // C_CHUNK
    half = C_QK_DIM // 2
    pos = jnp.arange(seq, dtype=F32)
    inv = 1.0 / (10000.0 ** (jnp.arange(0, C_QK_DIM, 2, dtype=F32) / C_QK_DIM))
    ang = pos[:, None] * inv[None, :]
    cos, sin = jnp.cos(ang), jnp.sin(ang)
    log_g = jnp.log(1.0 - 2.0 ** (-5.0 - jnp.arange(C_HEADS, dtype=F32)))
    j = jnp.arange(C_CHUNK, dtype=F32)
    diff = j[:, None] - j[None, :]
    dmask = jnp.where(diff >= 0, jnp.exp(diff[None] * log_g[:, None, None]), 0.0)
    q_dec = jnp.exp((j[None, :] + 1.0) * log_g[:, None])[:, :, None]
    k_dec = jnp.exp((C_CHUNK - 1.0 - j[None, :]) * log_g[:, None])[:, :, None]
    chunk_dec = jnp.exp(C_CHUNK * log_g)

    def col(off, w):
        return lambda h, c: (0, c, off // w + h)

    return pl.pallas_call(
        _retention_kernel,
        grid=(C_HEADS, n_chunks),
        in_specs=[pl.BlockSpec((b, C_CHUNK, C_QK_DIM), col(OD_QC, C_QK_DIM)),
                  pl.BlockSpec((b, C_CHUNK, C_QK_DIM), col(OD_KC, C_QK_DIM)),
                  pl.BlockSpec((b, C_CHUNK, C_V_DIM), col(OD_VC, C_V_DIM)),
                  pl.BlockSpec((b, C_CHUNK, C_V_DIM), col(OD_GC, C_V_DIM)),
                  pl.BlockSpec((C_CHUNK, half), lambda h, c: (c, 0)),
                  pl.BlockSpec((C_CHUNK, half), lambda h, c: (c, 0)),
                  pl.BlockSpec((1, C_CHUNK, C_CHUNK), lambda h, c: (h, 0, 0)),
                  pl.BlockSpec((1, C_CHUNK, 1), lambda h, c: (h, 0, 0)),
                  pl.BlockSpec((1, C_CHUNK, 1), lambda h, c: (h, 0, 0)),
                  pl.BlockSpec(memory_space=pltpu.SMEM),
                  pl.BlockSpec((1, C_V_DIM), lambda h, c: (0, h))],
        out_specs=pl.BlockSpec((b, C_CHUNK, C_V_DIM), lambda h, c: (0, c, h)),
        out_shape=jax.ShapeDtypeStruct((b, seq, C_V), BF16),
        scratch_shapes=[pltpu.VMEM((b, C_QK_DIM, C_V_DIM), F32)],
        compiler_params=_cparams(("parallel", "arbitrary")),
        name="retention",
    )(slab, slab, slab, slab, cos, sin, dmask, q_dec, k_dec, chunk_dec, gn.reshape(1, C_V).astype(F32))


def _compress_kernel(x_ref, pos_ref, w1_ref, w2_ref, o_ref):
    x = (x_ref[0].astype(F32) + pos_ref[0]).astype(BF16)
    hid = jax.nn.gelu(jnp.dot(x, w1_ref[0], preferred_element_type=F32))
    o_ref[0] = jnp.dot(hid.astype(BF16), w2_ref[0], preferred_element_type=F32)


def _compress(flat, pos, w1, w2):
    _, bg, rows, width = flat.shape
    tr = min(rows, 256)
    return pl.pallas_call(
        _compress_kernel,
        grid=(2, bg, rows // tr),
        in_specs=[pl.BlockSpec((None, 1, tr, width), lambda s, i, r: (s, i, r, 0)),
                  pl.BlockSpec((1, 1, width), lambda s, i, r: (s, 0, 0)),
                  pl.BlockSpec((1, width, D_CMP_HIDDEN), lambda s, i, r: (s, 0, 0)),
                  pl.BlockSpec((1, D_CMP_HIDDEN, HEAD_DIM), lambda s, i, r: (s, 0, 0))],
        out_specs=pl.BlockSpec((None, 1, tr, HEAD_DIM), lambda s, i, r: (s, i, r, 0)),
        out_shape=jax.ShapeDtypeStruct((2, bg, rows, HEAD_DIM), F32),
        compiler_params=_cparams(("parallel", "parallel", "parallel")),
        name="nsa_compress",
    )(flat, pos, w1, w2)


def _cmp_bias(table):
    tt = jnp.arange(BLK)[:, None]
    m = jnp.arange(CMP_WIN)[None, :] + (BLK // D_CMP_STRIDE) - CMP_WIN
    dist = tt - D_CMP_STRIDE * m - (D_CMP_LEN - 1)
    return jnp.where((dist >= 0)[None], _bias_lookup(table, dist), NEG_INF)


def _sel_matrix():
    c_rel = np.arange(CMP_WIN)[:, None] + (BLK // D_CMP_STRIDE) - CMP_WIN
    j_rel = np.arange(REL_BLOCKS)[None, :] - (REL_BLOCKS - 2)
    return ((c_rel >= 4 * j_rel - 1) & (c_rel <= 4 * j_rel + 3)).astype(np.float32)


def _cmp_attn_kernel(q_ref, kc_ref, vc_ref, bias_ref, sel_ref, o_ref, idx_ref):
    i = pl.program_id(1)
    hpg = D_HEADS // D_KV_HEADS
    lane = lax.broadcasted_iota(jnp.int32, (1, LANES), 1)
    lo = lane < HEAD_DIM
    half_mask = (jnp.where(lo, 1.0, 0.0).astype(BF16), jnp.where(lo, 0.0, 1.0).astype(BF16))
    row = lax.broadcasted_iota(jnp.int32, (BLK, REL_BLOCKS), 0)
    jj = lax.broadcasted_iota(jnp.int32, (BLK, REL_BLOCKS), 1)
    cur = (REL_BLOCKS - 2) + (row >= D_SEL_LEN).astype(jnp.int32)
    first = (REL_BLOCKS - 2) - 2 * i
    exists = jj >= first
    forced = exists & ((jj == first) | (jj == cur) | (jj == cur - 1))
    valid = exists & (jj <= cur)
    jjf = jj.astype(F32)
    out_lane = lax.broadcasted_iota(jnp.int32, (BLK, LANES), 1)
    out_row = lax.broadcasted_iota(jnp.int32, (BLK, LANES), 0)
    cur_abs = 2 * i + (out_row >= D_SEL_LEN).astype(jnp.int32)
    tt_in_blk = out_row % D_SEL_LEN

    for width in CMP_WIDTHS:
        lo_w = width - CMP_WIDTHS[0] if width > CMP_WIDTHS[0] else -1

        @pl.when((8 * i + 8 <= width) & (8 * i + 8 > lo_w))
        def _(width=width):
            _cmp_attn_body(q_ref, kc_ref, vc_ref, bias_ref, sel_ref, o_ref, idx_ref, i, width, lo, half_mask,
                           forced, valid, jjf, first, out_lane, cur_abs, tt_in_blk)


def _cmp_attn_body(q_ref, kc_ref, vc_ref, bias_ref, sel_ref, o_ref, idx_ref, i, width, lo, half_mask,
                   forced, valid, jjf, first, out_lane, cur_abs, tt_in_blk):
    hpg = D_HEADS // D_KV_HEADS
    off = CMP_WIN - width
    start = pl.multiple_of(8 * i + 8 + off, 8)
    ucol = lax.broadcasted_iota(jnp.int32, (1, width), 1) + off
    edge = jnp.where(ucol < CMP_WIN - 8 - 8 * i, NEG_INF, 0.0).astype(F32)

    for g in range(D_KV_HEADS):
        kwin = kc_ref[0, g, pl.ds(start, width), :].astype(BF16)
        vwin = vc_ref[0, g, pl.ds(start, width), :].astype(BF16)
        qs = []
        for hh in range(hpg):
            h = g * hpg + hh
            qs.append(q_ref[0, :, (h // 2) * LANES:(h // 2 + 1) * LANES] * half_mask[h % 2])
        s_all = lax.dot_general(jnp.concatenate(qs, axis=0), kwin, (((1,), (1,)), ((), ())),
                                preferred_element_type=F32)
        imp = jnp.zeros((BLK, width), F32)
        pns = []
        for hh in range(hpg):
            h = g * hpg + hh
            s = s_all[hh * BLK:(hh + 1) * BLK] * (HEAD_DIM ** -0.5) + bias_ref[h, :, off:] + edge
            ok = s > 0.5 * NEG_INF
            m = jnp.max(s, axis=-1, keepdims=True)
            p = jnp.where(ok, jnp.exp(s - m), 0.0)
            den = jnp.maximum(jnp.sum(p, axis=-1, keepdims=True), 1e-30)
            pn = p * (1.0 / den)
            imp = imp + pn
            pns.append(pn.astype(BF16))
        o_all = jnp.dot(jnp.concatenate(pns, axis=0), vwin, preferred_element_type=F32)
        for pr in range(hpg // 2):
            pair = (g * hpg) // 2 + pr
            o_ref[0, :, pair * LANES:(pair + 1) * LANES] = jnp.where(
                lo, o_all[2 * pr * BLK:(2 * pr + 1) * BLK], o_all[(2 * pr + 1) * BLK:(2 * pr + 2) * BLK]
            ).astype(o_ref.dtype)

        sel = sel_ref[off:, :]
        hi = imp.astype(BF16)
        r1 = imp - hi.astype(F32)
        mid = r1.astype(BF16)
        low = (r1 - mid.astype(F32)).astype(BF16)
        imp_sel = (jnp.dot(hi, sel, preferred_element_type=F32)
                   + jnp.dot(mid, sel, preferred_element_type=F32)
                   + jnp.dot(low, sel, preferred_element_type=F32))
        score = jnp.where(forced, 1e9, jnp.where(valid, imp_sel, -1e9))
        picked = jnp.zeros((BLK, LANES), jnp.int32)
        for r in range(D_SEL_COUNT):
            m = jnp.max(score, axis=-1, keepdims=True)
            am = jnp.min(jnp.where(score == m, jjf, float(REL_BLOCKS)), axis=-1, keepdims=True)
            none = m < -5e8
            blk = am.astype(jnp.int32) - first
            key_blk = jnp.where(none, 0, blk)
            bias_row = jnp.where(none, SEL_NONE, jnp.minimum(cur_abs - blk, SEL_FAR)) * D_SEL_LEN + tt_in_blk
            picked = jnp.where(out_lane == r, key_blk, picked)
            picked = jnp.where(out_lane == D_SEL_COUNT + r, bias_row, picked)
            score = jnp.where(jjf == am, -jnp.inf, score)
        idx_ref[0, g] = picked


def _cmp_attention(slab, kc_pad, vc_pad, table, seq):
    b = slab.shape[0]
    n_blk = seq // BLK
    rows = kc_pad.shape[2]
    bias = _cmp_bias(table)
    sel = jnp.asarray(_sel_matrix(), BF16)
    return pl.pallas_call(
        _cmp_attn_kernel,
        grid=(b, n_blk),
        in_specs=[pl.BlockSpec((1, BLK, D_Q), lambda bi, i: (bi, i, OD_QD // D_Q)),
                  pl.BlockSpec((1, D_KV_HEADS, rows, LANES), lambda bi, i: (bi, 0, 0, 0)),
                  pl.BlockSpec((1, D_KV_HEADS, rows, LANES), lambda bi, i: (bi, 0, 0, 0)),
                  pl.BlockSpec((D_HEADS, BLK, CMP_WIN), lambda bi, i: (0, 0, 0)),
                  pl.BlockSpec((CMP_WIN, REL_BLOCKS), lambda bi, i: (0, 0))],
        out_specs=[pl.BlockSpec((1, BLK, D_Q), lambda bi, i: (bi, i, 0)),
                   pl.BlockSpec((1, D_KV_HEADS, BLK, LANES), lambda bi, i: (bi, 0, i, 0))],
        out_shape=[jax.ShapeDtypeStruct((b, seq, D_Q), BF16),
                   jax.ShapeDtypeStruct((b, D_KV_HEADS, seq, LANES), jnp.int32)],
        compiler_params=_cparams(("parallel", "arbitrary")),
        name="nsa_cmp_attention",
    )(slab, kc_pad, vc_pad, bias, sel)


def _sel_bias(table):
    hpg = D_HEADS // D_KV_HEADS
    delta = jnp.arange(SEL_FAR)[:, None, None]
    tt = jnp.arange(D_SEL_LEN)[None, :, None]
    l = jnp.arange(D_SEL_LEN)[None, None, :]
    dist = D_SEL_LEN * delta + tt - l
    near = jnp.where((dist >= 0)[None], _bias_lookup(table, dist), NEG_INF)
    far = jnp.broadcast_to(table.astype(F32)[REL_BUCKETS - 1][:, None, None, None],
                           (D_HEADS, 1, D_SEL_LEN, D_SEL_LEN))
    none = jnp.full((D_HEADS, 1, D_SEL_LEN, D_SEL_LEN), NEG_INF, F32)
    rows = jnp.concatenate([near, far, none], axis=1)
    half = D_SEL_LEN // 2
    rows = rows.reshape(D_KV_HEADS, hpg, SEL_NONE + 1, D_SEL_LEN, half, 2)
    rows = jnp.transpose(rows, (0, 2, 3, 5, 1, 4))
    rows = rows.reshape(D_KV_HEADS, (SEL_NONE + 1) * D_SEL_LEN, 2 * hpg, half)
    return jnp.tile(rows, (1, 1, 1, LANES // half))


SEL_IDX = 2 * D_SEL_COUNT
SEL_UNROLL = 16


def _sel_attn_kernel(idx_hbm, q_ref, kv_ref, bias_ref, o_ref, idx_smem, sem):
    n_g, n_i = pl.num_programs(1), pl.num_programs(2)
    step = (pl.program_id(0) * n_g + pl.program_id(1)) * n_i + pl.program_id(2)
    total = pl.num_programs(0) * n_g * n_i
    slot = step % 2

    tile_words = BLK * SEL_IDX

    def idx_copy(s, sl):
        dst = idx_smem.at[pl.ds(pl.multiple_of(sl * tile_words, tile_words), tile_words)]
        return pltpu.make_async_copy(idx_hbm.at[s], dst, sem.at[sl])

    @pl.when(step == 0)
    def _():
        idx_copy(0, 0).start()

    @pl.when(step + 1 < total)
    def _():
        idx_copy(step + 1, 1 - slot).start()

    idx_copy(step, slot).wait()

    lane = lax.broadcasted_iota(jnp.int32, (1, LANES), 1)
    lo = lane < HEAD_DIM
    hpg = D_HEADS // D_KV_HEADS
    quarter = D_SEL_LEN // 2
    n_quads = D_SEL_COUNT // 4

    def scores(tl):
        picks = idx_smem.at[pl.ds(slot * tile_words + tl * SEL_IDX, SEL_IDX)]
        qq = q_ref[0, 0, tl]
        qbd = jnp.concatenate([jnp.where(lo, qq, 0.0), jnp.where(lo, 0.0, qq)], axis=0).astype(BF16)
        ks, vs, bs = [], [], []
        for n in range(D_SEL_COUNT):
            kv = kv_ref[0, 0, picks[n]]
            ks.append(kv[:quarter])
            vs.append(kv[quarter:])
            bs.append(bias_ref[0, picks[D_SEL_COUNT + n]])
        s = lax.dot_general(qbd, jnp.concatenate(ks, axis=0), (((1,), (1,)), ((), ())),
                            preferred_element_type=F32)
        bias = jnp.concatenate(
            [jnp.where(lane < quarter, bs[4 * c],
                       jnp.where(lane < 2 * quarter, bs[4 * c + 1],
                                 jnp.where(lane < 3 * quarter, bs[4 * c + 2], bs[4 * c + 3])))
             for c in range(n_quads)], axis=1)
        return s * (HEAD_DIM ** -0.5) + bias, jnp.concatenate(vs, axis=0)

    def probs(s):
        m16 = jnp.max(s, axis=-1, keepdims=True)
        m = jnp.maximum(m16[:hpg], m16[hpg:])
        p = jnp.exp(s - jnp.concatenate([m, m], axis=0))
        d16 = jnp.sum(p, axis=-1, keepdims=True)
        return p.astype(BF16), d16[:hpg] + d16[hpg:]

    def body(it, carry):
        t0 = it * SEL_UNROLL
        sv = [scores(t0 + u) for u in range(SEL_UNROLL)]
        pd = [probs(s) for s, _ in sv]
        outs = []
        for (p, den), (_, vall) in zip(pd, sv):
            o2 = jnp.dot(p, vall, preferred_element_type=F32)
            outs.append(jnp.where(lo, o2[:hpg], o2[hpg:]) * (1.0 / den))
        for u in range(SEL_UNROLL):
            o_ref[0, 0, t0 + u] = outs[u]
        return carry

    lax.fori_loop(0, BLK // SEL_UNROLL, body, 0)


def _sel_attention(idx, q_sel, kv_sel, table, seq):
    b = q_sel.shape[0]
    hpg = D_HEADS // D_KV_HEADS
    n_blk = seq // BLK
    n_sb = seq // D_SEL_LEN
    bias = _sel_bias(table)
    kv_spec = pl.BlockSpec((1, 1, n_sb, D_SEL_LEN, LANES), lambda bi, g, i: (bi, g, 0, 0, 0))
    return pl.pallas_call(
        _sel_attn_kernel,
        grid=(b, D_KV_HEADS, n_blk),
        in_specs=[pl.BlockSpec(memory_space=pl.ANY),
                  pl.BlockSpec((1, 1, BLK, hpg, LANES), lambda bi, g, i: (bi, g, i, 0, 0)),
                  kv_spec,
                  pl.BlockSpec((1, (SEL_NONE + 1) * D_SEL_LEN, 2 * hpg, LANES), lambda bi, g, i: (g, 0, 0, 0))],
        out_specs=pl.BlockSpec((1, 1, BLK, hpg, LANES), lambda bi, g, i: (bi, g, i, 0, 0)),
        out_shape=jax.ShapeDtypeStruct((b, D_KV_HEADS, seq, hpg, LANES), F32),
        scratch_shapes=[pltpu.SMEM((2 * BLK * SEL_IDX,), jnp.int32), pltpu.SemaphoreType.DMA((2,))],
        compiler_params=_cparams(("arbitrary", "arbitrary", "arbitrary")),
        name="nsa_sel_attention",
    )(idx, q_sel, kv_sel, bias)


def _gate_expand():
    e = np.zeros((3, LANES, D_Q), np.float32)
    for h in range(D_HEADS):
        for c in range(3):
            e[c, 3 * h + c, h * HEAD_DIM:(h + 1) * HEAD_DIM] = 1.0
    return e


def _nsa_gate_kernel(gd_ref, e_ref, oc_ref, os_ref, ow_ref, out_ref):
    sg = jax.nn.sigmoid(gd_ref[...].astype(F32))
    hi = sg.astype(BF16)
    low = (sg - hi.astype(F32)).astype(BF16)
    acc = None
    for c, o_ref in enumerate((oc_ref, os_ref, ow_ref)):
        gate = (jnp.dot(hi, e_ref[c], preferred_element_type=F32)
                + jnp.dot(low, e_ref[c], preferred_element_type=F32))
        term = gate * o_ref[...].astype(F32)
        acc = term if acc is None else acc + term
    out_ref[...] = acc.astype(out_ref.dtype)


def _nsa_gate(slab2d, o_c, o_s, o_w, tm):
    m = slab2d.shape[0]
    e = jnp.asarray(_gate_expand(), BF16)
    spec = pl.BlockSpec((tm, D_Q), lambda i: (i, 0))
    return pl.pallas_call(
        _nsa_gate_kernel,
        grid=(m // tm,),
        in_specs=[pl.BlockSpec((tm, LANES), lambda i: (i, OD_GD // LANES)),
                  pl.BlockSpec((3, LANES, D_Q), lambda i: (0, 0, 0)),
                  spec, spec, spec],
        out_specs=spec,
        out_shape=jax.ShapeDtypeStruct((m, D_Q), BF16),
        compiler_params=_cparams(("parallel",)),
        name="nsa_gate",
    )(slab2d, e, o_c, o_s, o_w)


def _even_mixer(h, nw, w_in, sinks, w_out, rel_table, b, seq):
    qa, ka, va, qb, kb, vb = jnp.split(w_in, [int(c) for c in np.cumsum([A_Q, A_KV, A_KV, B_W, B_W])], axis=1)
    pad = jnp.zeros((D_MODEL, EVEN_SLAB - w_in.shape[1]), w_in.dtype)
    w_slab = jnp.concatenate([qa, qb, kb, vb, ka, va, pad], axis=1).astype(BF16)
    dilations = [dil for _, dil in B_PATTERNS if dil > 1]
    slabs = _norm_matmul(h, nw, w_slab, 1024, EVEN_SLAB // 2, dilations, seq)
    by_dil = {1: slabs[0].reshape(b, 1, seq, EVEN_SLAB)}
    by_dil.update(zip(dilations, slabs[1:]))

    (oa,) = _banded_attention(by_dil[1], q_off=EV_QA, k_off=EV_KA, v_off=EV_VA,
                              n_heads=A_HEADS, n_groups=A_KV_HEADS, max_dist=A_WINDOW - 1,
                              table=rel_table[:, :A_HEADS], sinks=sinks)
    outs, lses = [], []
    for window, dil in B_PATTERNS:
        o, lse = _banded_attention(by_dil[dil], q_off=EV_QB, k_off=EV_KB, v_off=EV_VB,
                                   n_heads=B_HEADS, n_groups=B_HEADS, max_dist=window // dil,
                                   table=rel_table[:, A_HEADS:A_HEADS + B_HEADS], want_lse=True, out_dtype=F32)
        outs.append(o)
        lses.append(lse)
    ob = _dilated_mix(outs, lses, 1024)
    w_out = w_out.astype(BF16)
    return _proj_residual(h, [oa.reshape(b * seq, A_Q), ob], [w_out[:A_Q], w_out[A_Q:]], 512)


def _nsa(slab, pos_k, pos_v, k_w1, k_w2, v_w1, v_w2, rel_table, b, seq):
    g_kv = D_KV_HEADS
    hpg = D_HEADS // g_kv
    slab2d = slab.reshape(b * seq, ODD_SLAB)
    n_rows = seq // D_CMP_STRIDE

    def rows16(off):
        a = slab[:, :, off:off + D_KV].reshape(b, n_rows, D_CMP_STRIDE, g_kv, HEAD_DIM)
        a = jnp.transpose(a, (0, 3, 1, 2, 4)).reshape(b * g_kv, n_rows, D_CMP_STRIDE * HEAD_DIM)
        nxt = jnp.concatenate([a[:, 1:], jnp.zeros_like(a[:, :1])], axis=1)
        return jnp.concatenate([a, nxt], axis=-1)

    flat = jnp.stack([rows16(OD_KCMP), rows16(OD_VCMP)])
    pos = jnp.stack([pos_k.reshape(1, -1), pos_v.reshape(1, -1)]).astype(F32)
    w1 = jnp.stack([k_w1, v_w1]).astype(BF16)
    w2 = jnp.stack([k_w2, v_w2]).astype(BF16)
    cmp = _compress(flat, pos, w1, w2).reshape(2, b, g_kv, n_rows, HEAD_DIM)
    cmp = jnp.pad(cmp, ((0, 0), (0, 0), (0, 0), (CMP_WIN, 0), (0, 0)))
    cmp = jnp.concatenate([cmp, cmp], axis=-1)
    o_c, idx = _cmp_attention(slab, cmp[0], cmp[1], rel_table, seq)

    idx = idx[..., :SEL_IDX].reshape(b * g_kv * (seq // BLK), BLK * SEL_IDX)

    def per_group(off):
        return slab[:, :, off:off + D_KV].reshape(b, seq, g_kv, HEAD_DIM)

    def two_per_row(off):
        a = jnp.transpose(per_group(off), (0, 2, 1, 3))
        return a.reshape(b, g_kv, seq // D_SEL_LEN, D_SEL_LEN // 2, LANES)

    q_sel = slab[:, :, OD_QD:OD_QD + D_Q].reshape(b, seq, g_kv, hpg, HEAD_DIM)
    q_sel = jnp.transpose(q_sel, (0, 2, 1, 3, 4)).astype(F32)
    q_sel = jnp.concatenate([q_sel, q_sel], axis=-1)
    kv_sel = jnp.concatenate([two_per_row(OD_KSLC), two_per_row(OD_VSLC)], axis=3)
    o_s = _sel_attention(idx, q_sel, kv_sel, rel_table, seq)
    o_s = o_s[..., :HEAD_DIM] + o_s[..., HEAD_DIM:]
    o_s = jnp.transpose(o_s, (0, 2, 1, 3, 4)).reshape(b * seq, D_Q).astype(BF16)

    (o_w,) = _banded_attention(slab.reshape(b, 1, seq, ODD_SLAB), q_off=OD_QD, k_off=OD_KWIN, v_off=OD_VWIN,
                               n_heads=D_HEADS, n_groups=D_KV_HEADS, max_dist=D_WINDOW - 1, table=rel_table)
    return _nsa_gate(slab2d, o_c.reshape(b * seq, D_Q), o_s, o_w.reshape(b * seq, D_Q), 1024)


def _odd_mixer(h, nw, w_in, ret_gn, pos_k, pos_v, k_w1, k_w2, v_w1, v_w2, w_out, rel_table, b, seq):
    pad = jnp.zeros((D_MODEL, ODD_SLAB - ODD_IN), w_in.dtype)
    w_slab = jnp.concatenate([w_in, pad], axis=1).astype(BF16)
    slab = _norm_matmul(h, nw, w_slab, 1024, ODD_SLAB // 4).reshape(b, seq, ODD_SLAB)
    oc = _retention(slab, ret_gn, seq)
    od = _nsa(slab, pos_k, pos_v, k_w1, k_w2, v_w1, v_w2, rel_table, b, seq)
    w_out = w_out.astype(BF16)
    return _proj_residual(h, [oc.reshape(b * seq, C_V), od], [w_out[:C_V], w_out[C_V:]], 512)


def kernel(x, rel_table, norm_mix, norm_ffn, norm_final, even_w_in, even_sinks, even_w_out, odd_w_in, odd_ret_gn, odd_cmp_pos_k, odd_cmp_pos_v, odd_cmp_k_w1, odd_cmp_k_w2, odd_cmp_v_w1, odd_cmp_v_w2, odd_w_out, ffn_w_gate, ffn_w_up, ffn_w_down):
    b, seq, d = x.shape
    h = x.reshape(b * seq, d)
    for layer in range(DEPTH):
        li = layer // 2
        if layer % 2 == 0:
            h = _even_mixer(h, norm_mix[layer], even_w_in[li], even_sinks[li], even_w_out[li], rel_table, b, seq)
        else:
            h = _odd_mixer(h, norm_mix[layer], odd_w_in[li], odd_ret_gn[li], odd_cmp_pos_k[li],
                           odd_cmp_pos_v[li], odd_cmp_k_w1[li], odd_cmp_k_w2[li], odd_cmp_v_w1[li],
                           odd_cmp_v_w2[li], odd_w_out[li], rel_table, b, seq)
        h = _ffn(h, norm_ffn[layer], ffn_w_gate[layer].astype(BF16), ffn_w_up[layer].astype(BF16),
                 ffn_w_down[layer].astype(BF16), norm_final, layer == DEPTH - 1, 512)
    return h.reshape(b, seq, d)
```

```python
import functools
import math

import numpy as np
import jax
import jax.numpy as jnp
from jax import lax
from jax.experimental import pallas as pl
from jax.experimental.pallas import tpu as pltpu

F32 = jnp.float32
BF16 = jnp.bfloat16

D_MODEL = 1024
DEPTH = 4
HEAD_DIM = 64
BLK = 128
NEG_INF = -1e30
REL_BUCKETS = 32
REL_MAX_DIST = 2048
A_HEADS = 8
A_KV_HEADS = 2
A_WINDOW = 128
B_HEADS = 8
B_PATTERNS = ((128, 1), (512, 4), (2048, 16))
C_HEADS = 4
C_QK_DIM = 256
C_V_DIM = 512
C_CHUNK = 128
D_HEADS = 16
D_KV_HEADS = 2
D_CMP_LEN = 32
D_CMP_STRIDE = 16
D_CMP_HIDDEN = 128
D_SEL_LEN = 64
D_SEL_COUNT = 16
D_WINDOW = 512
D_FF = 2816

A_Q = A_HEADS * HEAD_DIM
A_KV = A_KV_HEADS * HEAD_DIM
B_W = B_HEADS * HEAD_DIM
C_QK = C_HEADS * C_QK_DIM
C_V = C_HEADS * C_V_DIM
D_Q = D_HEADS * HEAD_DIM
D_KV = D_KV_HEADS * HEAD_DIM
ODD_IN = 2 * C_QK + 2 * C_V + D_Q + 6 * D_KV + 3 * D_HEADS

SCORE_SCALE = HEAD_DIM ** -0.5
LANES = 128
VMEM_LIMIT = 56 * 1024 * 1024

EVEN_SLAB = 2560
EV_QA, EV_QB, EV_KB, EV_VB, EV_KA, EV_VA = 0, 512, 1024, 1536, 2048, 2176
ODD_SLAB = 8192
OD_QC, OD_KC, OD_VC, OD_GC, OD_QD = 0, 1024, 2048, 4096, 6144
OD_KCMP, OD_VCMP, OD_KSLC, OD_VSLC, OD_KWIN, OD_VWIN, OD_GD = 7168, 7296, 7424, 7552, 7680, 7808, 7936

SEL_FAR = 25
SEL_NONE = 26
CMP_WIN = 1024
CMP_WIDTHS = (256, 512, 768, 1024)
REL_BLOCKS = 256


def _cparams(sem):
    return pltpu.CompilerParams(dimension_semantics=sem, vmem_limit_bytes=VMEM_LIMIT)


def _t5_bucket(dist):
    max_exact = REL_BUCKETS // 2
    d = jnp.maximum(dist, 0)
    df = jnp.maximum(d, 1).astype(jnp.float32)
    large = max_exact + (jnp.log(df / max_exact) / math.log(REL_MAX_DIST / max_exact)
                         * (REL_BUCKETS - max_exact)).astype(jnp.int32)
    large = jnp.minimum(large, REL_BUCKETS - 1)
    return jnp.where(d < max_exact, d, large)


def _bias_lookup(table, dist):
    bucket = _t5_bucket(dist)[None]
    tab = table.astype(F32)
    expand = (slice(None),) + (None,) * dist.ndim
    out = jnp.zeros((tab.shape[1],) + dist.shape, F32)
    for b in range(REL_BUCKETS):
        out = jnp.where(bucket == b, tab[b][expand], out)
    return out


def _rms(x, w, eps=1e-6):
    return x * lax.rsqrt(jnp.mean(x * x, axis=-1, keepdims=True) + eps) * w


def _norm_matmul_kernel(h_ref, nw_ref, w_ref, o_ref, *rest, dilations):
    @pl.when(pl.program_id(1) == 0)
    def _():
        rest[-1][...] = _rms(h_ref[...], nw_ref[...]).astype(BF16)

    acc = jnp.dot(rest[-1][...], w_ref[...], preferred_element_type=F32)
    o_ref[...] = acc.astype(o_ref.dtype)
    if dilations:
        acc_ref = rest[-2]
        n_chunks = acc_ref.shape[0]
        for c in range(n_chunks):
            acc_ref[c] = acc[:, c * LANES:(c + 1) * LANES]
        for d, ref in zip(dilations, rest):
            rows = acc_ref.shape[1] // d
            for r in range(d):
                ref[0, r] = jnp.concatenate(
                    [acc_ref[c, pl.ds(r, rows, stride=d), :] for c in range(n_chunks)], axis=1).astype(ref.dtype)


def _norm_matmul(h, nw, w, tm, tn, dilations=(), seq=None):
    m, d_model = h.shape
    n = w.shape[1]
    out_specs = [pl.BlockSpec((tm, tn), lambda i, j: (i, j))]
    out_shape = [jax.ShapeDtypeStruct((m, n), BF16)]
    scratch = []
    if dilations:
        tiles_per_seq = seq // tm
        for d in dilations:
            out_specs.append(pl.BlockSpec((1, d, tm // d, tn),
                                          lambda i, j: (i // tiles_per_seq, 0, i % tiles_per_seq, j)))
            out_shape.append(jax.ShapeDtypeStruct((m // seq, d, seq // d, n), BF16))
        scratch.append(pltpu.VMEM((tn // LANES, tm, LANES), F32))
    scratch.append(pltpu.VMEM((tm, d_model), BF16))
    res = pl.pallas_call(
        functools.partial(_norm_matmul_kernel, dilations=tuple(dilations)),
        grid=(m // tm, n // tn),
        in_specs=[pl.BlockSpec((tm, d_model), lambda i, j: (i, 0)),
                  pl.BlockSpec((1, d_model), lambda i, j: (0, 0)),
                  pl.BlockSpec((d_model, tn), lambda i, j: (0, j))],
        out_specs=out_specs,
        out_shape=out_shape,
        scratch_shapes=scratch,
        compiler_params=_cparams(("parallel", "arbitrary")),
        name="norm_matmul",
    )(h, nw.reshape(1, d_model), w)
    return res if dilations else res[0]


def _proj_residual_kernel(*refs, n_in):
    h_ref, out_ref = refs[0], refs[-1]
    acc = h_ref[...]
    for o_ref, w_ref in zip(refs[1:1 + n_in], refs[1 + n_in:1 + 2 * n_in]):
        acc = acc + jnp.dot(o_ref[...], w_ref[...], preferred_element_type=F32)
    out_ref[...] = acc


def _proj_residual(h, outs, ws, tm):
    m, d = h.shape
    n_in = len(outs)
    in_specs = [pl.BlockSpec((tm, d), lambda i: (i, 0))]
    in_specs += [pl.BlockSpec((tm, o.shape[1]), lambda i: (i, 0)) for o in outs]
    in_specs += [pl.BlockSpec(w.shape, lambda i: (0, 0)) for w in ws]
    return pl.pallas_call(
        functools.partial(_proj_residual_kernel, n_in=n_in),
        grid=(m // tm,),
        in_specs=in_specs,
        out_specs=pl.BlockSpec((tm, d), lambda i: (i, 0)),
        out_shape=jax.ShapeDtypeStruct((m, d), F32),
        compiler_params=_cparams(("parallel",)),
        name="proj_residual",
    )(h, *outs, *ws)


def _ffn_kernel(h_ref, nw_ref, wg_ref, wu_ref, wd_ref, fw_ref, o_ref, *, final):
    h = h_ref[...]
    hn = _rms(h, nw_ref[...]).astype(BF16)
    g = jnp.dot(hn, wg_ref[...], preferred_element_type=F32)
    u = jnp.dot(hn, wu_ref[...], preferred_element_type=F32)
    a = (jax.nn.silu(g) * u).astype(BF16)
    y = h + jnp.dot(a, wd_ref[...], preferred_element_type=F32)
    if final:
        y = _rms(y, fw_ref[...])
    o_ref[...] = y


def _ffn(h, nw, wg, wu, wd, fw, final, tm):
    m, d = h.shape
    ff = wg.shape[1]
    resident = dict(pipeline_mode=pl.Buffered(1))
    return pl.pallas_call(
        functools.partial(_ffn_kernel, final=final),
        grid=(m // tm,),
        in_specs=[pl.BlockSpec((tm, d), lambda i: (i, 0)),
                  pl.BlockSpec((1, d), lambda i: (0, 0)),
                  pl.BlockSpec((d, ff), lambda i: (0, 0), **resident),
                  pl.BlockSpec((d, ff), lambda i: (0, 0), **resident),
                  pl.BlockSpec((ff, d), lambda i: (0, 0), **resident),
                  pl.BlockSpec((1, d), lambda i: (0, 0))],
        out_specs=pl.BlockSpec((tm, d), lambda i: (i, 0)),
        out_shape=jax.ShapeDtypeStruct((m, d), F32),
        compiler_params=_cparams(("parallel",)),
        name="ffn",
    )(h, nw.reshape(1, d), wg, wu, wd, fw.reshape(1, d))


def _band_bias(table, max_dist, dist_scale, nb):
    kw = (nb + 1) * BLK
    rel = jnp.arange(BLK)[:, None] + nb * BLK - jnp.arange(kw)[None, :]
    band = (rel >= 0) & (rel <= max_dist)
    return jnp.where(band[None], _bias_lookup(table, rel * dist_scale), NEG_INF)


def _swap_halves(x):
    return jnp.concatenate([x[:, HEAD_DIM:], x[:, :HEAD_DIM]], axis=1)


def _banded_kernel(*refs, n_heads, n_groups, nb, has_sinks, want_lse):
    pos = 0
    if has_sinks:
        sink_ref = refs[0]
        pos = 1
    q_ref = refs[pos]
    k_refs = refs[pos + 1:pos + 2 + nb]
    v_refs = refs[pos + 2 + nb:pos + 3 + 2 * nb]
    bias_ref = refs[pos + 3 + 2 * nb]
    o_ref = refs[pos + 4 + 2 * nb]
    lse_ref = refs[pos + 5 + 2 * nb] if want_lse else None

    i = pl.program_id(1)
    n_batch = q_ref.shape[0]
    kw = (nb + 1) * BLK
    hpg = n_heads // n_groups
    lane = lax.broadcasted_iota(jnp.int32, (1, LANES), 1)
    lo = lane < HEAD_DIM
    half_mask = (jnp.where(lo, SCORE_SCALE, 0.0).astype(BF16), jnp.where(lo, 0.0, SCORE_SCALE).astype(BF16))
    col = lax.broadcasted_iota(jnp.int32, (1, kw), 1)
    edge = jnp.where(col < (nb - i) * BLK, NEG_INF, 0.0).astype(F32)

    operands = []
    for bi in range(n_batch):
        kcat = jnp.concatenate([k_refs[nb - jj][bi] for jj in range(nb + 1)], axis=0)
        vcat = jnp.concatenate([v_refs[nb - jj][bi] for jj in range(nb + 1)], axis=0)
        if hpg == 1:
            for p in range(n_heads // 2):
                operands.append((bi, kcat[:, p * LANES:(p + 1) * LANES], vcat[:, p * LANES:(p + 1) * LANES],
                                 [2 * p, 2 * p + 1]))
        else:
            k_sw = _swap_halves(kcat)
            v_sw = _swap_halves(vcat)
            for g in range(n_groups):
                for par in range(2):
                    heads = [h for h in range(g * hpg, (g + 1) * hpg) if h % 2 == par]
                    operands.append((bi, kcat if g == par else k_sw, vcat if g == par else v_sw, heads))

    scores = []
    for bi, kh, _, heads in operands:
        qz = jnp.concatenate([q_ref[bi, :, (h // 2) * LANES:(h // 2 + 1) * LANES] * half_mask[h % 2]
                              for h in heads], axis=0)
        scores.append(lax.dot_general(qz, kh, (((1,), (1,)), ((), ())), preferred_element_type=F32))
    ms, dens, probs = {}, {}, []
    for (bi, _, _, heads), s_all in zip(operands, scores):
        ps = []
        for r, h in enumerate(heads):
            s = s_all[r * BLK:(r + 1) * BLK] + bias_ref[h] + edge
            m = jnp.max(s, axis=-1, keepdims=True)
            if has_sinks:
                m = jnp.maximum(m, sink_ref[h])
            p = jnp.exp(s - m)
            den = jnp.sum(p, axis=-1, keepdims=True)
            if has_sinks:
                den = den + jnp.exp(sink_ref[h] - m)
            ps.append(p.astype(BF16))
            ms[bi, h], dens[bi, h] = m, den
        probs.append(jnp.concatenate(ps, axis=0))
    outs = {}
    for (bi, _, vh, heads), p_all in zip(operands, probs):
        o_all = jnp.dot(p_all, vh, preferred_element_type=F32)
        for r, h in enumerate(heads):
            outs[bi, h] = o_all[r * BLK:(r + 1) * BLK]
    for bi in range(n_batch):
        for pair in range(n_heads // 2):
            h0, h1 = (bi, 2 * pair), (bi, 2 * pair + 1)
            inv = jnp.where(lo, 1.0 / dens[h0], 1.0 / dens[h1])
            o_pair = jnp.where(lo, outs[h0], outs[h1]) * inv
            o_ref[bi, :, pair * LANES:(pair + 1) * LANES] = o_pair.astype(o_ref.dtype)
            if want_lse:
                lse_ref[bi, :, pair * LANES:(pair + 1) * LANES] = jnp.where(
                    lo, ms[h0] + jnp.log(dens[h0]), ms[h1] + jnp.log(dens[h1]))


def _banded_attention(slab, *, q_off, k_off, v_off, n_heads, n_groups,
                      max_dist, table, sinks=None, want_lse=False, out_dtype=BF16):
    b, dil, length, width = slab.shape
    n_blk = length // BLK
    nb = -(-max_dist // BLK)
    kw = (nb + 1) * BLK
    hd = n_heads * HEAD_DIM
    gd = n_groups * HEAD_DIM
    view = slab.reshape(b, dil * length, width)
    bias = _band_bias(table, max_dist, dil, nb)

    def q_map(r, i):
        return (0, r * n_blk + i, q_off // hd)

    def kv_map(off, j):
        return lambda r, i: (0, r * n_blk + jnp.maximum(i - j, 0), off // gd)

    in_specs, args = [], []
    if sinks is not None:
        in_specs.append(pl.BlockSpec(memory_space=pltpu.SMEM))
        args.append(sinks.astype(F32))
    in_specs.append(pl.BlockSpec((b, BLK, hd), q_map))
    args.append(view)
    for off in (k_off, v_off):
        for j in range(nb + 1):
            in_specs.append(pl.BlockSpec((b, BLK, gd), kv_map(off, j)))
            args.append(view)
    in_specs.append(pl.BlockSpec((n_heads, BLK, kw), lambda r, i: (0, 0, 0)))
    args.append(bias)

    out_spec = pl.BlockSpec((b, BLK, hd), lambda r, i: (0, r * n_blk + i, 0))
    out_shape = [jax.ShapeDtypeStruct((b, dil * length, hd), out_dtype)]
    out_specs = [out_spec]
    if want_lse:
        out_shape.append(jax.ShapeDtypeStruct((b, dil * length, hd), F32))
        out_specs.append(out_spec)

    res = pl.pallas_call(
        functools.partial(_banded_kernel, n_heads=n_heads, n_groups=n_groups, nb=nb,
                          has_sinks=sinks is not None, want_lse=want_lse),
        grid=(dil, n_blk),
        in_specs=in_specs,
        out_specs=out_specs,
        out_shape=out_shape,
        compiler_params=_cparams(("parallel", "arbitrary")),
        name="banded_attention",
    )(*args)
    return [r.reshape(b, dil, length, hd) for r in res]


def _dilated_mix_kernel(*refs, n_pat):
    o_refs, l_refs = refs[:n_pat], refs[n_pat:2 * n_pat]
    out_ref = refs[2 * n_pat]
    scratch = refs[2 * n_pat + 1:]

    def natural(ref, buf):
        d = ref.shape[1]
        if d == 1:
            return ref[0, 0]
        rows = ref.shape[2]
        n_chunks = buf.shape[0]
        for r in range(d):
            for c in range(n_chunks):
                buf[c, pl.ds(r, rows, stride=d), :] = ref[0, r, :, c * LANES:(c + 1) * LANES]
        return jnp.concatenate([buf[c] for c in range(n_chunks)], axis=1)

    os_ = [natural(ref, scratch[2 * p]) for p, ref in enumerate(o_refs)]
    ls = [natural(ref, scratch[2 * p + 1]) for p, ref in enumerate(l_refs)]
    m = functools.reduce(jnp.maximum, ls)
    es = [jnp.exp(l - m) for l in ls]
    den = functools.reduce(jnp.add, es)
    acc = functools.reduce(jnp.add, [(e / den) * o for e, o in zip(es, os_)])
    out_ref[...] = acc.astype(out_ref.dtype)


def _dilated_mix(outs, lses, tm):
    b, _, _, c = outs[0].shape
    seq = outs[0].shape[1] * outs[0].shape[2]
    tiles_per_seq = seq // tm

    def spec(a):
        d = a.shape[1]
        return pl.BlockSpec((1, d, tm // d, c), lambda i: (i // tiles_per_seq, 0, i % tiles_per_seq, 0))

    return pl.pallas_call(
        functools.partial(_dilated_mix_kernel, n_pat=len(outs)),
        grid=(b * tiles_per_seq,),
        in_specs=[spec(a) for a in outs] + [spec(a) for a in lses],
        out_specs=pl.BlockSpec((tm, c), lambda i: (i, 0)),
        out_shape=jax.ShapeDtypeStruct((b * seq, c), BF16),
        scratch_shapes=[pltpu.VMEM((c // LANES, tm, LANES), F32) for _ in range(2 * len(outs))],
        compiler_params=_cparams(("parallel",)),
        name="dilated_mix",
    )(*outs, *lses)


def _retention_kernel(q_ref, k_ref, v_ref, g_ref, cos_ref, sin_ref, dmask_ref, qdec_ref, kdec_ref,
                      cdec_ref, gn_ref, o_ref, state_ref):
    hd = pl.program_id(0)

    @pl.when(pl.program_id(1) == 0)
    def _():
        state_ref[...] = jnp.zeros_like(state_ref)

    cos = cos_ref[...]
    sin = sin_ref[...]
    half = C_QK_DIM // 2

    def rot(x):
        x1, x2 = x[:, :half], x[:, half:]
        return jnp.concatenate([x1 * cos - x2 * sin, x1 * sin + x2 * cos], axis=1)

    for bi in range(q_ref.shape[0]):
        q = rot(q_ref[bi].astype(F32))
        k = rot(k_ref[bi].astype(F32)) * (C_QK_DIM ** -0.5)
        v = v_ref[bi]
        qb = q.astype(BF16)
        inner = lax.dot_general(qb, k.astype(BF16), (((1,), (1,)), ((), ())),
                                preferred_element_type=F32) * dmask_ref[0]
        state = state_ref[bi]
        o = jnp.dot(inner.astype(BF16), v, preferred_element_type=F32)
        o = o + jnp.dot(qb, state.astype(BF16), preferred_element_type=F32) * qdec_ref[0]
        kd_t = jnp.transpose(k * kdec_ref[0]).astype(BF16)
        state_ref[bi] = state * cdec_ref[hd] + jnp.dot(kd_t, v, preferred_element_type=F32)

        mu = jnp.mean(o, axis=-1, keepdims=True)
        oc = o - mu
        var = jnp.mean(oc * oc, axis=-1, keepdims=True)
        on = oc * lax.rsqrt(var + 1e-5)
        o_ref[bi] = (on * gn_ref[...] * jax.nn.silu(g_ref[bi].astype(F32))).astype(o_ref.dtype)


def _retention(slab, gn, seq):
    b = slab.shape[0]
    n_chunks = seq // C_CHUNK
    half = C_QK_DIM // 2
    pos = jnp.arange(seq, dtype=F32)
    inv = 1.0 / (10000.0 ** (jnp.arange(0, C_QK_DIM, 2, dtype=F32) / C_QK_DIM))
    ang = pos[:, None] * inv[None, :]
    cos, sin = jnp.cos(ang), jnp.sin(ang)
    log_g = jnp.log(1.0 - 2.0 ** (-5.0 - jnp.arange(C_HEADS, dtype=F32)))
    j = jnp.arange(C_CHUNK, dtype=F32)
    diff = j[:, None] - j[None, :]
    dmask = jnp.where(diff >= 0, jnp.exp(diff[None] * log_g[:, None, None]), 0.0)
    q_dec = jnp.exp((j[None, :] + 1.0) * log_g[:, None])[:, :, None]
    k_dec = jnp.exp((C_CHUNK - 1.0 - j[None, :]) * log_g[:, None])[:, :, None]
    chunk_dec = jnp.exp(C_CHUNK * log_g)

    def col(off, w):
        return lambda h, c: (0, c, off // w + h)

    return pl.pallas_call(
        _retention_kernel,
        grid=(C_HEADS, n_chunks),
        in_specs=[pl.BlockSpec((b, C_CHUNK, C_QK_DIM), col(OD_QC, C_QK_DIM)),
                  pl.BlockSpec((b, C_CHUNK, C_QK_DIM), col(OD_KC, C_QK_DIM)),
                  pl.BlockSpec((b, C_CHUNK, C_V_DIM), col(OD_VC, C_V_DIM)),
                  pl.BlockSpec((b, C_CHUNK, C_V_DIM), col(OD_GC, C_V_DIM)),
                  pl.BlockSpec((C_CHUNK, half), lambda h, c: (c, 0)),
                  pl.BlockSpec((C_CHUNK, half), lambda h, c: (c, 0)),
                  pl.BlockSpec((1, C_CHUNK, C_CHUNK), lambda h, c: (h, 0, 0)),
                  pl.BlockSpec((1, C_CHUNK, 1), lambda h, c: (h, 0, 0)),
                  pl.BlockSpec((1, C_CHUNK, 1), lambda h, c: (h, 0, 0)),
                  pl.BlockSpec(memory_space=pltpu.SMEM),
                  pl.BlockSpec((1, C_V_DIM), lambda h, c: (0, h))],
        out_specs=pl.BlockSpec((b, C_CHUNK, C_V_DIM), lambda h, c: (0, c, h)),
        out_shape=jax.ShapeDtypeStruct((b, seq, C_V), BF16),
        scratch_shapes=[pltpu.VMEM((b, C_QK_DIM, C_V_DIM), F32)],
        compiler_params=_cparams(("parallel", "arbitrary")),
        name="retention",
    )(slab, slab, slab, slab, cos, sin, dmask, q_dec, k_dec, chunk_dec, gn.reshape(1, C_V).astype(F32))


def _compress_kernel(x_ref, pos_ref, w1_ref, w2_ref, o_ref):
    x = (x_ref[0].astype(F32) + pos_ref[0]).astype(BF16)
    hid = jax.nn.gelu(jnp.dot(x, w1_ref[0], preferred_element_type=F32))
    o_ref[0] = jnp.dot(hid.astype(BF16), w2_ref[0], preferred_element_type=F32)


def _compress(flat, pos, w1, w2):
    _, bg, rows, width = flat.shape
    tr = min(rows, 256)
    return pl.pallas_call(
        _compress_kernel,
        grid=(2, bg, rows // tr),
        in_specs=[pl.BlockSpec((None, 1, tr, width), lambda s, i, r: (s, i, r, 0)),
                  pl.BlockSpec((1, 1, width), lambda s, i, r: (s, 0, 0)),
                  pl.BlockSpec((1, width, D_CMP_HIDDEN), lambda s, i, r: (s, 0, 0)),
                  pl.BlockSpec((1, D_CMP_HIDDEN, HEAD_DIM), lambda s, i, r: (s, 0, 0))],
        out_specs=pl.BlockSpec((None, 1, tr, HEAD_DIM), lambda s, i, r: (s, i, r, 0)),
        out_shape=jax.ShapeDtypeStruct((2, bg, rows, HEAD_DIM), F32),
        compiler_params=_cparams(("parallel", "parallel", "parallel")),
        name="nsa_compress",
    )(flat, pos, w1, w2)


def _cmp_bias(table):
    tt = jnp.arange(BLK)[:, None]
    m = jnp.arange(CMP_WIN)[None, :] + (BLK // D_CMP_STRIDE) - CMP_WIN
    dist = tt - D_CMP_STRIDE * m - (D_CMP_LEN - 1)
    return jnp.where((dist >= 0)[None], _bias_lookup(table, dist), NEG_INF)


def _sel_matrix():
    c_rel = np.arange(CMP_WIN)[:, None] + (BLK // D_CMP_STRIDE) - CMP_WIN
    j_rel = np.arange(REL_BLOCKS)[None, :] - (REL_BLOCKS - 2)
    return ((c_rel >= 4 * j_rel - 1) & (c_rel <= 4 * j_rel + 3)).astype(np.float32)


def _cmp_attn_kernel(q_ref, kc_ref, vc_ref, bias_ref, sel_ref, o_ref, idx_ref):
    i = pl.program_id(1)
    lane = lax.broadcasted_iota(jnp.int32, (1, LANES), 1)
    lo = lane < HEAD_DIM
    half_mask = (jnp.where(lo, SCORE_SCALE, 0.0).astype(BF16), jnp.where(lo, 0.0, SCORE_SCALE).astype(BF16))
    row = lax.broadcasted_iota(jnp.int32, (BLK, REL_BLOCKS), 0)
    jj = lax.broadcasted_iota(jnp.int32, (BLK, REL_BLOCKS), 1)
    cur = (REL_BLOCKS - 2) + (row >= D_SEL_LEN).astype(jnp.int32)
    first = (REL_BLOCKS - 2) - 2 * i
    exists = jj >= first
    forced = exists & ((jj == first) | (jj == cur) | (jj == cur - 1))
    valid = exists & (jj <= cur)
    jjf = jj.astype(F32)
    out_lane = lax.broadcasted_iota(jnp.int32, (BLK, LANES), 1)
    out_row = lax.broadcasted_iota(jnp.int32, (BLK, LANES), 0)
    cur_abs = 2 * i + (out_row >= D_SEL_LEN).astype(jnp.int32)
    tt_in_blk = out_row % D_SEL_LEN

    for width in CMP_WIDTHS:
        lo_w = width - CMP_WIDTHS[0] if width > CMP_WIDTHS[0] else -1

        @pl.when((8 * i + 8 <= width) & (8 * i + 8 > lo_w))
        def _(width=width):
            _cmp_attn_body(q_ref, kc_ref, vc_ref, bias_ref, sel_ref, o_ref, idx_ref, i, width, lo, half_mask,
                           forced, valid, jjf, first, out_lane, cur_abs, tt_in_blk)


def _cmp_attn_body(q_ref, kc_ref, vc_ref, bias_ref, sel_ref, o_ref, idx_ref, i, width, lo, half_mask,
                   forced, valid, jjf, first, out_lane, cur_abs, tt_in_blk):
    hpg = D_HEADS // D_KV_HEADS
    off = CMP_WIN - width
    start = pl.multiple_of(8 * i + 8 + off, 8)
    ucol = lax.broadcasted_iota(jnp.int32, (1, width), 1) + off
    edge = jnp.where(ucol < CMP_WIN - 8 - 8 * i, NEG_INF, 0.0).astype(F32)

    for g in range(D_KV_HEADS):
        kwin = kc_ref[0, g, pl.ds(start, width), :].astype(BF16)
        vwin = vc_ref[0, g, pl.ds(start, width), :].astype(BF16)
        qs = []
        for hh in range(hpg):
            h = g * hpg + hh
            qs.append(q_ref[0, :, (h // 2) * LANES:(h // 2 + 1) * LANES] * half_mask[h % 2])
        s_all = lax.dot_general(jnp.concatenate(qs, axis=0), kwin, (((1,), (1,)), ((), ())),
                                preferred_element_type=F32)
        imp = jnp.zeros((BLK, width), F32)
        pns = []
        for hh in range(hpg):
            h = g * hpg + hh
            s = s_all[hh * BLK:(hh + 1) * BLK] + bias_ref[h, :, off:] + edge
            m = jnp.maximum(jnp.max(s, axis=-1, keepdims=True), 0.1 * NEG_INF)
            p = jnp.exp(s - m)
            den = jnp.maximum(jnp.sum(p, axis=-1, keepdims=True), 1e-30)
            pn = p * (1.0 / den)
            imp = imp + pn
            pns.append(pn.astype(BF16))
        o_all = jnp.dot(jnp.concatenate(pns, axis=0), vwin, preferred_element_type=F32)
        for pr in range(hpg // 2):
            pair = (g * hpg) // 2 + pr
            o_ref[0, :, pair * LANES:(pair + 1) * LANES] = jnp.where(
                lo, o_all[2 * pr * BLK:(2 * pr + 1) * BLK], o_all[(2 * pr + 1) * BLK:(2 * pr + 2) * BLK]
            ).astype(o_ref.dtype)

        sel = sel_ref[off:, :]
        hi = imp.astype(BF16)
        r1 = imp - hi.astype(F32)
        mid = r1.astype(BF16)
        low = (r1 - mid.astype(F32)).astype(BF16)
        imp_sel = (jnp.dot(hi, sel, preferred_element_type=F32)
                   + jnp.dot(mid, sel, preferred_element_type=F32)
                   + jnp.dot(low, sel, preferred_element_type=F32))
        score = jnp.where(forced, 1e9, jnp.where(valid, imp_sel, -1e9))
        picked = jnp.zeros((BLK, LANES), jnp.int32)
        for r in range(D_SEL_COUNT):
            m = jnp.max(score, axis=-1, keepdims=True)
            am = jnp.min(jnp.where(score == m, jjf, float(REL_BLOCKS)), axis=-1, keepdims=True)
            none = m < -5e8
            blk = am.astype(jnp.int32) - first
            key_blk = jnp.where(none, 0, blk)
            bias_row = jnp.where(none, SEL_NONE, jnp.minimum(cur_abs - blk, SEL_FAR)) * D_SEL_LEN + tt_in_blk
            picked = jnp.where(out_lane == r, key_blk, picked)
            picked = jnp.where(out_lane == D_SEL_COUNT + r, bias_row, picked)
            score = jnp.where(jjf == am, -jnp.inf, score)
        idx_ref[0, g] = picked


def _cmp_attention(slab, kc_pad, vc_pad, table, seq):
    b = slab.shape[0]
    n_blk = seq // BLK
    rows = kc_pad.shape[2]
    bias = _cmp_bias(table)
    sel = jnp.asarray(_sel_matrix(), BF16)
    return pl.pallas_call(
        _cmp_attn_kernel,
        grid=(b, n_blk),
        in_specs=[pl.BlockSpec((1, BLK, D_Q), lambda bi, i: (bi, i, OD_QD // D_Q)),
                  pl.BlockSpec((1, D_KV_HEADS, rows, LANES), lambda bi, i: (bi, 0, 0, 0)),
                  pl.BlockSpec((1, D_KV_HEADS, rows, LANES), lambda bi, i: (bi, 0, 0, 0)),
                  pl.BlockSpec((D_HEADS, BLK, CMP_WIN), lambda bi, i: (0, 0, 0)),
                  pl.BlockSpec((CMP_WIN, REL_BLOCKS), lambda bi, i: (0, 0))],
        out_specs=[pl.BlockSpec((1, BLK, D_Q), lambda bi, i: (bi, i, 0)),
                   pl.BlockSpec((1, D_KV_HEADS, BLK, LANES), lambda bi, i: (bi, 0, i, 0))],
        out_shape=[jax.ShapeDtypeStruct((b, seq, D_Q), BF16),
                   jax.ShapeDtypeStruct((b, D_KV_HEADS, seq, LANES), jnp.int32)],
        compiler_params=_cparams(("parallel", "arbitrary")),
        name="nsa_cmp_attention",
    )(slab, kc_pad, vc_pad, bias, sel)


def _sel_bias(table):
    hpg = D_HEADS // D_KV_HEADS
    delta = jnp.arange(SEL_FAR)[:, None, None]
    tt = jnp.arange(D_SEL_LEN)[None, :, None]
    l = jnp.arange(D_SEL_LEN)[None, None, :]
    dist = D_SEL_LEN * delta + tt - l
    near = jnp.where((dist >= 0)[None], _bias_lookup(table, dist), NEG_INF)
    far = jnp.broadcast_to(table.astype(F32)[REL_BUCKETS - 1][:, None, None, None],
                           (D_HEADS, 1, D_SEL_LEN, D_SEL_LEN))
    none = jnp.full((D_HEADS, 1, D_SEL_LEN, D_SEL_LEN), NEG_INF, F32)
    rows = jnp.concatenate([near, far, none], axis=1)
    half = D_SEL_LEN // 2
    rows = rows.reshape(D_KV_HEADS, hpg, SEL_NONE + 1, D_SEL_LEN, half, 2)
    rows = jnp.transpose(rows, (0, 2, 3, 5, 1, 4))
    rows = rows.reshape(D_KV_HEADS, (SEL_NONE + 1) * D_SEL_LEN, 2 * hpg, half)
    return jnp.tile(rows, (1, 1, 1, LANES // half))


SEL_IDX = 2 * D_SEL_COUNT
SEL_UNROLL = 16


def _sel_attn_kernel(idx_hbm, q_ref, kv_ref, bias_ref, o_ref, idx_smem, sem):
    n_g, n_i = pl.num_programs(1), pl.num_programs(2)
    step = (pl.program_id(0) * n_g + pl.program_id(1)) * n_i + pl.program_id(2)
    total = pl.num_programs(0) * n_g * n_i
    slot = step % 2

    tile_words = BLK * SEL_IDX

    def idx_copy(s, sl):
        dst = idx_smem.at[pl.ds(pl.multiple_of(sl * tile_words, tile_words), tile_words)]
        return pltpu.make_async_copy(idx_hbm.at[s], dst, sem.at[sl])

    @pl.when(step == 0)
    def _():
        idx_copy(0, 0).start()

    @pl.when(step + 1 < total)
    def _():
        idx_copy(step + 1, 1 - slot).start()

    idx_copy(step, slot).wait()

    lane = lax.broadcasted_iota(jnp.int32, (1, LANES), 1)
    lo = lane < HEAD_DIM
    hpg = D_HEADS // D_KV_HEADS
    quarter = D_SEL_LEN // 2
    def token_scores(tl):
        picks = idx_smem.at[pl.ds(slot * tile_words + tl * SEL_IDX, SEL_IDX)]
        qq = q_ref[0, 0, tl] * SCORE_SCALE
        qbd = jnp.concatenate([jnp.where(lo, qq, 0.0), jnp.where(lo, 0.0, qq)], axis=0).astype(BF16)
        ks, vs, bs = [], [], []
        for n in range(D_SEL_COUNT):
            kv = kv_ref[0, 0, picks[n]]
            ks.append(kv[:quarter])
            vs.append(kv[quarter:])
            bs.append(bias_ref[0, picks[D_SEL_COUNT + n]])
        s = lax.dot_general(qbd, jnp.concatenate(ks, axis=0), (((1,), (1,)), ((), ())),
                            preferred_element_type=F32)
        bias = jnp.concatenate(
            [jnp.where(lane < quarter, bs[c],
                       jnp.where(lane < 2 * quarter, bs[c + 1],
                                 jnp.where(lane < 3 * quarter, bs[c + 2], bs[c + 3])))
             for c in range(0, D_SEL_COUNT, 4)], axis=1)
        return s + bias, jnp.concatenate(vs, axis=0)

    def fold(x16):
        return x16[:hpg], x16[hpg:]

    def body(it, carry):
        t0 = it * SEL_UNROLL
        sv = [token_scores(t0 + u) for u in range(SEL_UNROLL)]
        ps, dens = [], []
        for s, _ in sv:
            m = jnp.maximum(*fold(jnp.max(s, axis=-1, keepdims=True)))
            p = jnp.exp(s - jnp.concatenate([m, m], axis=0))
            dens.append(sum(fold(jnp.sum(p, axis=-1, keepdims=True))))
            ps.append(p.astype(BF16))
        outs = []
        for p, den, (_, vall) in zip(ps, dens, sv):
            o_top, o_bot = fold(jnp.dot(p, vall, preferred_element_type=F32))
            outs.append(jnp.where(lo, o_top, o_bot) * (1.0 / den))
        for u in range(SEL_UNROLL):
            o_ref[0, 0, t0 + u] = outs[u]
        return carry

    lax.fori_loop(0, BLK // SEL_UNROLL, body, 0)


def _sel_attention(idx, q_sel, kv_sel, table, seq):
    b = q_sel.shape[0]
    hpg = D_HEADS // D_KV_HEADS
    n_blk = seq // BLK
    n_sb = seq // D_SEL_LEN
    bias = _sel_bias(table)
    kv_spec = pl.BlockSpec((1, 1, n_sb, D_SEL_LEN, LANES), lambda bi, g, i: (bi, g, 0, 0, 0))
    return pl.pallas_call(
        _sel_attn_kernel,
        grid=(b, D_KV_HEADS, n_blk),
        in_specs=[pl.BlockSpec(memory_space=pl.ANY),
                  pl.BlockSpec((1, 1, BLK, hpg, LANES), lambda bi, g, i: (bi, g, i, 0, 0)),
                  kv_spec,
                  pl.BlockSpec((1, (SEL_NONE + 1) * D_SEL_LEN, 2 * hpg, LANES), lambda bi, g, i: (g, 0, 0, 0))],
        out_specs=pl.BlockSpec((1, 1, BLK, hpg, LANES), lambda bi, g, i: (bi, g, i, 0, 0)),
        out_shape=jax.ShapeDtypeStruct((b, D_KV_HEADS, seq, hpg, LANES), F32),
        scratch_shapes=[pltpu.SMEM((2 * BLK * SEL_IDX,), jnp.int32), pltpu.SemaphoreType.DMA((2,))],
        compiler_params=_cparams(("arbitrary", "arbitrary", "arbitrary")),
        name="nsa_sel_attention",
    )(idx, q_sel, kv_sel, bias)


def _gate_expand():
    e = np.zeros((3, LANES, D_Q), np.float32)
    for h in range(D_HEADS):
        for c in range(3):
            e[c, 3 * h + c, h * HEAD_DIM:(h + 1) * HEAD_DIM] = 1.0
    return e


def _nsa_gate_kernel(gd_ref, e_ref, oc_ref, os_ref, ow_ref, out_ref):
    sg = jax.nn.sigmoid(gd_ref[...].astype(F32))
    hi = sg.astype(BF16)
    low = (sg - hi.astype(F32)).astype(BF16)
    acc = None
    for c, o_ref in enumerate((oc_ref, os_ref, ow_ref)):
        gate = (jnp.dot(hi, e_ref[c], preferred_element_type=F32)
                + jnp.dot(low, e_ref[c], preferred_element_type=F32))
        term = gate * o_ref[...].astype(F32)
        acc = term if acc is None else acc + term
    out_ref[...] = acc.astype(out_ref.dtype)


def _nsa_gate(slab2d, o_c, o_s, o_w, tm):
    m = slab2d.shape[0]
    e = jnp.asarray(_gate_expand(), BF16)
    spec = pl.BlockSpec((tm, D_Q), lambda i: (i, 0))
    return pl.pallas_call(
        _nsa_gate_kernel,
        grid=(m // tm,),
        in_specs=[pl.BlockSpec((tm, LANES), lambda i: (i, OD_GD // LANES)),
                  pl.BlockSpec((3, LANES, D_Q), lambda i: (0, 0, 0)),
                  spec, spec, spec],
        out_specs=spec,
        out_shape=jax.ShapeDtypeStruct((m, D_Q), BF16),
        compiler_params=_cparams(("parallel",)),
        name="nsa_gate",
    )(slab2d, e, o_c, o_s, o_w)


def _even_mixer(h, nw, w_in, sinks, w_out, rel_table, b, seq):
    qa, ka, va, qb, kb, vb = jnp.split(w_in, [int(c) for c in np.cumsum([A_Q, A_KV, A_KV, B_W, B_W])], axis=1)
    pad = jnp.zeros((D_MODEL, EVEN_SLAB - w_in.shape[1]), w_in.dtype)
    w_slab = jnp.concatenate([qa, qb, kb, vb, ka, va, pad], axis=1).astype(BF16)
    dilations = [dil for _, dil in B_PATTERNS if dil > 1]
    slabs = _norm_matmul(h, nw, w_slab, 1024, EVEN_SLAB // 2, dilations, seq)
    by_dil = {1: slabs[0].reshape(b, 1, seq, EVEN_SLAB)}
    by_dil.update(zip(dilations, slabs[1:]))

    (oa,) = _banded_attention(by_dil[1], q_off=EV_QA, k_off=EV_KA, v_off=EV_VA,
                              n_heads=A_HEADS, n_groups=A_KV_HEADS, max_dist=A_WINDOW - 1,
                              table=rel_table[:, :A_HEADS], sinks=sinks)
    outs, lses = [], []
    for window, dil in B_PATTERNS:
        o, lse = _banded_attention(by_dil[dil], q_off=EV_QB, k_off=EV_KB, v_off=EV_VB,
                                   n_heads=B_HEADS, n_groups=B_HEADS, max_dist=window // dil,
                                   table=rel_table[:, A_HEADS:A_HEADS + B_HEADS], want_lse=True, out_dtype=F32)
        outs.append(o)
        lses.append(lse)
    ob = _dilated_mix(outs, lses, 1024)
    w_out = w_out.astype(BF16)
    return _proj_residual(h, [oa.reshape(b * seq, A_Q), ob], [w_out[:A_Q], w_out[A_Q:]], 512)


def _nsa(slab, pos_k, pos_v, k_w1, k_w2, v_w1, v_w2, rel_table, b, seq):
    g_kv = D_KV_HEADS
    hpg = D_HEADS // g_kv
    slab2d = slab.reshape(b * seq, ODD_SLAB)
    n_rows = seq // D_CMP_STRIDE

    def rows16(off):
        a = slab[:, :, off:off + D_KV].reshape(b, n_rows, D_CMP_STRIDE, g_kv, HEAD_DIM)
        a = jnp.transpose(a, (0, 3, 1, 2, 4)).reshape(b * g_kv, n_rows, D_CMP_STRIDE * HEAD_DIM)
        nxt = jnp.concatenate([a[:, 1:], jnp.zeros_like(a[:, :1])], axis=1)
        return jnp.concatenate([a, nxt], axis=-1)

    flat = jnp.stack([rows16(OD_KCMP), rows16(OD_VCMP)])
    pos = jnp.stack([pos_k.reshape(1, -1), pos_v.reshape(1, -1)]).astype(F32)
    w1 = jnp.stack([k_w1, v_w1]).astype(BF16)
    w2 = jnp.stack([k_w2, v_w2]).astype(BF16)
    cmp = _compress(flat, pos, w1, w2).reshape(2, b, g_kv, n_rows, HEAD_DIM)
    cmp = jnp.pad(cmp, ((0, 0), (0, 0), (0, 0), (CMP_WIN, 0), (0, 0)))
    cmp = jnp.concatenate([cmp, cmp], axis=-1)
    o_c, idx = _cmp_attention(slab, cmp[0], cmp[1], rel_table, seq)

    idx = idx[..., :SEL_IDX].reshape(b * g_kv * (seq // BLK), BLK * SEL_IDX)

    def per_group(off):
        return slab[:, :, off:off + D_KV].reshape(b, seq, g_kv, HEAD_DIM)

    def two_per_row(off):
        a = jnp.transpose(per_group(off), (0, 2, 1, 3))
        return a.reshape(b, g_kv, seq // D_SEL_LEN, D_SEL_LEN // 2, LANES)

    q_sel = slab[:, :, OD_QD:OD_QD + D_Q].reshape(b, seq, g_kv, hpg, HEAD_DIM)
    q_sel = jnp.transpose(q_sel, (0, 2, 1, 3, 4)).astype(F32)
    q_sel = jnp.concatenate([q_sel, q_sel], axis=-1)
    kv_sel = jnp.concatenate([two_per_row(OD_KSLC), two_per_row(OD_VSLC)], axis=3)
    o_s = _sel_attention(idx, q_sel, kv_sel, rel_table, seq)
    o_s = o_s[..., :HEAD_DIM] + o_s[..., HEAD_DIM:]
    o_s = jnp.transpose(o_s, (0, 2, 1, 3, 4)).reshape(b * seq, D_Q).astype(BF16)

    (o_w,) = _banded_attention(slab.reshape(b, 1, seq, ODD_SLAB), q_off=OD_QD, k_off=OD_KWIN, v_off=OD_VWIN,
                               n_heads=D_HEADS, n_groups=D_KV_HEADS, max_dist=D_WINDOW - 1, table=rel_table)
    return _nsa_gate(slab2d, o_c.reshape(b * seq, D_Q), o_s, o_w.reshape(b * seq, D_Q), 1024)


def _odd_mixer(h, nw, w_in, ret_gn, pos_k, pos_v, k_w1, k_w2, v_w1, v_w2, w_out, rel_table, b, seq):
    pad = jnp.zeros((D_MODEL, ODD_SLAB - ODD_IN), w_in.dtype)
    w_slab = jnp.concatenate([w_in, pad], axis=1).astype(BF16)
    slab = _norm_matmul(h, nw, w_slab, 1024, ODD_SLAB // 4).reshape(b, seq, ODD_SLAB)
    oc = _retention(slab, ret_gn, seq)
    od = _nsa(slab, pos_k, pos_v, k_w1, k_w2, v_w1, v_w2, rel_table, b, seq)
    w_out = w_out.astype(BF16)
    return _proj_residual(h, [oc.reshape(b * seq, C_V), od], [w_out[:C_V], w_out[C_V:]], 512)


def kernel(x, rel_table, norm_mix, norm_ffn, norm_final, even_w_in, even_sinks, even_w_out, odd_w_in, odd_ret_gn, odd_cmp_pos_k, odd_cmp_pos_v, odd_cmp_k_w1, odd_cmp_k_w2, odd_cmp_v_w1, odd_cmp_v_w2, odd_w_out, ffn_w_gate, ffn_w_up, ffn_w_down):
    b, seq, d = x.shape
    h = x.reshape(b * seq, d)
    for layer in range(DEPTH):
        li = layer // 2
        if layer % 2 == 0:
            h = _even_mixer(h, norm_mix[layer], even_w_in[li], even_sinks[li], even_w_out[li], rel_table, b, seq)
        else:
            h = _odd_mixer(h, norm_mix[layer], odd_w_in[li], odd_ret_gn[li], odd_cmp_pos_k[li],
                           odd_cmp_pos_v[li], odd_cmp_k_w1[li], odd_cmp_k_w2[li], odd_cmp_v_w1[li],
                           odd_cmp_v_w2[li], odd_w_out[li], rel_table, b, seq)
        h = _ffn(h, norm_ffn[layer], ffn_w_gate[layer].astype(BF16), ffn_w_up[layer].astype(BF16),
                 ffn_w_down[layer].astype(BF16), norm_final, layer == DEPTH - 1, 512)
    return h.reshape(b, seq, d)
```

```python
import functools
import math

import numpy as np
import jax
import jax.numpy as jnp
from jax import lax
from jax.experimental import pallas as pl
from jax.experimental.pallas import tpu as pltpu

F32 = jnp.float32
BF16 = jnp.bfloat16

D_MODEL = 1024
DEPTH = 4
HEAD_DIM = 64
BLK = 128
NEG_INF = -1e30
REL_BUCKETS = 32
REL_MAX_DIST = 2048
A_HEADS = 8
A_KV_HEADS = 2
A_WINDOW = 128
B_HEADS = 8
B_PATTERNS = ((128, 1), (512, 4), (2048, 16))
C_HEADS = 4
C_QK_DIM = 256
C_V_DIM = 512
C_CHUNK = 128
D_HEADS = 16
D_KV_HEADS = 2
D_CMP_LEN = 32
D_CMP_STRIDE = 16
D_CMP_HIDDEN = 128
D_SEL_LEN = 64
D_SEL_COUNT = 16
D_WINDOW = 512
D_FF = 2816

A_Q = A_HEADS * HEAD_DIM
A_KV = A_KV_HEADS * HEAD_DIM
B_W = B_HEADS * HEAD_DIM
C_QK = C_HEADS * C_QK_DIM
C_V = C_HEADS * C_V_DIM
D_Q = D_HEADS * HEAD_DIM
D_KV = D_KV_HEADS * HEAD_DIM
ODD_IN = 2 * C_QK + 2 * C_V + D_Q + 6 * D_KV + 3 * D_HEADS

SCORE_SCALE = HEAD_DIM ** -0.5
LANES = 128
VMEM_LIMIT = 56 * 1024 * 1024

EVEN_SLAB = 2560
EV_QA, EV_QB, EV_KB, EV_VB, EV_KA, EV_VA = 0, 512, 1024, 1536, 2048, 2176
ODD_SLAB = 8192
OD_QC, OD_KC, OD_VC, OD_GC, OD_QD = 0, 1024, 2048, 4096, 6144
OD_KCMP, OD_VCMP, OD_KSLC, OD_VSLC, OD_KWIN, OD_VWIN, OD_GD = 7168, 7296, 7424, 7552, 7680, 7808, 7936

SEL_FAR = 25
SEL_NONE = 26
CMP_WIN = 1024
CMP_WIDTHS = (256, 512, 768, 1024)
REL_BLOCKS = 256


def _cparams(sem):
    return pltpu.CompilerParams(dimension_semantics=sem, vmem_limit_bytes=VMEM_LIMIT)


def _t5_bucket(dist):
    max_exact = REL_BUCKETS // 2
    d = jnp.maximum(dist, 0)
    df = jnp.maximum(d, 1).astype(jnp.float32)
    large = max_exact + (jnp.log(df / max_exact) / math.log(REL_MAX_DIST / max_exact)
                         * (REL_BUCKETS - max_exact)).astype(jnp.int32)
    large = jnp.minimum(large, REL_BUCKETS - 1)
    return jnp.where(d < max_exact, d, large)


def _bias_lookup(table, dist):
    bucket = _t5_bucket(dist)[None]
    tab = table.astype(F32)
    expand = (slice(None),) + (None,) * dist.ndim
    out = jnp.zeros((tab.shape[1],) + dist.shape, F32)
    for b in range(REL_BUCKETS):
        out = jnp.where(bucket == b, tab[b][expand], out)
    return out


def _rms(x, w, eps=1e-6):
    return x * lax.rsqrt(jnp.mean(x * x, axis=-1, keepdims=True) + eps) * w


def _norm_matmul_kernel(h_ref, nw_ref, w_ref, o_ref, *rest, dilations):
    @pl.when(pl.program_id(1) == 0)
    def _():
        rest[-1][...] = _rms(h_ref[...], nw_ref[...]).astype(BF16)

    acc = jnp.dot(rest[-1][...], w_ref[...], preferred_element_type=F32)
    o_ref[...] = acc.astype(o_ref.dtype)
    if dilations:
        acc_ref = rest[-2]
        n_chunks = acc_ref.shape[0]
        for c in range(n_chunks):
            acc_ref[c] = acc[:, c * LANES:(c + 1) * LANES]
        for d, ref in zip(dilations, rest):
            rows = acc_ref.shape[1] // d
            for r in range(d):
                ref[0, r] = jnp.concatenate(
                    [acc_ref[c, pl.ds(r, rows, stride=d), :] for c in range(n_chunks)], axis=1).astype(ref.dtype)


def _norm_matmul(h, nw, w, tm, tn, dilations=(), seq=None):
    m, d_model = h.shape
    n = w.shape[1]
    out_specs = [pl.BlockSpec((tm, tn), lambda i, j: (i, j))]
    out_shape = [jax.ShapeDtypeStruct((m, n), BF16)]
    scratch = []
    if dilations:
        tiles_per_seq = seq // tm
        for d in dilations:
            out_specs.append(pl.BlockSpec((1, d, tm // d, tn),
                                          lambda i, j: (i // tiles_per_seq, 0, i % tiles_per_seq, j)))
            out_shape.append(jax.ShapeDtypeStruct((m // seq, d, seq // d, n), BF16))
        scratch.append(pltpu.VMEM((tn // LANES, tm, LANES), F32))
    scratch.append(pltpu.VMEM((tm, d_model), BF16))
    res = pl.pallas_call(
        functools.partial(_norm_matmul_kernel, dilations=tuple(dilations)),
        grid=(m // tm, n // tn),
        in_specs=[pl.BlockSpec((tm, d_model), lambda i, j: (i, 0)),
                  pl.BlockSpec((1, d_model), lambda i, j: (0, 0)),
                  pl.BlockSpec((d_model, tn), lambda i, j: (0, j))],
        out_specs=out_specs,
        out_shape=out_shape,
        scratch_shapes=scratch,
        compiler_params=_cparams(("parallel", "arbitrary")),
        name="norm_matmul",
    )(h, nw.reshape(1, d_model), w)
    return res if dilations else res[0]


def _proj_residual_kernel(*refs, n_in):
    h_ref, out_ref = refs[0], refs[-1]
    acc = h_ref[...]
    for o_ref, w_ref in zip(refs[1:1 + n_in], refs[1 + n_in:1 + 2 * n_in]):
        acc = acc + jnp.dot(o_ref[...], w_ref[...], preferred_element_type=F32)
    out_ref[...] = acc


def _proj_residual(h, outs, ws, tm):
    m, d = h.shape
    n_in = len(outs)
    in_specs = [pl.BlockSpec((tm, d), lambda i: (i, 0))]
    in_specs += [pl.BlockSpec((tm, o.shape[1]), lambda i: (i, 0)) for o in outs]
    in_specs += [pl.BlockSpec(w.shape, lambda i: (0, 0)) for w in ws]
    return pl.pallas_call(
        functools.partial(_proj_residual_kernel, n_in=n_in),
        grid=(m // tm,),
        in_specs=in_specs,
        out_specs=pl.BlockSpec((tm, d), lambda i: (i, 0)),
        out_shape=jax.ShapeDtypeStruct((m, d), F32),
        compiler_params=_cparams(("parallel",)),
        name="proj_residual",
    )(h, *outs, *ws)


def _ffn_kernel(h_ref, nw_ref, wg_ref, wu_ref, wd_ref, fw_ref, o_ref, *, final):
    h = h_ref[...]
    hn = _rms(h, nw_ref[...]).astype(BF16)
    g = jnp.dot(hn, wg_ref[...], preferred_element_type=F32)
    u = jnp.dot(hn, wu_ref[...], preferred_element_type=F32)
    a = (jax.nn.silu(g) * u).astype(BF16)
    y = h + jnp.dot(a, wd_ref[...], preferred_element_type=F32)
    if final:
        y = _rms(y, fw_ref[...])
    o_ref[...] = y


def _ffn(h, nw, wg, wu, wd, fw, final, tm):
    m, d = h.shape
    ff = wg.shape[1]
    resident = dict(pipeline_mode=pl.Buffered(1))
    return pl.pallas_call(
        functools.partial(_ffn_kernel, final=final),
        grid=(m // tm,),
        in_specs=[pl.BlockSpec((tm, d), lambda i: (i, 0)),
                  pl.BlockSpec((1, d), lambda i: (0, 0)),
                  pl.BlockSpec((d, ff), lambda i: (0, 0), **resident),
                  pl.BlockSpec((d, ff), lambda i: (0, 0), **resident),
                  pl.BlockSpec((ff, d), lambda i: (0, 0), **resident),
                  pl.BlockSpec((1, d), lambda i: (0, 0))],
        out_specs=pl.BlockSpec((tm, d), lambda i: (i, 0)),
        out_shape=jax.ShapeDtypeStruct((m, d), F32),
        compiler_params=_cparams(("parallel",)),
        name="ffn",
    )(h, nw.reshape(1, d), wg, wu, wd, fw.reshape(1, d))


def _band_bias(table, max_dist, dist_scale, nb):
    kw = (nb + 1) * BLK
    rel = jnp.arange(BLK)[:, None] + nb * BLK - jnp.arange(kw)[None, :]
    band = (rel >= 0) & (rel <= max_dist)
    return jnp.where(band[None], _bias_lookup(table, rel * dist_scale), NEG_INF)


def _swap_halves(x):
    return jnp.concatenate([x[:, HEAD_DIM:], x[:, :HEAD_DIM]], axis=1)


def _banded_kernel(*refs, n_heads, n_groups, nb, has_sinks, want_lse):
    pos = 0
    if has_sinks:
        sink_ref = refs[0]
        pos = 1
    q_ref = refs[pos]
    k_refs = refs[pos + 1:pos + 2 + nb]
    v_refs = refs[pos + 2 + nb:pos + 3 + 2 * nb]
    bias_ref = refs[pos + 3 + 2 * nb]
    o_ref = refs[pos + 4 + 2 * nb]
    lse_ref = refs[pos + 5 + 2 * nb] if want_lse else None

    i = pl.program_id(1)
    n_batch = q_ref.shape[0]
    kw = (nb + 1) * BLK
    hpg = n_heads // n_groups
    lane = lax.broadcasted_iota(jnp.int32, (1, LANES), 1)
    lo = lane < HEAD_DIM
    half_mask = (jnp.where(lo, SCORE_SCALE, 0.0).astype(BF16), jnp.where(lo, 0.0, SCORE_SCALE).astype(BF16))
    col = lax.broadcasted_iota(jnp.int32, (1, kw), 1)
    edge = jnp.where(col < (nb - i) * BLK, NEG_INF, 0.0).astype(F32)

    operands = []
    for bi in range(n_batch):
        kcat = jnp.concatenate([k_refs[nb - jj][bi] for jj in range(nb + 1)], axis=0)
        vcat = jnp.concatenate([v_refs[nb - jj][bi] for jj in range(nb + 1)], axis=0)
        if hpg == 1:
            for p in range(n_heads // 2):
                operands.append((bi, kcat[:, p * LANES:(p + 1) * LANES], vcat[:, p * LANES:(p + 1) * LANES],
                                 [2 * p, 2 * p + 1]))
        else:
            k_sw = _swap_halves(kcat)
            v_sw = _swap_halves(vcat)
            for g in range(n_groups):
                for par in range(2):
                    heads = [h for h in range(g * hpg, (g + 1) * hpg) if h % 2 == par]
                    operands.append((bi, kcat if g == par else k_sw, vcat if g == par else v_sw, heads))

    scores = []
    for bi, kh, _, heads in operands:
        qz = jnp.concatenate([q_ref[bi, :, (h // 2) * LANES:(h // 2 + 1) * LANES] * half_mask[h % 2]
                              for h in heads], axis=0)
        scores.append(lax.dot_general(qz, kh, (((1,), (1,)), ((), ())), preferred_element_type=F32))
    ms, dens, probs = {}, {}, []
    for (bi, _, _, heads), s_all in zip(operands, scores):
        ps = []
        for r, h in enumerate(heads):
            s = s_all[r * BLK:(r + 1) * BLK] + bias_ref[h] + edge
            m = jnp.max(s, axis=-1, keepdims=True)
            if has_sinks:
                m = jnp.maximum(m, sink_ref[h])
            p = jnp.exp(s - m)
            den = jnp.sum(p, axis=-1, keepdims=True)
            if has_sinks:
                den = den + jnp.exp(sink_ref[h] - m)
            ps.append(p.astype(BF16))
            ms[bi, h], dens[bi, h] = m, den
        probs.append(jnp.concatenate(ps, axis=0))
    outs = {}
    for (bi, _, vh, heads), p_all in zip(operands, probs):
        o_all = jnp.dot(p_all, vh, preferred_element_type=F32)
        for r, h in enumerate(heads):
            outs[bi, h] = o_all[r * BLK:(r + 1) * BLK]
    for bi in range(n_batch):
        for pair in range(n_heads // 2):
            h0, h1 = (bi, 2 * pair), (bi, 2 * pair + 1)
            inv = jnp.where(lo, 1.0 / dens[h0], 1.0 / dens[h1])
            o_pair = jnp.where(lo, outs[h0], outs[h1]) * inv
            o_ref[bi, :, pair * LANES:(pair + 1) * LANES] = o_pair.astype(o_ref.dtype)
            if want_lse:
                lse_ref[bi, :, pair * LANES:(pair + 1) * LANES] = jnp.where(
                    lo, ms[h0] + jnp.log(dens[h0]), ms[h1] + jnp.log(dens[h1]))


def _banded_attention(slab, *, q_off, k_off, v_off, n_heads, n_groups,
                      max_dist, table, sinks=None, want_lse=False, out_dtype=BF16):
    b, dil, length, width = slab.shape
    n_blk = length // BLK
    nb = -(-max_dist // BLK)
    kw = (nb + 1) * BLK
    hd = n_heads * HEAD_DIM
    gd = n_groups * HEAD_DIM
    view = slab.reshape(b, dil * length, width)
    bias = _band_bias(table, max_dist, dil, nb)

    def q_map(r, i):
        return (0, r * n_blk + i, q_off // hd)

    def kv_map(off, j):
        return lambda r, i: (0, r * n_blk + jnp.maximum(i - j, 0), off // gd)

    in_specs, args = [], []
    if sinks is not None:
        in_specs.append(pl.BlockSpec(memory_space=pltpu.SMEM))
        args.append(sinks.astype(F32))
    in_specs.append(pl.BlockSpec((b, BLK, hd), q_map))
    args.append(view)
    for off in (k_off, v_off):
        for j in range(nb + 1):
            in_specs.append(pl.BlockSpec((b, BLK, gd), kv_map(off, j)))
            args.append(view)
    in_specs.append(pl.BlockSpec((n_heads, BLK, kw), lambda r, i: (0, 0, 0)))
    args.append(bias)

    out_spec = pl.BlockSpec((b, BLK, hd), lambda r, i: (0, r * n_blk + i, 0))
    out_shape = [jax.ShapeDtypeStruct((b, dil * length, hd), out_dtype)]
    out_specs = [out_spec]
    if want_lse:
        out_shape.append(jax.ShapeDtypeStruct((b, dil * length, hd), F32))
        out_specs.append(out_spec)

    res = pl.pallas_call(
        functools.partial(_banded_kernel, n_heads=n_heads, n_groups=n_groups, nb=nb,
                          has_sinks=sinks is not None, want_lse=want_lse),
        grid=(dil, n_blk),
        in_specs=in_specs,
        out_specs=out_specs,
        out_shape=out_shape,
        compiler_params=_cparams(("parallel", "arbitrary")),
        name="banded_attention",
    )(*args)
    return [r.reshape(b, dil, length, hd) for r in res]


def _dilated_mix_kernel(*refs, n_pat):
    o_refs, l_refs = refs[:n_pat], refs[n_pat:2 * n_pat]
    out_ref = refs[2 * n_pat]
    scratch = refs[2 * n_pat + 1:]

    def natural(ref, buf):
        d = ref.shape[1]
        if d == 1:
            return ref[0, 0]
        rows = ref.shape[2]
        n_chunks = buf.shape[0]
        for r in range(d):
            for c in range(n_chunks):
                buf[c, pl.ds(r, rows, stride=d), :] = ref[0, r, :, c * LANES:(c + 1) * LANES]
        return jnp.concatenate([buf[c] for c in range(n_chunks)], axis=1)

    os_ = [natural(ref, scratch[2 * p]) for p, ref in enumerate(o_refs)]
    ls = [natural(ref, scratch[2 * p + 1]) for p, ref in enumerate(l_refs)]
    m = functools.reduce(jnp.maximum, ls)
    es = [jnp.exp(l - m) for l in ls]
    den = functools.reduce(jnp.add, es)
    acc = functools.reduce(jnp.add, [(e / den) * o for e, o in zip(es, os_)])
    out_ref[...] = acc.astype(out_ref.dtype)


def _dilated_mix(outs, lses, tm):
    b, _, _, c = outs[0].shape
    seq = outs[0].shape[1] * outs[0].shape[2]
    tiles_per_seq = seq // tm

    def spec(a):
        d = a.shape[1]
        return pl.BlockSpec((1, d, tm // d, c), lambda i: (i // tiles_per_seq, 0, i % tiles_per_seq, 0))

    return pl.pallas_call(
        functools.partial(_dilated_mix_kernel, n_pat=len(outs)),
        grid=(b * tiles_per_seq,),
        in_specs=[spec(a) for a in outs] + [spec(a) for a in lses],
        out_specs=pl.BlockSpec((tm, c), lambda i: (i, 0)),
        out_shape=jax.ShapeDtypeStruct((b * seq, c), BF16),
        scratch_shapes=[pltpu.VMEM((c // LANES, tm, LANES), F32) for _ in range(2 * len(outs))],
        compiler_params=_cparams(("parallel",)),
        name="dilated_mix",
    )(*outs, *lses)


def _retention_kernel(q_ref, k_ref, v_ref, g_ref, cos_ref, sin_ref, dmask_ref, qdec_ref, kdec_ref,
                      cdec_ref, gn_ref, o_ref, state_ref):
    hd = pl.program_id(0)

    @pl.when(pl.program_id(1) == 0)
    def _():
        state_ref[...] = jnp.zeros_like(state_ref)

    cos = cos_ref[...]
    sin = sin_ref[...]
    half = C_QK_DIM // 2

    def rot(x):
        x1, x2 = x[:, :half], x[:, half:]
        return jnp.concatenate([x1 * cos - x2 * sin, x1 * sin + x2 * cos], axis=1)

    for bi in range(q_ref.shape[0]):
        q = rot(q_ref[bi].astype(F32))
        k = rot(k_ref[bi].astype(F32)) * (C_QK_DIM ** -0.5)
        v = v_ref[bi]
        qb = q.astype(BF16)
        inner = lax.dot_general(qb, k.astype(BF16), (((1,), (1,)), ((), ())),
                                preferred_element_type=F32) * dmask_ref[0]
        state = state_ref[bi]
        o = jnp.dot(inner.astype(BF16), v, preferred_element_type=F32)
        o = o + jnp.dot(qb, state.astype(BF16), preferred_element_type=F32) * qdec_ref[0]
        kd_t = jnp.transpose(k * kdec_ref[0]).astype(BF16)
        state_ref[bi] = state * cdec_ref[hd] + jnp.dot(kd_t, v, preferred_element_type=F32)

        mu = jnp.mean(o, axis=-1, keepdims=True)
        oc = o - mu
        var = jnp.mean(oc * oc, axis=-1, keepdims=True)
        on = oc * lax.rsqrt(var + 1e-5)
        o_ref[bi] = (on * gn_ref[...] * jax.nn.silu(g_ref[bi].astype(F32))).astype(o_ref.dtype)


def _retention(slab, gn, seq):
    b = slab.shape[0]
    n_chunks = seq // C_CHUNK
    half = C_QK_DIM // 2
    pos = jnp.arange(seq, dtype=F32)
    inv = 1.0 / (10000.0 ** (jnp.arange(0, C_QK_DIM, 2, dtype=F32) / C_QK_DIM))
    ang = pos[:, None] * inv[None, :]
    cos, sin = jnp.cos(ang), jnp.sin(ang)
    log_g = jnp.log(1.0 - 2.0 ** (-5.0 - jnp.arange(C_HEADS, dtype=F32)))
    j = jnp.arange(C_CHUNK, dtype=F32)
    diff = j[:, None] - j[None, :]
    dmask = jnp.where(diff >= 0, jnp.exp(diff[None] * log_g[:, None, None]), 0.0)
    q_dec = jnp.exp((j[None, :] + 1.0) * log_g[:, None])[:, :, None]
    k_dec = jnp.exp((C_CHUNK - 1.0 - j[None, :]) * log_g[:, None])[:, :, None]
    chunk_dec = jnp.exp(C_CHUNK * log_g)

    def col(off, w):
        return lambda h, c: (0, c, off // w + h)

    return pl.pallas_call(
        _retention_kernel,
        grid=(C_HEADS, n_chunks),
        in_specs=[pl.BlockSpec((b, C_CHUNK, C_QK_DIM), col(OD_QC, C_QK_DIM)),
                  pl.BlockSpec((b, C_CHUNK, C_QK_DIM), col(OD_KC, C_QK_DIM)),
                  pl.BlockSpec((b, C_CHUNK, C_V_DIM), col(OD_VC, C_V_DIM)),
                  pl.BlockSpec((b, C_CHUNK, C_V_DIM), col(OD_GC, C_V_DIM)),
                  pl.BlockSpec((C_CHUNK, half), lambda h, c: (c, 0)),
                  pl.BlockSpec((C_CHUNK, half), lambda h, c: (c, 0)),
                  pl.BlockSpec((1, C_CHUNK, C_CHUNK), lambda h, c: (h, 0, 0)),
                  pl.BlockSpec((1, C_CHUNK, 1), lambda h, c: (h, 0, 0)),
                  pl.BlockSpec((1, C_CHUNK, 1), lambda h, c: (h, 0, 0)),
                  pl.BlockSpec(memory_space=pltpu.SMEM),
                  pl.BlockSpec((1, C_V_DIM), lambda h, c: (0, h))],
        out_specs=pl.BlockSpec((b, C_CHUNK, C_V_DIM), lambda h, c: (0, c, h)),
        out_shape=jax.ShapeDtypeStruct((b, seq, C_V), BF16),
        scratch_shapes=[pltpu.VMEM((b, C_QK_DIM, C_V_DIM), F32)],
        compiler_params=_cparams(("parallel", "arbitrary")),
        name="retention",
    )(slab, slab, slab, slab, cos, sin, dmask, q_dec, k_dec, chunk_dec, gn.reshape(1, C_V).astype(F32))


def _compress_kernel(x_ref, pos_ref, w1_ref, w2_ref, o_ref):
    x = (x_ref[0].astype(F32) + pos_ref[0]).astype(BF16)
    hid = jax.nn.gelu(jnp.dot(x, w1_ref[0], preferred_element_type=F32))
    o_ref[0] = jnp.dot(hid.astype(BF16), w2_ref[0], preferred_element_type=F32)


def _compress(flat, pos, w1, w2):
    _, bg, rows, width = flat.shape
    tr = min(rows, 256)
    return pl.pallas_call(
        _compress_kernel,
        grid=(2, bg, rows // tr),
        in_specs=[pl.BlockSpec((None, 1, tr, width), lambda s, i, r: (s, i, r, 0)),
                  pl.BlockSpec((1, 1, width), lambda s, i, r: (s, 0, 0)),
                  pl.BlockSpec((1, width, D_CMP_HIDDEN), lambda s, i, r: (s, 0, 0)),
                  pl.BlockSpec((1, D_CMP_HIDDEN, HEAD_DIM), lambda s, i, r: (s, 0, 0))],
        out_specs=pl.BlockSpec((None, 1, tr, HEAD_DIM), lambda s, i, r: (s, i, r, 0)),
        out_shape=jax.ShapeDtypeStruct((2, bg, rows, HEAD_DIM), F32),
        compiler_params=_cparams(("parallel", "parallel", "parallel")),
        name="nsa_compress",
    )(flat, pos, w1, w2)


def _cmp_bias(table):
    tt = jnp.arange(BLK)[:, None]
    m = jnp.arange(CMP_WIN)[None, :] + (BLK // D_CMP_STRIDE) - CMP_WIN
    dist = tt - D_CMP_STRIDE * m - (D_CMP_LEN - 1)
    return jnp.where((dist >= 0)[None], _bias_lookup(table, dist), NEG_INF)


def _sel_matrix():
    c_rel = np.arange(CMP_WIN)[:, None] + (BLK // D_CMP_STRIDE) - CMP_WIN
    j_rel = np.arange(REL_BLOCKS)[None, :] - (REL_BLOCKS - 2)
    return ((c_rel >= 4 * j_rel - 1) & (c_rel <= 4 * j_rel + 3)).astype(np.float32)


def _cmp_attn_kernel(q_ref, kc_ref, vc_ref, bias_ref, sel_ref, o_ref, idx_ref):
    i = pl.program_id(1)
    lane = lax.broadcasted_iota(jnp.int32, (1, LANES), 1)
    lo = lane < HEAD_DIM
    half_mask = (jnp.where(lo, SCORE_SCALE, 0.0).astype(BF16), jnp.where(lo, 0.0, SCORE_SCALE).astype(BF16))
    row = lax.broadcasted_iota(jnp.int32, (BLK, REL_BLOCKS), 0)
    jj = lax.broadcasted_iota(jnp.int32, (BLK, REL_BLOCKS), 1)
    cur = (REL_BLOCKS - 2) + (row >= D_SEL_LEN).astype(jnp.int32)
    first = (REL_BLOCKS - 2) - 2 * i
    exists = jj >= first
    forced = exists & ((jj == first) | (jj == cur) | (jj == cur - 1))
    valid = exists & (jj <= cur)
    jjf = jj.astype(F32)
    out_lane = lax.broadcasted_iota(jnp.int32, (BLK, LANES), 1)
    out_row = lax.broadcasted_iota(jnp.int32, (BLK, LANES), 0)
    cur_abs = 2 * i + (out_row >= D_SEL_LEN).astype(jnp.int32)
    tt_in_blk = out_row % D_SEL_LEN

    for width in CMP_WIDTHS:
        lo_w = width - CMP_WIDTHS[0] if width > CMP_WIDTHS[0] else -1

        @pl.when((8 * i + 8 <= width) & (8 * i + 8 > lo_w))
        def _(width=width):
            _cmp_attn_body(q_ref, kc_ref, vc_ref, bias_ref, sel_ref, o_ref, idx_ref, i, width, lo, half_mask,
                           forced, valid, jjf, first, out_lane, cur_abs, tt_in_blk)


def _cmp_attn_body(q_ref, kc_ref, vc_ref, bias_ref, sel_ref, o_ref, idx_ref, i, width, lo, half_mask,
                   forced, valid, jjf, first, out_lane, cur_abs, tt_in_blk):
    hpg = D_HEADS // D_KV_HEADS
    off = CMP_WIN - width
    start = pl.multiple_of(8 * i + 8 + off, 8)
    ucol = lax.broadcasted_iota(jnp.int32, (1, width), 1) + off
    edge = jnp.where(ucol < CMP_WIN - 8 - 8 * i, NEG_INF, 0.0).astype(F32)

    for g in range(D_KV_HEADS):
        kwin = kc_ref[0, g, pl.ds(start, width), :].astype(BF16)
        vwin = vc_ref[0, g, pl.ds(start, width), :].astype(BF16)
        qs = []
        for hh in range(hpg):
            h = g * hpg + hh
            qs.append(q_ref[0, :, (h // 2) * LANES:(h // 2 + 1) * LANES] * half_mask[h % 2])
        s_all = lax.dot_general(jnp.concatenate(qs, axis=0), kwin, (((1,), (1,)), ((), ())),
                                preferred_element_type=F32)
        imp = jnp.zeros((BLK, width), F32)
        pns = []
        for hh in range(hpg):
            h = g * hpg + hh
            s = s_all[hh * BLK:(hh + 1) * BLK] + bias_ref[h, :, off:] + edge
            m = jnp.maximum(jnp.max(s, axis=-1, keepdims=True), 0.1 * NEG_INF)
            p = jnp.exp(s - m)
            den = jnp.maximum(jnp.sum(p, axis=-1, keepdims=True), 1e-30)
            pn = p * (1.0 / den)
            imp = imp + pn
            pns.append(pn.astype(BF16))
        o_all = jnp.dot(jnp.concatenate(pns, axis=0), vwin, preferred_element_type=F32)
        for pr in range(hpg // 2):
            pair = (g * hpg) // 2 + pr
            o_ref[0, :, pair * LANES:(pair + 1) * LANES] = jnp.where(
                lo, o_all[2 * pr * BLK:(2 * pr + 1) * BLK], o_all[(2 * pr + 1) * BLK:(2 * pr + 2) * BLK]
            ).astype(o_ref.dtype)

        sel = sel_ref[off:, :]
        hi = imp.astype(BF16)
        r1 = imp - hi.astype(F32)
        mid = r1.astype(BF16)
        low = (r1 - mid.astype(F32)).astype(BF16)
        imp_sel = (jnp.dot(hi, sel, preferred_element_type=F32)
                   + jnp.dot(mid, sel, preferred_element_type=F32)
                   + jnp.dot(low, sel, preferred_element_type=F32))
        score = jnp.where(forced, 1e9, jnp.where(valid, imp_sel, -1e9))
        picked = jnp.zeros((BLK, LANES), jnp.int32)
        for r in range(D_SEL_COUNT):
            m = jnp.max(score, axis=-1, keepdims=True)
            am = jnp.min(jnp.where(score == m, jjf, float(REL_BLOCKS)), axis=-1, keepdims=True)
            none = m < -5e8
            blk = am.astype(jnp.int32) - first
            key_blk = jnp.where(none, 0, blk)
            bias_row = jnp.where(none, SEL_NONE, jnp.minimum(cur_abs - blk, SEL_FAR)) * D_SEL_LEN + tt_in_blk
            picked = jnp.where(out_lane == r, key_blk, picked)
            picked = jnp.where(out_lane == D_SEL_COUNT + r, bias_row, picked)
            score = jnp.where(jjf == am, -jnp.inf, score)
        idx_ref[0, g] = picked


def _cmp_attention(slab, kc_pad, vc_pad, table, seq):
    b = slab.shape[0]
    n_blk = seq // BLK
    rows = kc_pad.shape[2]
    bias = _cmp_bias(table)
    sel = jnp.asarray(_sel_matrix(), BF16)
    return pl.pallas_call(
        _cmp_attn_kernel,
        grid=(b, n_blk),
        in_specs=[pl.BlockSpec((1, BLK, D_Q), lambda bi, i: (bi, i, OD_QD // D_Q)),
                  pl.BlockSpec((1, D_KV_HEADS, rows, LANES), lambda bi, i: (bi, 0, 0, 0)),
                  pl.BlockSpec((1, D_KV_HEADS, rows, LANES), lambda bi, i: (bi, 0, 0, 0)),
                  pl.BlockSpec((D_HEADS, BLK, CMP_WIN), lambda bi, i: (0, 0, 0)),
                  pl.BlockSpec((CMP_WIN, REL_BLOCKS), lambda bi, i: (0, 0))],
        out_specs=[pl.BlockSpec((1, BLK, D_Q), lambda bi, i: (bi, i, 0)),
                   pl.BlockSpec((1, D_KV_HEADS, BLK, LANES), lambda bi, i: (bi, 0, i, 0))],
        out_shape=[jax.ShapeDtypeStruct((b, seq, D_Q), BF16),
                   jax.ShapeDtypeStruct((b, D_KV_HEADS, seq, LANES), jnp.int32)],
        compiler_params=_cparams(("parallel", "arbitrary")),
        name="nsa_cmp_attention",
    )(slab, kc_pad, vc_pad, bias, sel)


def _sel_bias(table):
    hpg = D_HEADS // D_KV_HEADS
    delta = jnp.arange(SEL_FAR)[:, None, None]
    tt = jnp.arange(D_SEL_LEN)[None, :, None]
    l = jnp.arange(D_SEL_LEN)[None, None, :]
    dist = D_SEL_LEN * delta + tt - l
    near = jnp.where((dist >= 0)[None], _bias_lookup(table, dist), NEG_INF)
    far = jnp.broadcast_to(table.astype(F32)[REL_BUCKETS - 1][:, None, None, None],
                           (D_HEADS, 1, D_SEL_LEN, D_SEL_LEN))
    none = jnp.full((D_HEADS, 1, D_SEL_LEN, D_SEL_LEN), NEG_INF, F32)
    rows = jnp.concatenate([near, far, none], axis=1)
    half = D_SEL_LEN // 2
    rows = rows.reshape(D_KV_HEADS, hpg, SEL_NONE + 1, D_SEL_LEN, half, 2)
    rows = jnp.transpose(rows, (0, 2, 3, 5, 1, 4))
    rows = rows.reshape(D_KV_HEADS, (SEL_NONE + 1) * D_SEL_LEN, 2 * hpg, half)
    return jnp.tile(rows, (1, 1, 1, LANES // half))


SEL_IDX = 2 * D_SEL_COUNT
SEL_UNROLL = 16


def _sel_attn_kernel(idx_hbm, q_ref, kv_ref, bias_ref, o_ref, idx_smem, sem):
    n_g, n_i = pl.num_programs(1), pl.num_programs(2)
    step = (pl.program_id(0) * n_g + pl.program_id(1)) * n_i + pl.program_id(2)
    total = pl.num_programs(0) * n_g * n_i
    slot = step % 2

    tile_words = BLK * SEL_IDX

    def idx_copy(s, sl):
        dst = idx_smem.at[pl.ds(pl.multiple_of(sl * tile_words, tile_words), tile_words)]
        return pltpu.make_async_copy(idx_hbm.at[s], dst, sem.at[sl])

    @pl.when(step == 0)
    def _():
        idx_copy(0, 0).start()

    @pl.when(step + 1 < total)
    def _():
        idx_copy(step + 1, 1 - slot).start()

    idx_copy(step, slot).wait()

    lane = lax.broadcasted_iota(jnp.int32, (1, LANES), 1)
    lo = lane < HEAD_DIM
    hpg = D_HEADS // D_KV_HEADS
    quarter = D_SEL_LEN // 2
    def token_scores(tl):
        picks = idx_smem.at[pl.ds(slot * tile_words + tl * SEL_IDX, SEL_IDX)]
        qq = q_ref[0, 0, tl] * SCORE_SCALE
        qbd = jnp.concatenate([jnp.where(lo, qq, 0.0), jnp.where(lo, 0.0, qq)], axis=0).astype(BF16)
        ks, vs, bs = [], [], []
        for n in range(D_SEL_COUNT):
            kv = kv_ref[0, 0, picks[n]]
            ks.append(kv[:quarter])
            vs.append(kv[quarter:])
            bs.append(bias_ref[0, picks[D_SEL_COUNT + n]])
        s = lax.dot_general(qbd, jnp.concatenate(ks, axis=0), (((1,), (1,)), ((), ())),
                            preferred_element_type=F32)
        bias = jnp.concatenate(
            [jnp.where(lane < quarter, bs[c],
                       jnp.where(lane < 2 * quarter, bs[c + 1],
                                 jnp.where(lane < 3 * quarter, bs[c + 2], bs[c + 3])))
             for c in range(0, D_SEL_COUNT, 4)], axis=1)
        return s + bias, jnp.concatenate(vs, axis=0)

    def fold(x16):
        return x16[:hpg], x16[hpg:]

    def body(it, carry):
        t0 = it * SEL_UNROLL
        sv = [token_scores(t0 + u) for u in range(SEL_UNROLL)]
        ps, dens = [], []
        for s, _ in sv:
            m = jnp.maximum(*fold(jnp.max(s, axis=-1, keepdims=True)))
            p = jnp.exp(s - jnp.concatenate([m, m], axis=0))
            dens.append(sum(fold(jnp.sum(p, axis=-1, keepdims=True))))
            ps.append(p.astype(BF16))
        outs = []
        for p, den, (_, vall) in zip(ps, dens, sv):
            o_top, o_bot = fold(jnp.dot(p, vall, preferred_element_type=F32))
            outs.append(jnp.where(lo, o_top, o_bot) * (1.0 / den))
        for u in range(SEL_UNROLL):
            o_ref[0, 0, t0 + u] = outs[u]
        return carry

    lax.fori_loop(0, BLK // SEL_UNROLL, body, 0)


def _sel_attention(idx, q_sel, kv_sel, table, seq):
    b = q_sel.shape[0]
    hpg = D_HEADS // D_KV_HEADS
    n_blk = seq // BLK
    n_sb = seq // D_SEL_LEN
    bias = _sel_bias(table)
    kv_spec = pl.BlockSpec((1, 1, n_sb, D_SEL_LEN, LANES), lambda bi, g, i: (bi, g, 0, 0, 0))
    return pl.pallas_call(
        _sel_attn_kernel,
        grid=(b, D_KV_HEADS, n_blk),
        in_specs=[pl.BlockSpec(memory_space=pl.ANY),
                  pl.BlockSpec((1, 1, BLK, hpg, LANES), lambda bi, g, i: (bi, g, i, 0, 0)),
                  kv_spec,
                  pl.BlockSpec((1, (SEL_NONE + 1) * D_SEL_LEN, 2 * hpg, LANES), lambda bi, g, i: (g, 0, 0, 0))],
        out_specs=pl.BlockSpec((1, 1, BLK, hpg, LANES), lambda bi, g, i: (bi, g, i, 0, 0)),
        out_shape=jax.ShapeDtypeStruct((b, D_KV_HEADS, seq, hpg, LANES), F32),
        scratch_shapes=[pltpu.SMEM((2 * BLK * SEL_IDX,), jnp.int32), pltpu.SemaphoreType.DMA((2,))],
        compiler_params=_cparams(("arbitrary", "arbitrary", "arbitrary")),
        name="nsa_sel_attention",
    )(idx, q_sel, kv_sel, bias)


def _gate_expand():
    e = np.zeros((3, LANES, D_Q), np.float32)
    for h in range(D_HEADS):
        for c in range(3):
            e[c, 3 * h + c, h * HEAD_DIM:(h + 1) * HEAD_DIM] = 1.0
    return e


def _head_place():
    hpg = D_HEADS // D_KV_HEADS
    p = np.zeros((hpg, LANES, hpg * HEAD_DIM), np.float32)
    for hh in range(hpg):
        for lane in range(LANES):
            p[hh, lane, hh * HEAD_DIM + lane % HEAD_DIM] = 1.0
    return p


def _nsa_gate_kernel(gd_ref, e_ref, place_ref, oc_ref, os_ref, ow_ref, out_ref):
    hpg = D_HEADS // D_KV_HEADS
    groups = []
    for g in range(D_KV_HEADS):
        acc_g = None
        for hh in range(hpg):
            part = jnp.dot(os_ref[0, g, :, hh, :].astype(BF16), place_ref[hh], preferred_element_type=F32)
            acc_g = part if acc_g is None else acc_g + part
        groups.append(acc_g)
    o_s = jnp.concatenate(groups, axis=1)

    sg = jax.nn.sigmoid(gd_ref[...].astype(F32))
    hi = sg.astype(BF16)
    low = (sg - hi.astype(F32)).astype(BF16)
    acc = None
    for c, branch in enumerate((oc_ref[...].astype(F32), o_s, ow_ref[...].astype(F32))):
        gate = (jnp.dot(hi, e_ref[c], preferred_element_type=F32)
                + jnp.dot(low, e_ref[c], preferred_element_type=F32))
        acc = gate * branch if acc is None else acc + gate * branch
    out_ref[...] = acc.astype(out_ref.dtype)


def _nsa_gate(slab2d, o_c, o_s_raw, o_w, tm):
    m = slab2d.shape[0]
    b, g_kv, seq, hpg, _ = o_s_raw.shape
    tiles_per_seq = seq // tm
    e = jnp.asarray(_gate_expand(), BF16)
    place = jnp.asarray(_head_place(), BF16)
    spec = pl.BlockSpec((tm, D_Q), lambda i: (i, 0))
    return pl.pallas_call(
        _nsa_gate_kernel,
        grid=(m // tm,),
        in_specs=[pl.BlockSpec((tm, LANES), lambda i: (i, OD_GD // LANES)),
                  pl.BlockSpec((3, LANES, D_Q), lambda i: (0, 0, 0)),
                  pl.BlockSpec((hpg, LANES, hpg * HEAD_DIM), lambda i: (0, 0, 0)),
                  spec,
                  pl.BlockSpec((1, g_kv, tm, hpg, LANES),
                               lambda i: (i // tiles_per_seq, 0, i % tiles_per_seq, 0, 0)),
                  spec],
        out_specs=spec,
        out_shape=jax.ShapeDtypeStruct((m, D_Q), BF16),
        compiler_params=_cparams(("parallel",)),
        name="nsa_gate",
    )(slab2d, e, place, o_c, o_s_raw, o_w)


def _even_mixer(h, nw, w_in, sinks, w_out, rel_table, b, seq):
    qa, ka, va, qb, kb, vb = jnp.split(w_in, [int(c) for c in np.cumsum([A_Q, A_KV, A_KV, B_W, B_W])], axis=1)
    pad = jnp.zeros((D_MODEL, EVEN_SLAB - w_in.shape[1]), w_in.dtype)
    w_slab = jnp.concatenate([qa, qb, kb, vb, ka, va, pad], axis=1).astype(BF16)
    dilations = [dil for _, dil in B_PATTERNS if dil > 1]
    slabs = _norm_matmul(h, nw, w_slab, 1024, EVEN_SLAB // 2, dilations, seq)
    by_dil = {1: slabs[0].reshape(b, 1, seq, EVEN_SLAB)}
    by_dil.update(zip(dilations, slabs[1:]))

    (oa,) = _banded_attention(by_dil[1], q_off=EV_QA, k_off=EV_KA, v_off=EV_VA,
                              n_heads=A_HEADS, n_groups=A_KV_HEADS, max_dist=A_WINDOW - 1,
                              table=rel_table[:, :A_HEADS], sinks=sinks)
    outs, lses = [], []
    for window, dil in B_PATTERNS:
        o, lse = _banded_attention(by_dil[dil], q_off=EV_QB, k_off=EV_KB, v_off=EV_VB,
                                   n_heads=B_HEADS, n_groups=B_HEADS, max_dist=window // dil,
                                   table=rel_table[:, A_HEADS:A_HEADS + B_HEADS], want_lse=True, out_dtype=F32)
        outs.append(o)
        lses.append(lse)
    ob = _dilated_mix(outs, lses, 1024)
    w_out = w_out.astype(BF16)
    return _proj_residual(h, [oa.reshape(b * seq, A_Q), ob], [w_out[:A_Q], w_out[A_Q:]], 512)


def _nsa(slab, pos_k, pos_v, k_w1, k_w2, v_w1, v_w2, rel_table, b, seq):
    g_kv = D_KV_HEADS
    hpg = D_HEADS // g_kv
    slab2d = slab.reshape(b * seq, ODD_SLAB)
    n_rows = seq // D_CMP_STRIDE

    def rows16(off):
        a = slab[:, :, off:off + D_KV].reshape(b, n_rows, D_CMP_STRIDE, g_kv, HEAD_DIM)
        a = jnp.transpose(a, (0, 3, 1, 2, 4)).reshape(b * g_kv, n_rows, D_CMP_STRIDE * HEAD_DIM)
        nxt = jnp.concatenate([a[:, 1:], jnp.zeros_like(a[:, :1])], axis=1)
        return jnp.concatenate([a, nxt], axis=-1)

    flat = jnp.stack([rows16(OD_KCMP), rows16(OD_VCMP)])
    pos = jnp.stack([pos_k.reshape(1, -1), pos_v.reshape(1, -1)]).astype(F32)
    w1 = jnp.stack([k_w1, v_w1]).astype(BF16)
    w2 = jnp.stack([k_w2, v_w2]).astype(BF16)
    cmp = _compress(flat, pos, w1, w2).reshape(2, b, g_kv, n_rows, HEAD_DIM)
    cmp = jnp.pad(cmp, ((0, 0), (0, 0), (0, 0), (CMP_WIN, 0), (0, 0)))
    cmp = jnp.concatenate([cmp, cmp], axis=-1)
    o_c, idx = _cmp_attention(slab, cmp[0], cmp[1], rel_table, seq)

    idx = idx[..., :SEL_IDX].reshape(b * g_kv * (seq // BLK), BLK * SEL_IDX)

    def per_group(off):
        return slab[:, :, off:off + D_KV].reshape(b, seq, g_kv, HEAD_DIM)

    def two_per_row(off):
        a = jnp.transpose(per_group(off), (0, 2, 1, 3))
        return a.reshape(b, g_kv, seq // D_SEL_LEN, D_SEL_LEN // 2, LANES)

    q_sel = slab[:, :, OD_QD:OD_QD + D_Q].reshape(b, seq, g_kv, hpg, HEAD_DIM)
    q_sel = jnp.transpose(q_sel, (0, 2, 1, 3, 4)).astype(F32)
    q_sel = jnp.concatenate([q_sel, q_sel], axis=-1)
    kv_sel = jnp.concatenate([two_per_row(OD_KSLC), two_per_row(OD_VSLC)], axis=3)
    o_s = _sel_attention(idx, q_sel, kv_sel, rel_table, seq)

    (o_w,) = _banded_attention(slab.reshape(b, 1, seq, ODD_SLAB), q_off=OD_QD, k_off=OD_KWIN, v_off=OD_VWIN,
                               n_heads=D_HEADS, n_groups=D_KV_HEADS, max_dist=D_WINDOW - 1, table=rel_table)
    return _nsa_gate(slab2d, o_c.reshape(b * seq, D_Q), o_s, o_w.reshape(b * seq, D_Q), 512)


def _odd_mixer(h, nw, w_in, ret_gn, pos_k, pos_v, k_w1, k_w2, v_w1, v_w2, w_out, rel_table, b, seq):
    pad = jnp.zeros((D_MODEL, ODD_SLAB - ODD_IN), w_in.dtype)
    w_slab = jnp.concatenate([w_in, pad], axis=1).astype(BF16)
    slab = _norm_matmul(h, nw, w_slab, 1024, ODD_SLAB // 4).reshape(b, seq, ODD_SLAB)
    oc = _retention(slab, ret_gn, seq)
    od = _nsa(slab, pos_k, pos_v, k_w1, k_w2, v_w1, v_w2, rel_table, b, seq)
    w_out = w_out.astype(BF16)
    return _proj_residual(h, [oc.reshape(b * seq, C_V), od], [w_out[:C_V], w_out[C_V:]], 512)


def kernel(x, rel_table, norm_mix, norm_ffn, norm_final, even_w_in, even_sinks, even_w_out, odd_w_in, odd_ret_gn, odd_cmp_pos_k, odd_cmp_pos_v, odd_cmp_k_w1, odd_cmp_k_w2, odd_cmp_v_w1, odd_cmp_v_w2, odd_w_out, ffn_w_gate, ffn_w_up, ffn_w_down):
    b, seq, d = x.shape
    h = x.reshape(b * seq, d)
    for layer in range(DEPTH):
        li = layer // 2
        if layer % 2 == 0:
            h = _even_mixer(h, norm_mix[layer], even_w_in[li], even_sinks[li], even_w_out[li], rel_table, b, seq)
        else:
            h = _odd_mixer(h, norm_mix[layer], odd_w_in[li], odd_ret_gn[li], odd_cmp_pos_k[li],
                           odd_cmp_pos_v[li], odd_cmp_k_w1[li], odd_cmp_k_w2[li], odd_cmp_v_w1[li],
                           odd_cmp_v_w2[li], odd_w_out[li], rel_table, b, seq)
        h = _ffn(h, norm_ffn[layer], ffn_w_gate[layer].astype(BF16), ffn_w_up[layer].astype(BF16),
                 ffn_w_down[layer].astype(BF16), norm_final, layer == DEPTH - 1, 512)
    return h.reshape(b, seq, d)
```

```python
import functools
import math

import numpy as np
import jax
import jax.numpy as jnp
from jax import lax
from jax.experimental import pallas as pl
from jax.experimental.pallas import tpu as pltpu

F32 = jnp.float32
BF16 = jnp.bfloat16

D_MODEL = 1024
DEPTH = 4
HEAD_DIM = 64
BLK = 128
NEG_INF = -1e30
REL_BUCKETS = 32
REL_MAX_DIST = 2048
A_HEADS = 8
A_KV_HEADS = 2
A_WINDOW = 128
B_HEADS = 8
B_PATTERNS = ((128, 1), (512, 4), (2048, 16))
C_HEADS = 4
C_QK_DIM = 256
C_V_DIM = 512
C_CHUNK = 128
D_HEADS = 16
D_KV_HEADS = 2
D_CMP_LEN = 32
D_CMP_STRIDE = 16
D_CMP_HIDDEN = 128
D_SEL_LEN = 64
D_SEL_COUNT = 16
D_WINDOW = 512
D_FF = 2816

A_Q = A_HEADS * HEAD_DIM
A_KV = A_KV_HEADS * HEAD_DIM
B_W = B_HEADS * HEAD_DIM
C_QK = C_HEADS * C_QK_DIM
C_V = C_HEADS * C_V_DIM
D_Q = D_HEADS * HEAD_DIM
D_KV = D_KV_HEADS * HEAD_DIM
ODD_IN = 2 * C_QK + 2 * C_V + D_Q + 6 * D_KV + 3 * D_HEADS

SCORE_SCALE = HEAD_DIM ** -0.5
LANES = 128
VMEM_LIMIT = 56 * 1024 * 1024

EVEN_SLAB = 2560
EV_QA, EV_QB, EV_KB, EV_VB, EV_KA, EV_VA = 0, 512, 1024, 1536, 2048, 2176
ODD_SLAB = 8192
OD_QC, OD_KC, OD_VC, OD_GC, OD_QD = 0, 1024, 2048, 4096, 6144
OD_KCMP, OD_VCMP, OD_KSLC, OD_VSLC, OD_KWIN, OD_VWIN, OD_GD = 7168, 7296, 7424, 7552, 7680, 7808, 7936

SEL_FAR = 25
SEL_NONE = 26
CMP_WIN = 1024
CMP_WIDTHS = (256, 512, 768, 1024)
REL_BLOCKS = 256


def _cparams(sem):
    return pltpu.CompilerParams(dimension_semantics=sem, vmem_limit_bytes=VMEM_LIMIT)


def _t5_bucket(dist):
    max_exact = REL_BUCKETS // 2
    d = jnp.maximum(dist, 0)
    df = jnp.maximum(d, 1).astype(jnp.float32)
    large = max_exact + (jnp.log(df / max_exact) / math.log(REL_MAX_DIST / max_exact)
                         * (REL_BUCKETS - max_exact)).astype(jnp.int32)
    large = jnp.minimum(large, REL_BUCKETS - 1)
    return jnp.where(d < max_exact, d, large)


def _bias_lookup(table, dist):
    bucket = _t5_bucket(dist)[None]
    tab = table.astype(F32)
    expand = (slice(None),) + (None,) * dist.ndim
    out = jnp.zeros((tab.shape[1],) + dist.shape, F32)
    for b in range(REL_BUCKETS):
        out = jnp.where(bucket == b, tab[b][expand], out)
    return out


def _rms(x, w, eps=1e-6):
    return x * lax.rsqrt(jnp.mean(x * x, axis=-1, keepdims=True) + eps) * w


def _norm_matmul_kernel(h_ref, nw_ref, w_ref, o_ref, *rest, dilations):
    @pl.when(pl.program_id(1) == 0)
    def _():
        rest[-1][...] = _rms(h_ref[...], nw_ref[...]).astype(BF16)

    acc = jnp.dot(rest[-1][...], w_ref[...], preferred_element_type=F32)
    o_ref[...] = acc.astype(o_ref.dtype)
    if dilations:
        acc_ref = rest[-2]
        n_chunks = acc_ref.shape[0]
        for c in range(n_chunks):
            acc_ref[c] = acc[:, c * LANES:(c + 1) * LANES]
        for d, ref in zip(dilations, rest):
            rows = acc_ref.shape[1] // d
            for r in range(d):
                ref[0, r] = jnp.concatenate(
                    [acc_ref[c, pl.ds(r, rows, stride=d), :] for c in range(n_chunks)], axis=1).astype(ref.dtype)


def _norm_matmul(h, nw, w, tm, tn, dilations=(), seq=None):
    m, d_model = h.shape
    n = w.shape[1]
    out_specs = [pl.BlockSpec((tm, tn), lambda i, j: (i, j))]
    out_shape = [jax.ShapeDtypeStruct((m, n), BF16)]
    scratch = []
    if dilations:
        tiles_per_seq = seq // tm
        for d in dilations:
            out_specs.append(pl.BlockSpec((1, d, tm // d, tn),
                                          lambda i, j: (i // tiles_per_seq, 0, i % tiles_per_seq, j)))
            out_shape.append(jax.ShapeDtypeStruct((m // seq, d, seq // d, n), BF16))
        scratch.append(pltpu.VMEM((tn // LANES, tm, LANES), F32))
    scratch.append(pltpu.VMEM((tm, d_model), BF16))
    res = pl.pallas_call(
        functools.partial(_norm_matmul_kernel, dilations=tuple(dilations)),
        grid=(m // tm, n // tn),
        in_specs=[pl.BlockSpec((tm, d_model), lambda i, j: (i, 0)),
                  pl.BlockSpec((1, d_model), lambda i, j: (0, 0)),
                  pl.BlockSpec((d_model, tn), lambda i, j: (0, j))],
        out_specs=out_specs,
        out_shape=out_shape,
        scratch_shapes=scratch,
        compiler_params=_cparams(("parallel", "arbitrary")),
        name="norm_matmul",
    )(h, nw.reshape(1, d_model), w)
    return res if dilations else res[0]


def _proj_residual_kernel(*refs, n_in):
    h_ref, out_ref = refs[0], refs[-1]
    acc = h_ref[...]
    for o_ref, w_ref in zip(refs[1:1 + n_in], refs[1 + n_in:1 + 2 * n_in]):
        acc = acc + jnp.dot(o_ref[...], w_ref[...], preferred_element_type=F32)
    out_ref[...] = acc


def _proj_residual(h, outs, ws, tm):
    m, d = h.shape
    n_in = len(outs)
    in_specs = [pl.BlockSpec((tm, d), lambda i: (i, 0))]
    in_specs += [pl.BlockSpec((tm, o.shape[1]), lambda i: (i, 0)) for o in outs]
    in_specs += [pl.BlockSpec(w.shape, lambda i: (0, 0)) for w in ws]
    return pl.pallas_call(
        functools.partial(_proj_residual_kernel, n_in=n_in),
        grid=(m // tm,),
        in_specs=in_specs,
        out_specs=pl.BlockSpec((tm, d), lambda i: (i, 0)),
        out_shape=jax.ShapeDtypeStruct((m, d), F32),
        compiler_params=_cparams(("parallel",)),
        name="proj_residual",
    )(h, *outs, *ws)


def _ffn_kernel(h_ref, nw_ref, wg_ref, wu_ref, wd_ref, fw_ref, o_ref, *, final):
    h = h_ref[...]
    hn = _rms(h, nw_ref[...]).astype(BF16)
    g = jnp.dot(hn, wg_ref[...], preferred_element_type=F32)
    u = jnp.dot(hn, wu_ref[...], preferred_element_type=F32)
    a = (jax.nn.silu(g) * u).astype(BF16)
    y = h + jnp.dot(a, wd_ref[...], preferred_element_type=F32)
    if final:
        y = _rms(y, fw_ref[...])
    o_ref[...] = y


def _ffn(h, nw, wg, wu, wd, fw, final, tm):
    m, d = h.shape
    ff = wg.shape[1]
    resident = dict(pipeline_mode=pl.Buffered(1))
    return pl.pallas_call(
        functools.partial(_ffn_kernel, final=final),
        grid=(m // tm,),
        in_specs=[pl.BlockSpec((tm, d), lambda i: (i, 0)),
                  pl.BlockSpec((1, d), lambda i: (0, 0)),
                  pl.BlockSpec((d, ff), lambda i: (0, 0), **resident),
                  pl.BlockSpec((d, ff), lambda i: (0, 0), **resident),
                  pl.BlockSpec((ff, d), lambda i: (0, 0), **resident),
                  pl.BlockSpec((1, d), lambda i: (0, 0))],
        out_specs=pl.BlockSpec((tm, d), lambda i: (i, 0)),
        out_shape=jax.ShapeDtypeStruct((m, d), F32),
        compiler_params=_cparams(("parallel",)),
        name="ffn",
    )(h, nw.reshape(1, d), wg, wu, wd, fw.reshape(1, d))


def _band_bias(table, max_dist, dist_scale, nb):
    kw = (nb + 1) * BLK
    rel = jnp.arange(BLK)[:, None] + nb * BLK - jnp.arange(kw)[None, :]
    band = (rel >= 0) & (rel <= max_dist)
    return jnp.where(band[None], _bias_lookup(table, rel * dist_scale), NEG_INF)


def _swap_halves(x):
    return jnp.concatenate([x[:, HEAD_DIM:], x[:, :HEAD_DIM]], axis=1)


def _banded_kernel(*refs, n_heads, n_groups, nb, has_sinks, want_lse):
    pos = 0
    if has_sinks:
        sink_ref = refs[0]
        pos = 1
    q_ref = refs[pos]
    k_refs = refs[pos + 1:pos + 2 + nb]
    v_refs = refs[pos + 2 + nb:pos + 3 + 2 * nb]
    bias_ref = refs[pos + 3 + 2 * nb]
    o_ref = refs[pos + 4 + 2 * nb]
    lse_ref = refs[pos + 5 + 2 * nb] if want_lse else None

    i = pl.program_id(1)
    n_batch = q_ref.shape[0]
    kw = (nb + 1) * BLK
    hpg = n_heads // n_groups
    lane = lax.broadcasted_iota(jnp.int32, (1, LANES), 1)
    lo = lane < HEAD_DIM
    half_mask = (jnp.where(lo, SCORE_SCALE, 0.0).astype(BF16), jnp.where(lo, 0.0, SCORE_SCALE).astype(BF16))
    col = lax.broadcasted_iota(jnp.int32, (1, kw), 1)
    edge = jnp.where(col < (nb - i) * BLK, NEG_INF, 0.0).astype(F32)

    operands = []
    for bi in range(n_batch):
        kcat = jnp.concatenate([k_refs[nb - jj][bi] for jj in range(nb + 1)], axis=0)
        vcat = jnp.concatenate([v_refs[nb - jj][bi] for jj in range(nb + 1)], axis=0)
        if hpg == 1:
            for p in range(n_heads // 2):
                operands.append((bi, kcat[:, p * LANES:(p + 1) * LANES], vcat[:, p * LANES:(p + 1) * LANES],
                                 [2 * p, 2 * p + 1]))
        else:
            k_sw = _swap_halves(kcat)
            v_sw = _swap_halves(vcat)
            for g in range(n_groups):
                for par in range(2):
                    heads = [h for h in range(g * hpg, (g + 1) * hpg) if h % 2 == par]
                    operands.append((bi, kcat if g == par else k_sw, vcat if g == par else v_sw, heads))

    scores = []
    for bi, kh, _, heads in operands:
        qz = jnp.concatenate([q_ref[bi, :, (h // 2) * LANES:(h // 2 + 1) * LANES] * half_mask[h % 2]
                              for h in heads], axis=0)
        scores.append(lax.dot_general(qz, kh, (((1,), (1,)), ((), ())), preferred_element_type=F32))
    ms, dens, probs = {}, {}, []
    for (bi, _, _, heads), s_all in zip(operands, scores):
        ps = []
        for r, h in enumerate(heads):
            s = s_all[r * BLK:(r + 1) * BLK] + bias_ref[h] + edge
            m = jnp.max(s, axis=-1, keepdims=True)
            if has_sinks:
                m = jnp.maximum(m, sink_ref[h])
            p = jnp.exp(s - m)
            den = jnp.sum(p, axis=-1, keepdims=True)
            if has_sinks:
                den = den + jnp.exp(sink_ref[h] - m)
            ps.append(p.astype(BF16))
            ms[bi, h], dens[bi, h] = m, den
        probs.append(jnp.concatenate(ps, axis=0))
    outs = {}
    for (bi, _, vh, heads), p_all in zip(operands, probs):
        o_all = jnp.dot(p_all, vh, preferred_element_type=F32)
        for r, h in enumerate(heads):
            outs[bi, h] = o_all[r * BLK:(r + 1) * BLK]
    for bi in range(n_batch):
        for pair in range(n_heads // 2):
            h0, h1 = (bi, 2 * pair), (bi, 2 * pair + 1)
            inv = jnp.where(lo, 1.0 / dens[h0], 1.0 / dens[h1])
            o_pair = jnp.where(lo, outs[h0], outs[h1]) * inv
            o_ref[bi, :, pair * LANES:(pair + 1) * LANES] = o_pair.astype(o_ref.dtype)
            if want_lse:
                lse_ref[bi, :, pair * LANES:(pair + 1) * LANES] = jnp.where(
                    lo, ms[h0] + jnp.log(dens[h0]), ms[h1] + jnp.log(dens[h1]))


def _banded_attention(slab, *, q_off, k_off, v_off, n_heads, n_groups,
                      max_dist, table, sinks=None, want_lse=False, out_dtype=BF16):
    b, dil, length, width = slab.shape
    n_blk = length // BLK
    nb = -(-max_dist // BLK)
    kw = (nb + 1) * BLK
    hd = n_heads * HEAD_DIM
    gd = n_groups * HEAD_DIM
    view = slab.reshape(b, dil * length, width)
    bias = _band_bias(table, max_dist, dil, nb)

    def q_map(r, i):
        return (0, r * n_blk + i, q_off // hd)

    def kv_map(off, j):
        return lambda r, i: (0, r * n_blk + jnp.maximum(i - j, 0), off // gd)

    in_specs, args = [], []
    if sinks is not None:
        in_specs.append(pl.BlockSpec(memory_space=pltpu.SMEM))
        args.append(sinks.astype(F32))
    in_specs.append(pl.BlockSpec((b, BLK, hd), q_map))
    args.append(view)
    for off in (k_off, v_off):
        for j in range(nb + 1):
            in_specs.append(pl.BlockSpec((b, BLK, gd), kv_map(off, j)))
            args.append(view)
    in_specs.append(pl.BlockSpec((n_heads, BLK, kw), lambda r, i: (0, 0, 0)))
    args.append(bias)

    out_spec = pl.BlockSpec((b, BLK, hd), lambda r, i: (0, r * n_blk + i, 0))
    out_shape = [jax.ShapeDtypeStruct((b, dil * length, hd), out_dtype)]
    out_specs = [out_spec]
    if want_lse:
        out_shape.append(jax.ShapeDtypeStruct((b, dil * length, hd), F32))
        out_specs.append(out_spec)

    res = pl.pallas_call(
        functools.partial(_banded_kernel, n_heads=n_heads, n_groups=n_groups, nb=nb,
                          has_sinks=sinks is not None, want_lse=want_lse),
        grid=(dil, n_blk),
        in_specs=in_specs,
        out_specs=out_specs,
        out_shape=out_shape,
        compiler_params=_cparams(("parallel", "arbitrary")),
        name="banded_attention",
    )(*args)
    return [r.reshape(b, dil, length, hd) for r in res]


def _dilated_mix_kernel(*refs, n_pat):
    o_refs, l_refs = refs[:n_pat], refs[n_pat:2 * n_pat]
    out_ref = refs[2 * n_pat]
    scratch = refs[2 * n_pat + 1:]

    def natural(ref, buf):
        d = ref.shape[1]
        if d == 1:
            return ref[0, 0]
        rows = ref.shape[2]
        n_chunks = buf.shape[0]
        for r in range(d):
            for c in range(n_chunks):
                buf[c, pl.ds(r, rows, stride=d), :] = ref[0, r, :, c * LANES:(c + 1) * LANES]
        return jnp.concatenate([buf[c] for c in range(n_chunks)], axis=1)

    os_ = [natural(ref, scratch[2 * p]) for p, ref in enumerate(o_refs)]
    ls = [natural(ref, scratch[2 * p + 1]) for p, ref in enumerate(l_refs)]
    m = functools.reduce(jnp.maximum, ls)
    es = [jnp.exp(l - m) for l in ls]
    den = functools.reduce(jnp.add, es)
    acc = functools.reduce(jnp.add, [(e / den) * o for e, o in zip(es, os_)])
    out_ref[...] = acc.astype(out_ref.dtype)


def _dilated_mix(outs, lses, tm):
    b, _, _, c = outs[0].shape
    seq = outs[0].shape[1] * outs[0].shape[2]
    tiles_per_seq = seq // tm

    def spec(a):
        d = a.shape[1]
        return pl.BlockSpec((1, d, tm // d, c), lambda i: (i // tiles_per_seq, 0, i % tiles_per_seq, 0))

    return pl.pallas_call(
        functools.partial(_dilated_mix_kernel, n_pat=len(outs)),
        grid=(b * tiles_per_seq,),
        in_specs=[spec(a) for a in outs] + [spec(a) for a in lses],
        out_specs=pl.BlockSpec((tm, c), lambda i: (i, 0)),
        out_shape=jax.ShapeDtypeStruct((b * seq, c), BF16),
        scratch_shapes=[pltpu.VMEM((c // LANES, tm, LANES), F32) for _ in range(2 * len(outs))],
        compiler_params=_cparams(("parallel",)),
        name="dilated_mix",
    )(*outs, *lses)


def _retention_kernel(q_ref, k_ref, v_ref, g_ref, cos_ref, sin_ref, dmask_ref, qdec_ref, kdec_ref,
                      cdec_ref, gn_ref, o_ref, state_ref):
    @pl.when(pl.program_id(0) == 0)
    def _():
        state_ref[...] = jnp.zeros_like(state_ref)

    cos = cos_ref[...]
    sin = sin_ref[...]
    half = C_QK_DIM // 2

    def rot(x):
        x1, x2 = x[:, :half], x[:, half:]
        return jnp.concatenate([x1 * cos - x2 * sin, x1 * sin + x2 * cos], axis=1)

    for hd in range(C_HEADS):
        qk_cols = slice(hd * C_QK_DIM, (hd + 1) * C_QK_DIM)
        v_cols = slice(hd * C_V_DIM, (hd + 1) * C_V_DIM)
        for bi in range(q_ref.shape[0]):
            q = rot(q_ref[bi, :, qk_cols].astype(F32))
            k = rot(k_ref[bi, :, qk_cols].astype(F32)) * (C_QK_DIM ** -0.5)
            v = v_ref[bi, :, v_cols]
            qb = q.astype(BF16)
            inner = lax.dot_general(qb, k.astype(BF16), (((1,), (1,)), ((), ())),
                                    preferred_element_type=F32) * dmask_ref[hd]
            state = state_ref[bi, hd]
            o = jnp.dot(inner.astype(BF16), v, preferred_element_type=F32)
            o = o + jnp.dot(qb, state.astype(BF16), preferred_element_type=F32) * qdec_ref[hd]
            kd_t = jnp.transpose(k * kdec_ref[hd]).astype(BF16)
            state_ref[bi, hd] = state * cdec_ref[hd] + jnp.dot(kd_t, v, preferred_element_type=F32)

            mu = jnp.mean(o, axis=-1, keepdims=True)
            oc = o - mu
            var = jnp.mean(oc * oc, axis=-1, keepdims=True)
            on = oc * lax.rsqrt(var + 1e-5)
            o_ref[bi, :, v_cols] = (on * gn_ref[:, v_cols]
                                    * jax.nn.silu(g_ref[bi, :, v_cols].astype(F32))).astype(o_ref.dtype)


def _retention(slab, gn, seq):
    b = slab.shape[0]
    n_chunks = seq // C_CHUNK
    half = C_QK_DIM // 2
    pos = jnp.arange(seq, dtype=F32)
    inv = 1.0 / (10000.0 ** (jnp.arange(0, C_QK_DIM, 2, dtype=F32) / C_QK_DIM))
    ang = pos[:, None] * inv[None, :]
    cos, sin = jnp.cos(ang), jnp.sin(ang)
    log_g = jnp.log(1.0 - 2.0 ** (-5.0 - jnp.arange(C_HEADS, dtype=F32)))
    j = jnp.arange(C_CHUNK, dtype=F32)
    diff = j[:, None] - j[None, :]
    dmask = jnp.where(diff >= 0, jnp.exp(diff[None] * log_g[:, None, None]), 0.0)
    q_dec = jnp.exp((j[None, :] + 1.0) * log_g[:, None])[:, :, None]
    k_dec = jnp.exp((C_CHUNK - 1.0 - j[None, :]) * log_g[:, None])[:, :, None]
    chunk_dec = jnp.exp(C_CHUNK * log_g)

    def col(off, w):
        return lambda c: (0, c, off // w)

    const3 = lambda c: (0, 0, 0)
    return pl.pallas_call(
        _retention_kernel,
        grid=(n_chunks,),
        in_specs=[pl.BlockSpec((b, C_CHUNK, C_QK), col(OD_QC, C_QK)),
                  pl.BlockSpec((b, C_CHUNK, C_QK), col(OD_KC, C_QK)),
                  pl.BlockSpec((b, C_CHUNK, C_V), col(OD_VC, C_V)),
                  pl.BlockSpec((b, C_CHUNK, C_V), col(OD_GC, C_V)),
                  pl.BlockSpec((C_CHUNK, half), lambda c: (c, 0)),
                  pl.BlockSpec((C_CHUNK, half), lambda c: (c, 0)),
                  pl.BlockSpec((C_HEADS, C_CHUNK, C_CHUNK), const3),
                  pl.BlockSpec((C_HEADS, C_CHUNK, 1), const3),
                  pl.BlockSpec((C_HEADS, C_CHUNK, 1), const3),
                  pl.BlockSpec(memory_space=pltpu.SMEM),
                  pl.BlockSpec((1, C_V), lambda c: (0, 0))],
        out_specs=pl.BlockSpec((b, C_CHUNK, C_V), lambda c: (0, c, 0)),
        out_shape=jax.ShapeDtypeStruct((b, seq, C_V), BF16),
        scratch_shapes=[pltpu.VMEM((b, C_HEADS, C_QK_DIM, C_V_DIM), F32)],
        compiler_params=_cparams(("arbitrary",)),
        name="retention",
    )(slab, slab, slab, slab, cos, sin, dmask, q_dec, k_dec, chunk_dec, gn.reshape(1, C_V).astype(F32))


def _compress_kernel(x_ref, pos_ref, w1_ref, w2_ref, o_ref):
    x = (x_ref[0].astype(F32) + pos_ref[0]).astype(BF16)
    hid = jax.nn.gelu(jnp.dot(x, w1_ref[0], preferred_element_type=F32))
    o_ref[0] = jnp.dot(hid.astype(BF16), w2_ref[0], preferred_element_type=F32)


def _compress(flat, pos, w1, w2):
    _, bg, rows, width = flat.shape
    tr = min(rows, 256)
    return pl.pallas_call(
        _compress_kernel,
        grid=(2, bg, rows // tr),
        in_specs=[pl.BlockSpec((None, 1, tr, width), lambda s, i, r: (s, i, r, 0)),
                  pl.BlockSpec((1, 1, width), lambda s, i, r: (s, 0, 0)),
                  pl.BlockSpec((1, width, D_CMP_HIDDEN), lambda s, i, r: (s, 0, 0)),
                  pl.BlockSpec((1, D_CMP_HIDDEN, HEAD_DIM), lambda s, i, r: (s, 0, 0))],
        out_specs=pl.BlockSpec((None, 1, tr, HEAD_DIM), lambda s, i, r: (s, i, r, 0)),
        out_shape=jax.ShapeDtypeStruct((2, bg, rows, HEAD_DIM), F32),
        compiler_params=_cparams(("parallel", "parallel", "parallel")),
        name="nsa_compress",
    )(flat, pos, w1, w2)


def _cmp_bias(table):
    tt = jnp.arange(BLK)[:, None]
    m = jnp.arange(CMP_WIN)[None, :] + (BLK // D_CMP_STRIDE) - CMP_WIN
    dist = tt - D_CMP_STRIDE * m - (D_CMP_LEN - 1)
    return jnp.where((dist >= 0)[None], _bias_lookup(table, dist), NEG_INF)


def _sel_matrix():
    c_rel = np.arange(CMP_WIN)[:, None] + (BLK // D_CMP_STRIDE) - CMP_WIN
    j_rel = np.arange(REL_BLOCKS)[None, :] - (REL_BLOCKS - 2)
    return ((c_rel >= 4 * j_rel - 1) & (c_rel <= 4 * j_rel + 3)).astype(np.float32)


def _cmp_attn_kernel(q_ref, kc_ref, vc_ref, bias_ref, sel_ref, o_ref, idx_ref):
    i = pl.program_id(1)
    lane = lax.broadcasted_iota(jnp.int32, (1, LANES), 1)
    lo = lane < HEAD_DIM
    half_mask = (jnp.where(lo, SCORE_SCALE, 0.0).astype(BF16), jnp.where(lo, 0.0, SCORE_SCALE).astype(BF16))
    row = lax.broadcasted_iota(jnp.int32, (BLK, REL_BLOCKS), 0)
    jj = lax.broadcasted_iota(jnp.int32, (BLK, REL_BLOCKS), 1)
    cur = (REL_BLOCKS - 2) + (row >= D_SEL_LEN).astype(jnp.int32)
    first = (REL_BLOCKS - 2) - 2 * i
    exists = jj >= first
    forced = exists & ((jj == first) | (jj == cur) | (jj == cur - 1))
    valid = exists & (jj <= cur)
    jjf = jj.astype(F32)
    out_lane = lax.broadcasted_iota(jnp.int32, (BLK, LANES), 1)
    out_row = lax.broadcasted_iota(jnp.int32, (BLK, LANES), 0)
    cur_abs = 2 * i + (out_row >= D_SEL_LEN).astype(jnp.int32)
    tt_in_blk = out_row % D_SEL_LEN

    for width in CMP_WIDTHS:
        lo_w = width - CMP_WIDTHS[0] if width > CMP_WIDTHS[0] else -1

        @pl.when((8 * i + 8 <= width) & (8 * i + 8 > lo_w))
        def _(width=width):
            _cmp_attn_body(q_ref, kc_ref, vc_ref, bias_ref, sel_ref, o_ref, idx_ref, i, width, lo, half_mask,
                           forced, valid, jjf, first, out_lane, cur_abs, tt_in_blk)


def _cmp_attn_body(q_ref, kc_ref, vc_ref, bias_ref, sel_ref, o_ref, idx_ref, i, width, lo, half_mask,
                   forced, valid, jjf, first, out_lane, cur_abs, tt_in_blk):
    hpg = D_HEADS // D_KV_HEADS
    off = CMP_WIN - width
    start = pl.multiple_of(8 * i + 8 + off, 8)
    ucol = lax.broadcasted_iota(jnp.int32, (1, width), 1) + off
    edge = jnp.where(ucol < CMP_WIN - 8 - 8 * i, NEG_INF, 0.0).astype(F32)

    for g in range(D_KV_HEADS):
        kwin = kc_ref[0, g, pl.ds(start, width), :].astype(BF16)
        vwin = vc_ref[0, g, pl.ds(start, width), :].astype(BF16)
        qs = []
        for hh in range(hpg):
            h = g * hpg + hh
            qs.append(q_ref[0, :, (h // 2) * LANES:(h // 2 + 1) * LANES] * half_mask[h % 2])
        s_all = lax.dot_general(jnp.concatenate(qs, axis=0), kwin, (((1,), (1,)), ((), ())),
                                preferred_element_type=F32)
        imp = jnp.zeros((BLK, width), F32)
        pns = []
        for hh in range(hpg):
            h = g * hpg + hh
            s = s_all[hh * BLK:(hh + 1) * BLK] + bias_ref[h, :, off:] + edge
            m = jnp.maximum(jnp.max(s, axis=-1, keepdims=True), 0.1 * NEG_INF)
            p = jnp.exp(s - m)
            den = jnp.maximum(jnp.sum(p, axis=-1, keepdims=True), 1e-30)
            pn = p * (1.0 / den)
            imp = imp + pn
            pns.append(pn.astype(BF16))
        o_all = jnp.dot(jnp.concatenate(pns, axis=0), vwin, preferred_element_type=F32)
        for pr in range(hpg // 2):
            pair = (g * hpg) // 2 + pr
            o_ref[0, :, pair * LANES:(pair + 1) * LANES] = jnp.where(
                lo, o_all[2 * pr * BLK:(2 * pr + 1) * BLK], o_all[(2 * pr + 1) * BLK:(2 * pr + 2) * BLK]
            ).astype(o_ref.dtype)

        sel = sel_ref[off:, :]
        hi = imp.astype(BF16)
        r1 = imp - hi.astype(F32)
        mid = r1.astype(BF16)
        low = (r1 - mid.astype(F32)).astype(BF16)
        imp_sel = (jnp.dot(hi, sel, preferred_element_type=F32)
                   + jnp.dot(mid, sel, preferred_element_type=F32)
                   + jnp.dot(low, sel, preferred_element_type=F32))
        score = jnp.where(forced, 1e9, jnp.where(valid, imp_sel, -1e9))
        picked = jnp.zeros((BLK, LANES), jnp.int32)
        for r in range(D_SEL_COUNT):
            m = jnp.max(score, axis=-1, keepdims=True)
            am = jnp.min(jnp.where(score == m, jjf, float(REL_BLOCKS)), axis=-1, keepdims=True)
            none = m < -5e8
            blk = am.astype(jnp.int32) - first
            key_blk = jnp.where(none, 0, blk)
            bias_row = jnp.where(none, SEL_NONE, jnp.minimum(cur_abs - blk, SEL_FAR)) * D_SEL_LEN + tt_in_blk
            picked = jnp.where(out_lane == r, key_blk, picked)
            picked = jnp.where(out_lane == D_SEL_COUNT + r, bias_row, picked)
            score = jnp.where(jjf == am, -jnp.inf, score)
        idx_ref[0, g] = picked


def _cmp_attention(slab, kc_pad, vc_pad, table, seq):
    b = slab.shape[0]
    n_blk = seq // BLK
    rows = kc_pad.shape[2]
    bias = _cmp_bias(table)
    sel = jnp.asarray(_sel_matrix(), BF16)
    return pl.pallas_call(
        _cmp_attn_kernel,
        grid=(b, n_blk),
        in_specs=[pl.BlockSpec((1, BLK, D_Q), lambda bi, i: (bi, i, OD_QD // D_Q)),
                  pl.BlockSpec((1, D_KV_HEADS, rows, LANES), lambda bi, i: (bi, 0, 0, 0)),
                  pl.BlockSpec((1, D_KV_HEADS, rows, LANES), lambda bi, i: (bi, 0, 0, 0)),
                  pl.BlockSpec((D_HEADS, BLK, CMP_WIN), lambda bi, i: (0, 0, 0)),
                  pl.BlockSpec((CMP_WIN, REL_BLOCKS), lambda bi, i: (0, 0))],
        out_specs=[pl.BlockSpec((1, BLK, D_Q), lambda bi, i: (bi, i, 0)),
                   pl.BlockSpec((1, D_KV_HEADS, BLK, LANES), lambda bi, i: (bi, 0, i, 0))],
        out_shape=[jax.ShapeDtypeStruct((b, seq, D_Q), BF16),
                   jax.ShapeDtypeStruct((b, D_KV_HEADS, seq, LANES), jnp.int32)],
        compiler_params=_cparams(("parallel", "arbitrary")),
        name="nsa_cmp_attention",
    )(slab, kc_pad, vc_pad, bias, sel)


def _sel_bias(table):
    hpg = D_HEADS // D_KV_HEADS
    delta = jnp.arange(SEL_FAR)[:, None, None]
    tt = jnp.arange(D_SEL_LEN)[None, :, None]
    l = jnp.arange(D_SEL_LEN)[None, None, :]
    dist = D_SEL_LEN * delta + tt - l
    near = jnp.where((dist >= 0)[None], _bias_lookup(table, dist), NEG_INF)
    far = jnp.broadcast_to(table.astype(F32)[REL_BUCKETS - 1][:, None, None, None],
                           (D_HEADS, 1, D_SEL_LEN, D_SEL_LEN))
    none = jnp.full((D_HEADS, 1, D_SEL_LEN, D_SEL_LEN), NEG_INF, F32)
    rows = jnp.concatenate([near, far, none], axis=1)
    half = D_SEL_LEN // 2
    rows = rows.reshape(D_KV_HEADS, hpg, SEL_NONE + 1, D_SEL_LEN, half, 2)
    rows = jnp.transpose(rows, (0, 2, 3, 5, 1, 4))
    rows = rows.reshape(D_KV_HEADS, (SEL_NONE + 1) * D_SEL_LEN, 2 * hpg, half)
    return jnp.tile(rows, (1, 1, 1, LANES // half))


SEL_IDX = 2 * D_SEL_COUNT
SEL_UNROLL = 16


def _sel_attn_kernel(idx_hbm, q_ref, kv_ref, bias_ref, o_ref, idx_smem, sem):
    n_g, n_i = pl.num_programs(1), pl.num_programs(2)
    step = (pl.program_id(0) * n_g + pl.program_id(1)) * n_i + pl.program_id(2)
    total = pl.num_programs(0) * n_g * n_i
    slot = step % 2

    tile_words = BLK * SEL_IDX

    def idx_copy(s, sl):
        dst = idx_smem.at[pl.ds(pl.multiple_of(sl * tile_words, tile_words), tile_words)]
        return pltpu.make_async_copy(idx_hbm.at[s], dst, sem.at[sl])

    @pl.when(step == 0)
    def _():
        idx_copy(0, 0).start()

    @pl.when(step + 1 < total)
    def _():
        idx_copy(step + 1, 1 - slot).start()

    idx_copy(step, slot).wait()

    lane = lax.broadcasted_iota(jnp.int32, (1, LANES), 1)
    lo = lane < HEAD_DIM
    hpg = D_HEADS // D_KV_HEADS
    quarter = D_SEL_LEN // 2
    def token_scores(tl):
        picks = idx_smem.at[pl.ds(slot * tile_words + tl * SEL_IDX, SEL_IDX)]
        qq = q_ref[0, 0, tl] * SCORE_SCALE
        qbd = jnp.concatenate([jnp.where(lo, qq, 0.0), jnp.where(lo, 0.0, qq)], axis=0).astype(BF16)
        ks, vs, bs = [], [], []
        for n in range(D_SEL_COUNT):
            kv = kv_ref[0, 0, picks[n]]
            ks.append(kv[:quarter])
            vs.append(kv[quarter:])
            bs.append(bias_ref[0, picks[D_SEL_COUNT + n]])
        s = lax.dot_general(qbd, jnp.concatenate(ks, axis=0), (((1,), (1,)), ((), ())),
                            preferred_element_type=F32)
        bias = jnp.concatenate(
            [jnp.where(lane < quarter, bs[c],
                       jnp.where(lane < 2 * quarter, bs[c + 1],
                                 jnp.where(lane < 3 * quarter, bs[c + 2], bs[c + 3])))
             for c in range(0, D_SEL_COUNT, 4)], axis=1)
        return s + bias, jnp.concatenate(vs, axis=0)

    def fold(x16):
        return x16[:hpg], x16[hpg:]

    def body(it, carry):
        t0 = it * SEL_UNROLL
        sv = [token_scores(t0 + u) for u in range(SEL_UNROLL)]
        ps, dens = [], []
        for s, _ in sv:
            m = jnp.maximum(*fold(jnp.max(s, axis=-1, keepdims=True)))
            p = jnp.exp(s - jnp.concatenate([m, m], axis=0))
            dens.append(sum(fold(jnp.sum(p, axis=-1, keepdims=True))))
            ps.append(p.astype(BF16))
        outs = []
        for p, den, (_, vall) in zip(ps, dens, sv):
            o_top, o_bot = fold(jnp.dot(p, vall, preferred_element_type=F32))
            outs.append(jnp.where(lo, o_top, o_bot) * (1.0 / den))
        for u in range(SEL_UNROLL):
            o_ref[0, 0, t0 + u] = outs[u]
        return carry

    lax.fori_loop(0, BLK // SEL_UNROLL, body, 0)


def _sel_attention(idx, q_sel, kv_sel, table, seq):
    b = q_sel.shape[0]
    hpg = D_HEADS // D_KV_HEADS
    n_blk = seq // BLK
    n_sb = seq // D_SEL_LEN
    bias = _sel_bias(table)
    kv_spec = pl.BlockSpec((1, 1, n_sb, D_SEL_LEN, LANES), lambda bi, g, i: (bi, g, 0, 0, 0))
    return pl.pallas_call(
        _sel_attn_kernel,
        grid=(b, D_KV_HEADS, n_blk),
        in_specs=[pl.BlockSpec(memory_space=pl.ANY),
                  pl.BlockSpec((1, 1, BLK, hpg, LANES), lambda bi, g, i: (bi, g, i, 0, 0)),
                  kv_spec,
                  pl.BlockSpec((1, (SEL_NONE + 1) * D_SEL_LEN, 2 * hpg, LANES), lambda bi, g, i: (g, 0, 0, 0))],
        out_specs=pl.BlockSpec((1, 1, BLK, hpg, LANES), lambda bi, g, i: (bi, g, i, 0, 0)),
        out_shape=jax.ShapeDtypeStruct((b, D_KV_HEADS, seq, hpg, LANES), F32),
        scratch_shapes=[pltpu.SMEM((2 * BLK * SEL_IDX,), jnp.int32), pltpu.SemaphoreType.DMA((2,))],
        compiler_params=_cparams(("arbitrary", "arbitrary", "arbitrary")),
        name="nsa_sel_attention",
    )(idx, q_sel, kv_sel, bias)


def _gate_expand():
    e = np.zeros((3, LANES, D_Q), np.float32)
    for h in range(D_HEADS):
        for c in range(3):
            e[c, 3 * h + c, h * HEAD_DIM:(h + 1) * HEAD_DIM] = 1.0
    return e


def _head_place():
    hpg = D_HEADS // D_KV_HEADS
    p = np.zeros((hpg * LANES, hpg * HEAD_DIM), np.float32)
    for hh in range(hpg):
        for lane in range(LANES):
            p[hh * LANES + lane, hh * HEAD_DIM + lane % HEAD_DIM] = 1.0
    return p


def _nsa_gate_kernel(gd_ref, e_ref, place_ref, oc_ref, os_ref, ow_ref, out_ref):
    hpg = D_HEADS // D_KV_HEADS
    tm = out_ref.shape[0]
    groups = []
    for g in range(D_KV_HEADS):
        rows = jnp.concatenate([os_ref[0, g, pl.ds(hh, tm, stride=hpg), :].astype(BF16) for hh in range(hpg)],
                               axis=1)
        groups.append(jnp.dot(rows, place_ref[...], preferred_element_type=F32))
    o_s = jnp.concatenate(groups, axis=1)

    sg = jax.nn.sigmoid(gd_ref[...].astype(F32))
    hi = sg.astype(BF16)
    low = (sg - hi.astype(F32)).astype(BF16)
    acc = None
    for c, branch in enumerate((oc_ref[...].astype(F32), o_s, ow_ref[...].astype(F32))):
        gate = (jnp.dot(hi, e_ref[c], preferred_element_type=F32)
                + jnp.dot(low, e_ref[c], preferred_element_type=F32))
        acc = gate * branch if acc is None else acc + gate * branch
    out_ref[...] = acc.astype(out_ref.dtype)


def _nsa_gate(slab2d, o_c, o_s_raw, o_w, tm):
    m = slab2d.shape[0]
    b, g_kv, seq, hpg, _ = o_s_raw.shape
    tiles_per_seq = seq // tm
    e = jnp.asarray(_gate_expand(), BF16)
    place = jnp.asarray(_head_place(), BF16)
    spec = pl.BlockSpec((tm, D_Q), lambda i: (i, 0))
    return pl.pallas_call(
        _nsa_gate_kernel,
        grid=(m // tm,),
        in_specs=[pl.BlockSpec((tm, LANES), lambda i: (i, OD_GD // LANES)),
                  pl.BlockSpec((3, LANES, D_Q), lambda i: (0, 0, 0)),
                  pl.BlockSpec((hpg * LANES, hpg * HEAD_DIM), lambda i: (0, 0)),
                  spec,
                  pl.BlockSpec((1, g_kv, tm * hpg, LANES),
                               lambda i: (i // tiles_per_seq, 0, i % tiles_per_seq, 0)),
                  spec],
        out_specs=spec,
        out_shape=jax.ShapeDtypeStruct((m, D_Q), BF16),
        compiler_params=_cparams(("parallel",)),
        name="nsa_gate",
    )(slab2d, e, place, o_c, o_s_raw.reshape(b, g_kv, seq * hpg, LANES), o_w)


def _even_mixer(h, nw, w_in, sinks, w_out, rel_table, b, seq):
    qa, ka, va, qb, kb, vb = jnp.split(w_in, [int(c) for c in np.cumsum([A_Q, A_KV, A_KV, B_W, B_W])], axis=1)
    pad = jnp.zeros((D_MODEL, EVEN_SLAB - w_in.shape[1]), w_in.dtype)
    w_slab = jnp.concatenate([qa, qb, kb, vb, ka, va, pad], axis=1).astype(BF16)
    dilations = [dil for _, dil in B_PATTERNS if dil > 1]
    slabs = _norm_matmul(h, nw, w_slab, 1024, EVEN_SLAB // 2, dilations, seq)
    by_dil = {1: slabs[0].reshape(b, 1, seq, EVEN_SLAB)}
    by_dil.update(zip(dilations, slabs[1:]))

    (oa,) = _banded_attention(by_dil[1], q_off=EV_QA, k_off=EV_KA, v_off=EV_VA,
                              n_heads=A_HEADS, n_groups=A_KV_HEADS, max_dist=A_WINDOW - 1,
                              table=rel_table[:, :A_HEADS], sinks=sinks)
    outs, lses = [], []
    for window, dil in B_PATTERNS:
        o, lse = _banded_attention(by_dil[dil], q_off=EV_QB, k_off=EV_KB, v_off=EV_VB,
                                   n_heads=B_HEADS, n_groups=B_HEADS, max_dist=window // dil,
                                   table=rel_table[:, A_HEADS:A_HEADS + B_HEADS], want_lse=True, out_dtype=F32)
        outs.append(o)
        lses.append(lse)
    ob = _dilated_mix(outs, lses, 1024)
    w_out = w_out.astype(BF16)
    return _proj_residual(h, [oa.reshape(b * seq, A_Q), ob], [w_out[:A_Q], w_out[A_Q:]], 512)


def _nsa(slab, pos_k, pos_v, k_w1, k_w2, v_w1, v_w2, rel_table, b, seq):
    g_kv = D_KV_HEADS
    hpg = D_HEADS // g_kv
    slab2d = slab.reshape(b * seq, ODD_SLAB)
    n_rows = seq // D_CMP_STRIDE

    def rows16(off):
        a = slab[:, :, off:off + D_KV].reshape(b, n_rows, D_CMP_STRIDE, g_kv, HEAD_DIM)
        a = jnp.transpose(a, (0, 3, 1, 2, 4)).reshape(b * g_kv, n_rows, D_CMP_STRIDE * HEAD_DIM)
        nxt = jnp.concatenate([a[:, 1:], jnp.zeros_like(a[:, :1])], axis=1)
        return jnp.concatenate([a, nxt], axis=-1)

    flat = jnp.stack([rows16(OD_KCMP), rows16(OD_VCMP)])
    pos = jnp.stack([pos_k.reshape(1, -1), pos_v.reshape(1, -1)]).astype(F32)
    w1 = jnp.stack([k_w1, v_w1]).astype(BF16)
    w2 = jnp.stack([k_w2, v_w2]).astype(BF16)
    cmp = _compress(flat, pos, w1, w2).reshape(2, b, g_kv, n_rows, HEAD_DIM)
    cmp = jnp.pad(cmp, ((0, 0), (0, 0), (0, 0), (CMP_WIN, 0), (0, 0)))
    cmp = jnp.concatenate([cmp, cmp], axis=-1)
    o_c, idx = _cmp_attention(slab, cmp[0], cmp[1], rel_table, seq)

    idx = idx[..., :SEL_IDX].reshape(b * g_kv * (seq // BLK), BLK * SEL_IDX)

    def per_group(off):
        return slab[:, :, off:off + D_KV].reshape(b, seq, g_kv, HEAD_DIM)

    def two_per_row(off):
        a = jnp.transpose(per_group(off), (0, 2, 1, 3))
        return a.reshape(b, g_kv, seq // D_SEL_LEN, D_SEL_LEN // 2, LANES)

    q_sel = slab[:, :, OD_QD:OD_QD + D_Q].reshape(b, seq, g_kv, hpg, HEAD_DIM)
    q_sel = jnp.transpose(q_sel, (0, 2, 1, 3, 4)).astype(F32)
    q_sel = jnp.concatenate([q_sel, q_sel], axis=-1)
    kv_sel = jnp.concatenate([two_per_row(OD_KSLC), two_per_row(OD_VSLC)], axis=3)
    o_s = _sel_attention(idx, q_sel, kv_sel, rel_table, seq)

    (o_w,) = _banded_attention(slab.reshape(b, 1, seq, ODD_SLAB), q_off=OD_QD, k_off=OD_KWIN, v_off=OD_VWIN,
                               n_heads=D_HEADS, n_groups=D_KV_HEADS, max_dist=D_WINDOW - 1, table=rel_table)
    return _nsa_gate(slab2d, o_c.reshape(b * seq, D_Q), o_s, o_w.reshape(b * seq, D_Q), 512)


def _odd_mixer(h, nw, w_in, ret_gn, pos_k, pos_v, k_w1, k_w2, v_w1, v_w2, w_out, rel_table, b, seq):
    pad = jnp.zeros((D_MODEL, ODD_SLAB - ODD_IN), w_in.dtype)
    w_slab = jnp.concatenate([w_in, pad], axis=1).astype(BF16)
    slab = _norm_matmul(h, nw, w_slab, 1024, ODD_SLAB // 4).reshape(b, seq, ODD_SLAB)
    oc = _retention(slab, ret_gn, seq)
    od = _nsa(slab, pos_k, pos_v, k_w1, k_w2, v_w1, v_w2, rel_table, b, seq)
    w_out = w_out.astype(BF16)
    return _proj_residual(h, [oc.reshape(b * seq, C_V), od], [w_out[:C_V], w_out[C_V:]], 512)


def kernel(x, rel_table, norm_mix, norm_ffn, norm_final, even_w_in, even_sinks, even_w_out, odd_w_in, odd_ret_gn, odd_cmp_pos_k, odd_cmp_pos_v, odd_cmp_k_w1, odd_cmp_k_w2, odd_cmp_v_w1, odd_cmp_v_w2, odd_w_out, ffn_w_gate, ffn_w_up, ffn_w_down):
    b, seq, d = x.shape
    h = x.reshape(b * seq, d)
    for layer in range(DEPTH):
        li = layer // 2
        if layer % 2 == 0:
            h = _even_mixer(h, norm_mix[layer], even_w_in[li], even_sinks[li], even_w_out[li], rel_table, b, seq)
        else:
            h = _odd_mixer(h, norm_mix[layer], odd_w_in[li], odd_ret_gn[li], odd_cmp_pos_k[li],
                           odd_cmp_pos_v[li], odd_cmp_k_w1[li], odd_cmp_k_w2[li], odd_cmp_v_w1[li],
                           odd_cmp_v_w2[li], odd_w_out[li], rel_table, b, seq)
        h = _ffn(h, norm_ffn[layer], ffn_w_gate[layer].astype(BF16), ffn_w_up[layer].astype(BF16),
                 ffn_w_down[layer].astype(BF16), norm_final, layer == DEPTH - 1, 512)
    return h.reshape(b, seq, d)
```

```python
import functools
import math

import numpy as np
import jax
import jax.numpy as jnp
from jax import lax
from jax.experimental import pallas as pl
from jax.experimental.pallas import tpu as pltpu

F32 = jnp.float32
BF16 = jnp.bfloat16

D_MODEL = 1024
DEPTH = 4
HEAD_DIM = 64
BLK = 128
NEG_INF = -1e30
REL_BUCKETS = 32
REL_MAX_DIST = 2048
A_HEADS = 8
A_KV_HEADS = 2
A_WINDOW = 128
B_HEADS = 8
B_PATTERNS = ((128, 1), (512, 4), (2048, 16))
C_HEADS = 4
C_QK_DIM = 256
C_V_DIM = 512
C_CHUNK = 128
D_HEADS = 16
D_KV_HEADS = 2
D_CMP_LEN = 32
D_CMP_STRIDE = 16
D_CMP_HIDDEN = 128
D_SEL_LEN = 64
D_SEL_COUNT = 16
D_WINDOW = 512
D_FF = 2816

A_Q = A_HEADS * HEAD_DIM
A_KV = A_KV_HEADS * HEAD_DIM
B_W = B_HEADS * HEAD_DIM
C_QK = C_HEADS * C_QK_DIM
C_V = C_HEADS * C_V_DIM
D_Q = D_HEADS * HEAD_DIM
D_KV = D_KV_HEADS * HEAD_DIM
ODD_IN = 2 * C_QK + 2 * C_V + D_Q + 6 * D_KV + 3 * D_HEADS

SCORE_SCALE = HEAD_DIM ** -0.5
LANES = 128
VMEM_LIMIT = 56 * 1024 * 1024

EVEN_SLAB = 2560
EV_QA, EV_QB, EV_KB, EV_VB, EV_KA, EV_VA = 0, 512, 1024, 1536, 2048, 2176
ODD_SLAB = 8192
OD_QC, OD_KC, OD_VC, OD_GC, OD_QD = 0, 1024, 2048, 4096, 6144
OD_KCMP, OD_VCMP, OD_KSLC, OD_VSLC, OD_KWIN, OD_VWIN, OD_GD = 7168, 7296, 7424, 7552, 7680, 7808, 7936

TM_IN_PROJ = 1024
TN_EVEN = EVEN_SLAB // 2
TN_ODD = ODD_SLAB // 4
TM_OUT_PROJ = 512
TM_FFN = 512
TM_MIX = 1024
TM_GATE = 512

FORCED_SCORE = 1e9
INVALID_SCORE = -1e9
ROW_MAX_FLOOR = 0.1 * NEG_INF
CMP_PER_BLK = BLK // D_CMP_STRIDE

SEL_FAR = 25
SEL_NONE = 26
CMP_WIN = 1024
CMP_WIDTHS = (256, 512, 768, 1024)
REL_BLOCKS = 256


def _cparams(sem):
    return pltpu.CompilerParams(dimension_semantics=sem, vmem_limit_bytes=VMEM_LIMIT)


def _t5_bucket(dist):
    max_exact = REL_BUCKETS // 2
    d = jnp.maximum(dist, 0)
    df = jnp.maximum(d, 1).astype(jnp.float32)
    large = max_exact + (jnp.log(df / max_exact) / math.log(REL_MAX_DIST / max_exact)
                         * (REL_BUCKETS - max_exact)).astype(jnp.int32)
    large = jnp.minimum(large, REL_BUCKETS - 1)
    return jnp.where(d < max_exact, d, large)


def _bias_lookup(table, dist):
    bucket = _t5_bucket(dist)[None]
    tab = table.astype(F32)
    expand = (slice(None),) + (None,) * dist.ndim
    out = jnp.zeros((tab.shape[1],) + dist.shape, F32)
    for b in range(REL_BUCKETS):
        out = jnp.where(bucket == b, tab[b][expand], out)
    return out


def _rms(x, w, eps=1e-6):
    return x * lax.rsqrt(jnp.mean(x * x, axis=-1, keepdims=True) + eps) * w


def _norm_matmul_kernel(h_ref, nw_ref, w_ref, o_ref, *rest, dilations):
    @pl.when(pl.program_id(1) == 0)
    def _():
        rest[-1][...] = _rms(h_ref[...], nw_ref[...]).astype(BF16)

    acc = jnp.dot(rest[-1][...], w_ref[...], preferred_element_type=F32)
    o_ref[...] = acc.astype(o_ref.dtype)
    if dilations:
        acc_ref = rest[-2]
        n_chunks = acc_ref.shape[0]
        for c in range(n_chunks):
            acc_ref[c] = acc[:, c * LANES:(c + 1) * LANES]
        for d, ref in zip(dilations, rest):
            rows = acc_ref.shape[1] // d
            for r in range(d):
                ref[0, r] = jnp.concatenate(
                    [acc_ref[c, pl.ds(r, rows, stride=d), :] for c in range(n_chunks)], axis=1).astype(ref.dtype)


def _norm_matmul(h, nw, w, tm, tn, dilations=(), seq=None):
    m, d_model = h.shape
    n = w.shape[1]
    out_specs = [pl.BlockSpec((tm, tn), lambda i, j: (i, j))]
    out_shape = [jax.ShapeDtypeStruct((m, n), BF16)]
    scratch = []
    if dilations:
        tiles_per_seq = seq // tm
        for d in dilations:
            out_specs.append(pl.BlockSpec((1, d, tm // d, tn),
                                          lambda i, j: (i // tiles_per_seq, 0, i % tiles_per_seq, j)))
            out_shape.append(jax.ShapeDtypeStruct((m // seq, d, seq // d, n), BF16))
        scratch.append(pltpu.VMEM((tn // LANES, tm, LANES), F32))
    scratch.append(pltpu.VMEM((tm, d_model), BF16))
    res = pl.pallas_call(
        functools.partial(_norm_matmul_kernel, dilations=tuple(dilations)),
        grid=(m // tm, n // tn),
        in_specs=[pl.BlockSpec((tm, d_model), lambda i, j: (i, 0)),
                  pl.BlockSpec((1, d_model), lambda i, j: (0, 0)),
                  pl.BlockSpec((d_model, tn), lambda i, j: (0, j))],
        out_specs=out_specs,
        out_shape=out_shape,
        scratch_shapes=scratch,
        compiler_params=_cparams(("parallel", "arbitrary")),
        name="norm_matmul",
    )(h, nw.reshape(1, d_model), w)
    return res if dilations else res[0]


def _proj_residual_kernel(*refs, n_in):
    h_ref, out_ref = refs[0], refs[-1]
    acc = h_ref[...]
    for o_ref, w_ref in zip(refs[1:1 + n_in], refs[1 + n_in:1 + 2 * n_in]):
        acc = acc + jnp.dot(o_ref[...], w_ref[...], preferred_element_type=F32)
    out_ref[...] = acc


def _proj_residual(h, outs, ws, tm):
    m, d = h.shape
    n_in = len(outs)
    in_specs = [pl.BlockSpec((tm, d), lambda i: (i, 0))]
    in_specs += [pl.BlockSpec((tm, o.shape[1]), lambda i: (i, 0)) for o in outs]
    in_specs += [pl.BlockSpec(w.shape, lambda i: (0, 0)) for w in ws]
    return pl.pallas_call(
        functools.partial(_proj_residual_kernel, n_in=n_in),
        grid=(m // tm,),
        in_specs=in_specs,
        out_specs=pl.BlockSpec((tm, d), lambda i: (i, 0)),
        out_shape=jax.ShapeDtypeStruct((m, d), F32),
        compiler_params=_cparams(("parallel",)),
        name="proj_residual",
    )(h, *outs, *ws)


def _ffn_kernel(h_ref, nw_ref, wg_ref, wu_ref, wd_ref, fw_ref, o_ref, *, final):
    h = h_ref[...]
    hn = _rms(h, nw_ref[...]).astype(BF16)
    g = jnp.dot(hn, wg_ref[...], preferred_element_type=F32)
    u = jnp.dot(hn, wu_ref[...], preferred_element_type=F32)
    a = (jax.nn.silu(g) * u).astype(BF16)
    y = h + jnp.dot(a, wd_ref[...], preferred_element_type=F32)
    if final:
        y = _rms(y, fw_ref[...])
    o_ref[...] = y


def _ffn(h, nw, wg, wu, wd, fw, final, tm):
    m, d = h.shape
    ff = wg.shape[1]
    resident = dict(pipeline_mode=pl.Buffered(1))
    return pl.pallas_call(
        functools.partial(_ffn_kernel, final=final),
        grid=(m // tm,),
        in_specs=[pl.BlockSpec((tm, d), lambda i: (i, 0)),
                  pl.BlockSpec((1, d), lambda i: (0, 0)),
                  pl.BlockSpec((d, ff), lambda i: (0, 0), **resident),
                  pl.BlockSpec((d, ff), lambda i: (0, 0), **resident),
                  pl.BlockSpec((ff, d), lambda i: (0, 0), **resident),
                  pl.BlockSpec((1, d), lambda i: (0, 0))],
        out_specs=pl.BlockSpec((tm, d), lambda i: (i, 0)),
        out_shape=jax.ShapeDtypeStruct((m, d), F32),
        compiler_params=_cparams(("parallel",)),
        name="ffn",
    )(h, nw.reshape(1, d), wg, wu, wd, fw.reshape(1, d))


def _band_bias(table, max_dist, dist_scale, nb):
    kw = (nb + 1) * BLK
    rel = jnp.arange(BLK)[:, None] + nb * BLK - jnp.arange(kw)[None, :]
    band = (rel >= 0) & (rel <= max_dist)
    return jnp.where(band[None], _bias_lookup(table, rel * dist_scale), NEG_INF)


def _swap_halves(x):
    return jnp.concatenate([x[:, HEAD_DIM:], x[:, :HEAD_DIM]], axis=1)


def _banded_kernel(*refs, n_heads, n_groups, nb, has_sinks, want_lse):
    pos = 0
    if has_sinks:
        sink_ref = refs[0]
        pos = 1
    q_ref = refs[pos]
    k_refs = refs[pos + 1:pos + 2 + nb]
    v_refs = refs[pos + 2 + nb:pos + 3 + 2 * nb]
    bias_ref = refs[pos + 3 + 2 * nb]
    o_ref = refs[pos + 4 + 2 * nb]
    lse_ref = refs[pos + 5 + 2 * nb] if want_lse else None

    i = pl.program_id(1)
    n_batch = q_ref.shape[0]
    kw = (nb + 1) * BLK
    hpg = n_heads // n_groups
    lane = lax.broadcasted_iota(jnp.int32, (1, LANES), 1)
    lo = lane < HEAD_DIM
    half_mask = (jnp.where(lo, SCORE_SCALE, 0.0).astype(BF16), jnp.where(lo, 0.0, SCORE_SCALE).astype(BF16))
    col = lax.broadcasted_iota(jnp.int32, (1, kw), 1)
    edge = jnp.where(col < (nb - i) * BLK, NEG_INF, 0.0).astype(F32)

    operands = []
    for bi in range(n_batch):
        kcat = jnp.concatenate([k_refs[nb - jj][bi] for jj in range(nb + 1)], axis=0)
        vcat = jnp.concatenate([v_refs[nb - jj][bi] for jj in range(nb + 1)], axis=0)
        if hpg == 1:
            for p in range(n_heads // 2):
                operands.append((bi, kcat[:, p * LANES:(p + 1) * LANES], vcat[:, p * LANES:(p + 1) * LANES],
                                 [2 * p, 2 * p + 1]))
        else:
            k_sw = _swap_halves(kcat)
            v_sw = _swap_halves(vcat)
            for g in range(n_groups):
                for par in range(2):
                    heads = [h for h in range(g * hpg, (g + 1) * hpg) if h % 2 == par]
                    operands.append((bi, kcat if g == par else k_sw, vcat if g == par else v_sw, heads))

    scores = []
    for bi, kh, _, heads in operands:
        qz = jnp.concatenate([q_ref[bi, :, (h // 2) * LANES:(h // 2 + 1) * LANES] * half_mask[h % 2]
                              for h in heads], axis=0)
        scores.append(lax.dot_general(qz, kh, (((1,), (1,)), ((), ())), preferred_element_type=F32))
    ms, dens, probs = {}, {}, []
    for (bi, _, _, heads), s_all in zip(operands, scores):
        ps = []
        for r, h in enumerate(heads):
            s = s_all[r * BLK:(r + 1) * BLK] + bias_ref[h] + edge
            m = jnp.max(s, axis=-1, keepdims=True)
            if has_sinks:
                m = jnp.maximum(m, sink_ref[h])
            p = jnp.exp(s - m)
            den = jnp.sum(p, axis=-1, keepdims=True)
            if has_sinks:
                den = den + jnp.exp(sink_ref[h] - m)
            ps.append(p.astype(BF16))
            ms[bi, h], dens[bi, h] = m, den
        probs.append(jnp.concatenate(ps, axis=0))
    outs = {}
    for (bi, _, vh, heads), p_all in zip(operands, probs):
        o_all = jnp.dot(p_all, vh, preferred_element_type=F32)
        for r, h in enumerate(heads):
            outs[bi, h] = o_all[r * BLK:(r + 1) * BLK]
    for bi in range(n_batch):
        for pair in range(n_heads // 2):
            h0, h1 = (bi, 2 * pair), (bi, 2 * pair + 1)
            inv = jnp.where(lo, 1.0 / dens[h0], 1.0 / dens[h1])
            o_pair = jnp.where(lo, outs[h0], outs[h1]) * inv
            o_ref[bi, :, pair * LANES:(pair + 1) * LANES] = o_pair.astype(o_ref.dtype)
            if want_lse:
                lse_ref[bi, :, pair * LANES:(pair + 1) * LANES] = jnp.where(
                    lo, ms[h0] + jnp.log(dens[h0]), ms[h1] + jnp.log(dens[h1]))


def _banded_attention(slab, *, q_off, k_off, v_off, n_heads, n_groups,
                      max_dist, table, sinks=None, want_lse=False, out_dtype=BF16):
    b, dil, length, width = slab.shape
    n_blk = length // BLK
    nb = -(-max_dist // BLK)
    kw = (nb + 1) * BLK
    hd = n_heads * HEAD_DIM
    gd = n_groups * HEAD_DIM
    view = slab.reshape(b, dil * length, width)
    bias = _band_bias(table, max_dist, dil, nb)

    def q_map(r, i):
        return (0, r * n_blk + i, q_off // hd)

    def kv_map(off, j):
        return lambda r, i: (0, r * n_blk + jnp.maximum(i - j, 0), off // gd)

    in_specs, args = [], []
    if sinks is not None:
        in_specs.append(pl.BlockSpec(memory_space=pltpu.SMEM))
        args.append(sinks.astype(F32))
    in_specs.append(pl.BlockSpec((b, BLK, hd), q_map))
    args.append(view)
    for off in (k_off, v_off):
        for j in range(nb + 1):
            in_specs.append(pl.BlockSpec((b, BLK, gd), kv_map(off, j)))
            args.append(view)
    in_specs.append(pl.BlockSpec((n_heads, BLK, kw), lambda r, i: (0, 0, 0)))
    args.append(bias)

    out_spec = pl.BlockSpec((b, BLK, hd), lambda r, i: (0, r * n_blk + i, 0))
    out_shape = [jax.ShapeDtypeStruct((b, dil * length, hd), out_dtype)]
    out_specs = [out_spec]
    if want_lse:
        out_shape.append(jax.ShapeDtypeStruct((b, dil * length, hd), F32))
        out_specs.append(out_spec)

    res = pl.pallas_call(
        functools.partial(_banded_kernel, n_heads=n_heads, n_groups=n_groups, nb=nb,
                          has_sinks=sinks is not None, want_lse=want_lse),
        grid=(dil, n_blk),
        in_specs=in_specs,
        out_specs=out_specs,
        out_shape=out_shape,
        compiler_params=_cparams(("parallel", "arbitrary")),
        name="banded_attention",
    )(*args)
    return [r.reshape(b, dil, length, hd) for r in res]


def _dilated_mix_kernel(*refs, n_pat):
    o_refs, l_refs = refs[:n_pat], refs[n_pat:2 * n_pat]
    out_ref = refs[2 * n_pat]
    scratch = refs[2 * n_pat + 1:]

    def natural(ref, buf):
        d = ref.shape[1]
        if d == 1:
            return ref[0, 0]
        rows = ref.shape[2]
        n_chunks = buf.shape[0]
        for r in range(d):
            for c in range(n_chunks):
                buf[c, pl.ds(r, rows, stride=d), :] = ref[0, r, :, c * LANES:(c + 1) * LANES]
        return jnp.concatenate([buf[c] for c in range(n_chunks)], axis=1)

    os_ = [natural(ref, scratch[2 * p]) for p, ref in enumerate(o_refs)]
    ls = [natural(ref, scratch[2 * p + 1]) for p, ref in enumerate(l_refs)]
    m = functools.reduce(jnp.maximum, ls)
    es = [jnp.exp(l - m) for l in ls]
    den = functools.reduce(jnp.add, es)
    acc = functools.reduce(jnp.add, [(e / den) * o for e, o in zip(es, os_)])
    out_ref[...] = acc.astype(out_ref.dtype)


def _dilated_mix(outs, lses, tm):
    b, _, _, c = outs[0].shape
    seq = outs[0].shape[1] * outs[0].shape[2]
    tiles_per_seq = seq // tm

    def spec(a):
        d = a.shape[1]
        return pl.BlockSpec((1, d, tm // d, c), lambda i: (i // tiles_per_seq, 0, i % tiles_per_seq, 0))

    return pl.pallas_call(
        functools.partial(_dilated_mix_kernel, n_pat=len(outs)),
        grid=(b * tiles_per_seq,),
        in_specs=[spec(a) for a in outs] + [spec(a) for a in lses],
        out_specs=pl.BlockSpec((tm, c), lambda i: (i, 0)),
        out_shape=jax.ShapeDtypeStruct((b * seq, c), BF16),
        scratch_shapes=[pltpu.VMEM((c // LANES, tm, LANES), F32) for _ in range(2 * len(outs))],
        compiler_params=_cparams(("parallel",)),
        name="dilated_mix",
    )(*outs, *lses)


def _retention_kernel(q_ref, k_ref, v_ref, g_ref, cos_ref, sin_ref, dmask_ref, qdec_ref, kdec_ref,
                      cdec_ref, gn_ref, o_ref, state_ref):
    @pl.when(pl.program_id(0) == 0)
    def _():
        state_ref[...] = jnp.zeros_like(state_ref)

    cos = cos_ref[...]
    sin = sin_ref[...]
    half = C_QK_DIM // 2

    def rot(x):
        x1, x2 = x[:, :half], x[:, half:]
        return jnp.concatenate([x1 * cos - x2 * sin, x1 * sin + x2 * cos], axis=1)

    for hd in range(C_HEADS):
        qk_cols = slice(hd * C_QK_DIM, (hd + 1) * C_QK_DIM)
        v_cols = slice(hd * C_V_DIM, (hd + 1) * C_V_DIM)
        for bi in range(q_ref.shape[0]):
            q = rot(q_ref[bi, :, qk_cols].astype(F32))
            k = rot(k_ref[bi, :, qk_cols].astype(F32)) * (C_QK_DIM ** -0.5)
            v = v_ref[bi, :, v_cols]
            qb = q.astype(BF16)
            inner = lax.dot_general(qb, k.astype(BF16), (((1,), (1,)), ((), ())),
                                    preferred_element_type=F32) * dmask_ref[hd]
            state = state_ref[bi, hd]
            o = jnp.dot(inner.astype(BF16), v, preferred_element_type=F32)
            o = o + jnp.dot(qb, state.astype(BF16), preferred_element_type=F32) * qdec_ref[hd]
            kd_t = jnp.transpose(k * kdec_ref[hd]).astype(BF16)
            state_ref[bi, hd] = state * cdec_ref[hd] + jnp.dot(kd_t, v, preferred_element_type=F32)

            mu = jnp.mean(o, axis=-1, keepdims=True)
            oc = o - mu
            var = jnp.mean(oc * oc, axis=-1, keepdims=True)
            on = oc * lax.rsqrt(var + 1e-5)
            o_ref[bi, :, v_cols] = (on * gn_ref[:, v_cols]
                                    * jax.nn.silu(g_ref[bi, :, v_cols].astype(F32))).astype(o_ref.dtype)


def _retention(slab, gn, seq):
    b = slab.shape[0]
    n_chunks = seq // C_CHUNK
    half = C_QK_DIM // 2
    pos = jnp.arange(seq, dtype=F32)
    inv = 1.0 / (10000.0 ** (jnp.arange(0, C_QK_DIM, 2, dtype=F32) / C_QK_DIM))
    ang = pos[:, None] * inv[None, :]
    cos, sin = jnp.cos(ang), jnp.sin(ang)
    log_g = jnp.log(1.0 - 2.0 ** (-5.0 - jnp.arange(C_HEADS, dtype=F32)))
    j = jnp.arange(C_CHUNK, dtype=F32)
    diff = j[:, None] - j[None, :]
    dmask = jnp.where(diff >= 0, jnp.exp(diff[None] * log_g[:, None, None]), 0.0)
    q_dec = jnp.exp((j[None, :] + 1.0) * log_g[:, None])[:, :, None]
    k_dec = jnp.exp((C_CHUNK - 1.0 - j[None, :]) * log_g[:, None])[:, :, None]
    chunk_dec = jnp.exp(C_CHUNK * log_g)

    def col(off, w):
        return lambda c: (0, c, off // w)

    const3 = lambda c: (0, 0, 0)
    return pl.pallas_call(
        _retention_kernel,
        grid=(n_chunks,),
        in_specs=[pl.BlockSpec((b, C_CHUNK, C_QK), col(OD_QC, C_QK)),
                  pl.BlockSpec((b, C_CHUNK, C_QK), col(OD_KC, C_QK)),
                  pl.BlockSpec((b, C_CHUNK, C_V), col(OD_VC, C_V)),
                  pl.BlockSpec((b, C_CHUNK, C_V), col(OD_GC, C_V)),
                  pl.BlockSpec((C_CHUNK, half), lambda c: (c, 0)),
                  pl.BlockSpec((C_CHUNK, half), lambda c: (c, 0)),
                  pl.BlockSpec((C_HEADS, C_CHUNK, C_CHUNK), const3),
                  pl.BlockSpec((C_HEADS, C_CHUNK, 1), const3),
                  pl.BlockSpec((C_HEADS, C_CHUNK, 1), const3),
                  pl.BlockSpec(memory_space=pltpu.SMEM),
                  pl.BlockSpec((1, C_V), lambda c: (0, 0))],
        out_specs=pl.BlockSpec((b, C_CHUNK, C_V), lambda c: (0, c, 0)),
        out_shape=jax.ShapeDtypeStruct((b, seq, C_V), BF16),
        scratch_shapes=[pltpu.VMEM((b, C_HEADS, C_QK_DIM, C_V_DIM), F32)],
        compiler_params=_cparams(("arbitrary",)),
        name="retention",
    )(slab, slab, slab, slab, cos, sin, dmask, q_dec, k_dec, chunk_dec, gn.reshape(1, C_V).astype(F32))


def _compress_kernel(x_ref, pos_ref, w1_ref, w2_ref, o_ref):
    x = (x_ref[0].astype(F32) + pos_ref[0]).astype(BF16)
    hid = jax.nn.gelu(jnp.dot(x, w1_ref[0], preferred_element_type=F32))
    o_ref[0] = jnp.dot(hid.astype(BF16), w2_ref[0], preferred_element_type=F32)


def _compress(flat, pos, w1, w2):
    _, bg, rows, width = flat.shape
    tr = min(rows, 256)
    return pl.pallas_call(
        _compress_kernel,
        grid=(2, bg, rows // tr),
        in_specs=[pl.BlockSpec((None, 1, tr, width), lambda s, i, r: (s, i, r, 0)),
                  pl.BlockSpec((1, 1, width), lambda s, i, r: (s, 0, 0)),
                  pl.BlockSpec((1, width, D_CMP_HIDDEN), lambda s, i, r: (s, 0, 0)),
                  pl.BlockSpec((1, D_CMP_HIDDEN, HEAD_DIM), lambda s, i, r: (s, 0, 0))],
        out_specs=pl.BlockSpec((None, 1, tr, HEAD_DIM), lambda s, i, r: (s, i, r, 0)),
        out_shape=jax.ShapeDtypeStruct((2, bg, rows, HEAD_DIM), F32),
        compiler_params=_cparams(("parallel", "parallel", "parallel")),
        name="nsa_compress",
    )(flat, pos, w1, w2)


def _cmp_bias(table):
    tt = jnp.arange(BLK)[:, None]
    m = jnp.arange(CMP_WIN)[None, :] + (BLK // D_CMP_STRIDE) - CMP_WIN
    dist = tt - D_CMP_STRIDE * m - (D_CMP_LEN - 1)
    return jnp.where((dist >= 0)[None], _bias_lookup(table, dist), NEG_INF)


def _sel_matrix():
    c_rel = np.arange(CMP_WIN)[:, None] + (BLK // D_CMP_STRIDE) - CMP_WIN
    j_rel = np.arange(REL_BLOCKS)[None, :] - (REL_BLOCKS - 2)
    return ((c_rel >= 4 * j_rel - 1) & (c_rel <= 4 * j_rel + 3)).astype(np.float32)


def _cmp_attn_kernel(q_ref, kc_ref, vc_ref, bias_ref, sel_ref, o_ref, idx_ref):
    i = pl.program_id(1)
    lane = lax.broadcasted_iota(jnp.int32, (1, LANES), 1)
    lo = lane < HEAD_DIM
    half_mask = (jnp.where(lo, SCORE_SCALE, 0.0).astype(BF16), jnp.where(lo, 0.0, SCORE_SCALE).astype(BF16))
    row = lax.broadcasted_iota(jnp.int32, (BLK, REL_BLOCKS), 0)
    jj = lax.broadcasted_iota(jnp.int32, (BLK, REL_BLOCKS), 1)
    cur = (REL_BLOCKS - 2) + (row >= D_SEL_LEN).astype(jnp.int32)
    first = (REL_BLOCKS - 2) - 2 * i
    exists = jj >= first
    forced = exists & ((jj == first) | (jj == cur) | (jj == cur - 1))
    valid = exists & (jj <= cur)
    jjf = jj.astype(F32)
    out_lane = lax.broadcasted_iota(jnp.int32, (BLK, LANES), 1)
    out_row = lax.broadcasted_iota(jnp.int32, (BLK, LANES), 0)
    cur_abs = 2 * i + (out_row >= D_SEL_LEN).astype(jnp.int32)
    tt_in_blk = out_row % D_SEL_LEN

    n_keys = CMP_PER_BLK * (i + 1)
    for width in CMP_WIDTHS:
        lo_w = width - CMP_WIDTHS[0] if width > CMP_WIDTHS[0] else -1

        @pl.when((n_keys <= width) & (n_keys > lo_w))
        def _(width=width):
            _cmp_attn_body(q_ref, kc_ref, vc_ref, bias_ref, sel_ref, o_ref, idx_ref, i, width, lo, half_mask,
                           forced, valid, jjf, first, out_lane, cur_abs, tt_in_blk)


def _cmp_attn_body(q_ref, kc_ref, vc_ref, bias_ref, sel_ref, o_ref, idx_ref, i, width, lo, half_mask,
                   forced, valid, jjf, first, out_lane, cur_abs, tt_in_blk):
    hpg = D_HEADS // D_KV_HEADS
    off = CMP_WIN - width
    n_keys = CMP_PER_BLK * (i + 1)
    start = pl.multiple_of(n_keys + off, CMP_PER_BLK)
    ucol = lax.broadcasted_iota(jnp.int32, (1, width), 1) + off
    edge = jnp.where(ucol < CMP_WIN - n_keys, NEG_INF, 0.0).astype(F32)

    for g in range(D_KV_HEADS):
        kwin = kc_ref[0, g, pl.ds(start, width), :].astype(BF16)
        vwin = vc_ref[0, g, pl.ds(start, width), :].astype(BF16)
        qs = []
        for hh in range(hpg):
            h = g * hpg + hh
            qs.append(q_ref[0, :, (h // 2) * LANES:(h // 2 + 1) * LANES] * half_mask[h % 2])
        s_all = lax.dot_general(jnp.concatenate(qs, axis=0), kwin, (((1,), (1,)), ((), ())),
                                preferred_element_type=F32)
        imp = jnp.zeros((BLK, width), F32)
        pns = []
        for hh in range(hpg):
            h = g * hpg + hh
            s = s_all[hh * BLK:(hh + 1) * BLK] + bias_ref[h, :, off:] + edge
            m = jnp.maximum(jnp.max(s, axis=-1, keepdims=True), ROW_MAX_FLOOR)
            p = jnp.exp(s - m)
            den = jnp.maximum(jnp.sum(p, axis=-1, keepdims=True), 1e-30)
            pn = p * (1.0 / den)
            imp = imp + pn
            pns.append(pn.astype(BF16))
        o_all = jnp.dot(jnp.concatenate(pns, axis=0), vwin, preferred_element_type=F32)
        for pr in range(hpg // 2):
            pair = (g * hpg) // 2 + pr
            o_ref[0, :, pair * LANES:(pair + 1) * LANES] = jnp.where(
                lo, o_all[2 * pr * BLK:(2 * pr + 1) * BLK], o_all[(2 * pr + 1) * BLK:(2 * pr + 2) * BLK]
            ).astype(o_ref.dtype)

        sel = sel_ref[off:, :]
        hi = imp.astype(BF16)
        r1 = imp - hi.astype(F32)
        mid = r1.astype(BF16)
        low = (r1 - mid.astype(F32)).astype(BF16)
        imp_sel = (jnp.dot(hi, sel, preferred_element_type=F32)
                   + jnp.dot(mid, sel, preferred_element_type=F32)
                   + jnp.dot(low, sel, preferred_element_type=F32))
        score = jnp.where(forced, FORCED_SCORE, jnp.where(valid, imp_sel, INVALID_SCORE))
        picked = jnp.zeros((BLK, LANES), jnp.int32)
        for r in range(D_SEL_COUNT):
            m = jnp.max(score, axis=-1, keepdims=True)
            am = jnp.min(jnp.where(score == m, jjf, float(REL_BLOCKS)), axis=-1, keepdims=True)
            none = m < 0.5 * INVALID_SCORE
            blk = am.astype(jnp.int32) - first
            key_blk = jnp.where(none, 0, blk)
            bias_row = jnp.where(none, SEL_NONE, jnp.minimum(cur_abs - blk, SEL_FAR)) * D_SEL_LEN + tt_in_blk
            picked = jnp.where(out_lane == r, key_blk, picked)
            picked = jnp.where(out_lane == D_SEL_COUNT + r, bias_row, picked)
            score = jnp.where(jjf == am, -jnp.inf, score)
        idx_ref[0, g] = picked


def _cmp_attention(slab, kc_pad, vc_pad, table, seq):
    b = slab.shape[0]
    n_blk = seq // BLK
    rows = kc_pad.shape[2]
    bias = _cmp_bias(table)
    sel = jnp.asarray(_sel_matrix(), BF16)
    return pl.pallas_call(
        _cmp_attn_kernel,
        grid=(b, n_blk),
        in_specs=[pl.BlockSpec((1, BLK, D_Q), lambda bi, i: (bi, i, OD_QD // D_Q)),
                  pl.BlockSpec((1, D_KV_HEADS, rows, LANES), lambda bi, i: (bi, 0, 0, 0)),
                  pl.BlockSpec((1, D_KV_HEADS, rows, LANES), lambda bi, i: (bi, 0, 0, 0)),
                  pl.BlockSpec((D_HEADS, BLK, CMP_WIN), lambda bi, i: (0, 0, 0)),
                  pl.BlockSpec((CMP_WIN, REL_BLOCKS), lambda bi, i: (0, 0))],
        out_specs=[pl.BlockSpec((1, BLK, D_Q), lambda bi, i: (bi, i, 0)),
                   pl.BlockSpec((1, D_KV_HEADS, BLK, LANES), lambda bi, i: (bi, 0, i, 0))],
        out_shape=[jax.ShapeDtypeStruct((b, seq, D_Q), BF16),
                   jax.ShapeDtypeStruct((b, D_KV_HEADS, seq, LANES), jnp.int32)],
        compiler_params=_cparams(("parallel", "arbitrary")),
        name="nsa_cmp_attention",
    )(slab, kc_pad, vc_pad, bias, sel)


def _sel_bias(table):
    hpg = D_HEADS // D_KV_HEADS
    delta = jnp.arange(SEL_FAR)[:, None, None]
    tt = jnp.arange(D_SEL_LEN)[None, :, None]
    l = jnp.arange(D_SEL_LEN)[None, None, :]
    dist = D_SEL_LEN * delta + tt - l
    near = jnp.where((dist >= 0)[None], _bias_lookup(table, dist), NEG_INF)
    far = jnp.broadcast_to(table.astype(F32)[REL_BUCKETS - 1][:, None, None, None],
                           (D_HEADS, 1, D_SEL_LEN, D_SEL_LEN))
    none = jnp.full((D_HEADS, 1, D_SEL_LEN, D_SEL_LEN), NEG_INF, F32)
    rows = jnp.concatenate([near, far, none], axis=1)
    half = D_SEL_LEN // 2
    rows = rows.reshape(D_KV_HEADS, hpg, SEL_NONE + 1, D_SEL_LEN, half, 2)
    rows = jnp.transpose(rows, (0, 2, 3, 5, 1, 4))
    rows = rows.reshape(D_KV_HEADS, (SEL_NONE + 1) * D_SEL_LEN, 2 * hpg, half)
    return jnp.tile(rows, (1, 1, 1, LANES // half))


SEL_IDX = 2 * D_SEL_COUNT
SEL_UNROLL = 16


def _sel_attn_kernel(idx_hbm, q_ref, kv_ref, bias_ref, o_ref, idx_smem, sem):
    n_g, n_i = pl.num_programs(1), pl.num_programs(2)
    step = (pl.program_id(0) * n_g + pl.program_id(1)) * n_i + pl.program_id(2)
    total = pl.num_programs(0) * n_g * n_i
    slot = step % 2

    tile_words = BLK * SEL_IDX

    def idx_copy(s, sl):
        dst = idx_smem.at[pl.ds(pl.multiple_of(sl * tile_words, tile_words), tile_words)]
        return pltpu.make_async_copy(idx_hbm.at[s], dst, sem.at[sl])

    @pl.when(step == 0)
    def _():
        idx_copy(0, 0).start()

    @pl.when(step + 1 < total)
    def _():
        idx_copy(step + 1, 1 - slot).start()

    idx_copy(step, slot).wait()

    lane = lax.broadcasted_iota(jnp.int32, (1, LANES), 1)
    lo = lane < HEAD_DIM
    hpg = D_HEADS // D_KV_HEADS
    quarter = D_SEL_LEN // 2
    def token_scores(tl):
        picks = idx_smem.at[pl.ds(slot * tile_words + tl * SEL_IDX, SEL_IDX)]
        qq = q_ref[0, 0, tl] * SCORE_SCALE
        qbd = jnp.concatenate([jnp.where(lo, qq, 0.0), jnp.where(lo, 0.0, qq)], axis=0).astype(BF16)
        ks, vs, bs = [], [], []
        for n in range(D_SEL_COUNT):
            kv = kv_ref[0, 0, picks[n]]
            ks.append(kv[:quarter])
            vs.append(kv[quarter:])
            bs.append(bias_ref[0, picks[D_SEL_COUNT + n]])
        s = lax.dot_general(qbd, jnp.concatenate(ks, axis=0), (((1,), (1,)), ((), ())),
                            preferred_element_type=F32)
        bias = jnp.concatenate(
            [jnp.where(lane < quarter, bs[c],
                       jnp.where(lane < 2 * quarter, bs[c + 1],
                                 jnp.where(lane < 3 * quarter, bs[c + 2], bs[c + 3])))
             for c in range(0, D_SEL_COUNT, 4)], axis=1)
        return s + bias, jnp.concatenate(vs, axis=0)

    def fold(x16):
        return x16[:hpg], x16[hpg:]

    def body(it, carry):
        t0 = it * SEL_UNROLL
        sv = [token_scores(t0 + u) for u in range(SEL_UNROLL)]
        ps, dens = [], []
        for s, _ in sv:
            m = jnp.maximum(*fold(jnp.max(s, axis=-1, keepdims=True)))
            p = jnp.exp(s - jnp.concatenate([m, m], axis=0))
            dens.append(sum(fold(jnp.sum(p, axis=-1, keepdims=True))))
            ps.append(p.astype(BF16))
        outs = []
        for p, den, (_, vall) in zip(ps, dens, sv):
            o_top, o_bot = fold(jnp.dot(p, vall, preferred_element_type=F32))
            outs.append(jnp.where(lo, o_top, o_bot) * (1.0 / den))
        for u in range(SEL_UNROLL):
            o_ref[0, 0, t0 + u] = outs[u]
        return carry

    lax.fori_loop(0, BLK // SEL_UNROLL, body, 0)


def _sel_attention(idx, q_sel, kv_sel, table, seq):
    b = q_sel.shape[0]
    hpg = D_HEADS // D_KV_HEADS
    n_blk = seq // BLK
    n_sb = seq // D_SEL_LEN
    bias = _sel_bias(table)
    kv_spec = pl.BlockSpec((1, 1, n_sb, D_SEL_LEN, LANES), lambda bi, g, i: (bi, g, 0, 0, 0))
    return pl.pallas_call(
        _sel_attn_kernel,
        grid=(b, D_KV_HEADS, n_blk),
        in_specs=[pl.BlockSpec(memory_space=pl.ANY),
                  pl.BlockSpec((1, 1, BLK, hpg, LANES), lambda bi, g, i: (bi, g, i, 0, 0)),
                  kv_spec,
                  pl.BlockSpec((1, (SEL_NONE + 1) * D_SEL_LEN, 2 * hpg, LANES), lambda bi, g, i: (g, 0, 0, 0))],
        out_specs=pl.BlockSpec((1, 1, BLK, hpg, LANES), lambda bi, g, i: (bi, g, i, 0, 0)),
        out_shape=jax.ShapeDtypeStruct((b, D_KV_HEADS, seq, hpg, LANES), F32),
        scratch_shapes=[pltpu.SMEM((2 * BLK * SEL_IDX,), jnp.int32), pltpu.SemaphoreType.DMA((2,))],
        compiler_params=_cparams(("arbitrary", "arbitrary", "arbitrary")),
        name="nsa_sel_attention",
    )(idx, q_sel, kv_sel, bias)


def _gate_expand():
    e = np.zeros((3, LANES, D_Q), np.float32)
    for h in range(D_HEADS):
        for c in range(3):
            e[c, 3 * h + c, h * HEAD_DIM:(h + 1) * HEAD_DIM] = 1.0
    return e


def _head_place():
    hpg = D_HEADS // D_KV_HEADS
    p = np.zeros((hpg * LANES, hpg * HEAD_DIM), np.float32)
    for hh in range(hpg):
        for lane in range(LANES):
            p[hh * LANES + lane, hh * HEAD_DIM + lane % HEAD_DIM] = 1.0
    return p


def _nsa_gate_kernel(gd_ref, e_ref, place_ref, oc_ref, os_ref, ow_ref, out_ref):
    hpg = D_HEADS // D_KV_HEADS
    tm = out_ref.shape[0]
    groups = []
    for g in range(D_KV_HEADS):
        rows = jnp.concatenate([os_ref[0, g, pl.ds(hh, tm, stride=hpg), :].astype(BF16) for hh in range(hpg)],
                               axis=1)
        groups.append(jnp.dot(rows, place_ref[...], preferred_element_type=F32))
    o_s = jnp.concatenate(groups, axis=1)

    sg = jax.nn.sigmoid(gd_ref[...].astype(F32))
    hi = sg.astype(BF16)
    low = (sg - hi.astype(F32)).astype(BF16)
    acc = None
    for c, branch in enumerate((oc_ref[...].astype(F32), o_s, ow_ref[...].astype(F32))):
        gate = (jnp.dot(hi, e_ref[c], preferred_element_type=F32)
                + jnp.dot(low, e_ref[c], preferred_element_type=F32))
        acc = gate * branch if acc is None else acc + gate * branch
    out_ref[...] = acc.astype(out_ref.dtype)


def _nsa_gate(slab2d, o_c, o_s_raw, o_w, tm):
    m = slab2d.shape[0]
    b, g_kv, seq, hpg, _ = o_s_raw.shape
    tiles_per_seq = seq // tm
    e = jnp.asarray(_gate_expand(), BF16)
    place = jnp.asarray(_head_place(), BF16)
    spec = pl.BlockSpec((tm, D_Q), lambda i: (i, 0))
    return pl.pallas_call(
        _nsa_gate_kernel,
        grid=(m // tm,),
        in_specs=[pl.BlockSpec((tm, LANES), lambda i: (i, OD_GD // LANES)),
                  pl.BlockSpec((3, LANES, D_Q), lambda i: (0, 0, 0)),
                  pl.BlockSpec((hpg * LANES, hpg * HEAD_DIM), lambda i: (0, 0)),
                  spec,
                  pl.BlockSpec((1, g_kv, tm * hpg, LANES),
                               lambda i: (i // tiles_per_seq, 0, i % tiles_per_seq, 0)),
                  spec],
        out_specs=spec,
        out_shape=jax.ShapeDtypeStruct((m, D_Q), BF16),
        compiler_params=_cparams(("parallel",)),
        name="nsa_gate",
    )(slab2d, e, place, o_c, o_s_raw.reshape(b, g_kv, seq * hpg, LANES), o_w)


def _even_mixer(h, nw, w_in, sinks, w_out, rel_table, b, seq):
    qa, ka, va, qb, kb, vb = jnp.split(w_in, [int(c) for c in np.cumsum([A_Q, A_KV, A_KV, B_W, B_W])], axis=1)
    pad = jnp.zeros((D_MODEL, EVEN_SLAB - w_in.shape[1]), w_in.dtype)
    w_slab = jnp.concatenate([qa, qb, kb, vb, ka, va, pad], axis=1).astype(BF16)
    dilations = [dil for _, dil in B_PATTERNS if dil > 1]
    slabs = _norm_matmul(h, nw, w_slab, TM_IN_PROJ, TN_EVEN, dilations, seq)
    by_dil = {1: slabs[0].reshape(b, 1, seq, EVEN_SLAB)}
    by_dil.update(zip(dilations, slabs[1:]))

    (oa,) = _banded_attention(by_dil[1], q_off=EV_QA, k_off=EV_KA, v_off=EV_VA,
                              n_heads=A_HEADS, n_groups=A_KV_HEADS, max_dist=A_WINDOW - 1,
                              table=rel_table[:, :A_HEADS], sinks=sinks)
    outs, lses = [], []
    for window, dil in B_PATTERNS:
        o, lse = _banded_attention(by_dil[dil], q_off=EV_QB, k_off=EV_KB, v_off=EV_VB,
                                   n_heads=B_HEADS, n_groups=B_HEADS, max_dist=window // dil,
                                   table=rel_table[:, A_HEADS:A_HEADS + B_HEADS], want_lse=True, out_dtype=F32)
        outs.append(o)
        lses.append(lse)
    ob = _dilated_mix(outs, lses, TM_MIX)
    w_out = w_out.astype(BF16)
    return _proj_residual(h, [oa.reshape(b * seq, A_Q), ob], [w_out[:A_Q], w_out[A_Q:]], TM_OUT_PROJ)


def _nsa(slab, pos_k, pos_v, k_w1, k_w2, v_w1, v_w2, rel_table, b, seq):
    g_kv = D_KV_HEADS
    hpg = D_HEADS // g_kv
    slab2d = slab.reshape(b * seq, ODD_SLAB)
    n_rows = seq // D_CMP_STRIDE

    def rows16(off):
        a = slab[:, :, off:off + D_KV].reshape(b, n_rows, D_CMP_STRIDE, g_kv, HEAD_DIM)
        a = jnp.transpose(a, (0, 3, 1, 2, 4)).reshape(b * g_kv, n_rows, D_CMP_STRIDE * HEAD_DIM)
        nxt = jnp.concatenate([a[:, 1:], jnp.zeros_like(a[:, :1])], axis=1)
        return jnp.concatenate([a, nxt], axis=-1)

    flat = jnp.stack([rows16(OD_KCMP), rows16(OD_VCMP)])
    pos = jnp.stack([pos_k.reshape(1, -1), pos_v.reshape(1, -1)]).astype(F32)
    w1 = jnp.stack([k_w1, v_w1]).astype(BF16)
    w2 = jnp.stack([k_w2, v_w2]).astype(BF16)
    cmp = _compress(flat, pos, w1, w2).reshape(2, b, g_kv, n_rows, HEAD_DIM)
    cmp = jnp.pad(cmp, ((0, 0), (0, 0), (0, 0), (CMP_WIN, 0), (0, 0)))
    cmp = jnp.concatenate([cmp, cmp], axis=-1)
    o_c, idx = _cmp_attention(slab, cmp[0], cmp[1], rel_table, seq)

    idx = idx[..., :SEL_IDX].reshape(b * g_kv * (seq // BLK), BLK * SEL_IDX)

    def per_group(off):
        return slab[:, :, off:off + D_KV].reshape(b, seq, g_kv, HEAD_DIM)

    def two_per_row(off):
        a = jnp.transpose(per_group(off), (0, 2, 1, 3))
        return a.reshape(b, g_kv, seq // D_SEL_LEN, D_SEL_LEN // 2, LANES)

    q_sel = slab[:, :, OD_QD:OD_QD + D_Q].reshape(b, seq, g_kv, hpg, HEAD_DIM)
    q_sel = jnp.transpose(q_sel, (0, 2, 1, 3, 4)).astype(F32)
    q_sel = jnp.concatenate([q_sel, q_sel], axis=-1)
    kv_sel = jnp.concatenate([two_per_row(OD_KSLC), two_per_row(OD_VSLC)], axis=3)
    o_s = _sel_attention(idx, q_sel, kv_sel, rel_table, seq)

    (o_w,) = _banded_attention(slab.reshape(b, 1, seq, ODD_SLAB), q_off=OD_QD, k_off=OD_KWIN, v_off=OD_VWIN,
                               n_heads=D_HEADS, n_groups=D_KV_HEADS, max_dist=D_WINDOW - 1, table=rel_table)
    return _nsa_gate(slab2d, o_c.reshape(b * seq, D_Q), o_s, o_w.reshape(b * seq, D_Q), TM_GATE)


def _odd_mixer(h, nw, w_in, ret_gn, pos_k, pos_v, k_w1, k_w2, v_w1, v_w2, w_out, rel_table, b, seq):
    pad = jnp.zeros((D_MODEL, ODD_SLAB - ODD_IN), w_in.dtype)
    w_slab = jnp.concatenate([w_in, pad], axis=1).astype(BF16)
    slab = _norm_matmul(h, nw, w_slab, TM_IN_PROJ, TN_ODD).reshape(b, seq, ODD_SLAB)
    oc = _retention(slab, ret_gn, seq)
    od = _nsa(slab, pos_k, pos_v, k_w1, k_w2, v_w1, v_w2, rel_table, b, seq)
    w_out = w_out.astype(BF16)
    return _proj_residual(h, [oc.reshape(b * seq, C_V), od], [w_out[:C_V], w_out[C_V:]], TM_OUT_PROJ)


def kernel(x, rel_table, norm_mix, norm_ffn, norm_final, even_w_in, even_sinks, even_w_out, odd_w_in, odd_ret_gn, odd_cmp_pos_k, odd_cmp_pos_v, odd_cmp_k_w1, odd_cmp_k_w2, odd_cmp_v_w1, odd_cmp_v_w2, odd_w_out, ffn_w_gate, ffn_w_up, ffn_w_down):
    b, seq, d = x.shape
    h = x.reshape(b * seq, d)
    for layer in range(DEPTH):
        li = layer // 2
        if layer % 2 == 0:
            h = _even_mixer(h, norm_mix[layer], even_w_in[li], even_sinks[li], even_w_out[li], rel_table, b, seq)
        else:
            h = _odd_mixer(h, norm_mix[layer], odd_w_in[li], odd_ret_gn[li], odd_cmp_pos_k[li],
                           odd_cmp_pos_v[li], odd_cmp_k_w1[li], odd_cmp_k_w2[li], odd_cmp_v_w1[li],
                           odd_cmp_v_w2[li], odd_w_out[li], rel_table, b, seq)
        h = _ffn(h, norm_ffn[layer], ffn_w_gate[layer].astype(BF16), ffn_w_up[layer].astype(BF16),
                 ffn_w_down[layer].astype(BF16), norm_final, layer == DEPTH - 1, TM_FFN)
    return h.reshape(b, seq, d)
```

```python
import functools
import math

import numpy as np
import jax
import jax.numpy as jnp
from jax import lax
from jax.experimental import pallas as pl
from jax.experimental.pallas import tpu as pltpu

F32 = jnp.float32
BF16 = jnp.bfloat16

D_MODEL = 1024
DEPTH = 4
HEAD_DIM = 64
BLK = 128
NEG_INF = -1e30
REL_BUCKETS = 32
REL_MAX_DIST = 2048
A_HEADS = 8
A_KV_HEADS = 2
A_WINDOW = 128
B_HEADS = 8
B_PATTERNS = ((128, 1), (512, 4), (2048, 16))
C_HEADS = 4
C_QK_DIM = 256
C_V_DIM = 512
C_CHUNK = 128
D_HEADS = 16
D_KV_HEADS = 2
D_CMP_LEN = 32
D_CMP_STRIDE = 16
D_CMP_HIDDEN = 128
D_SEL_LEN = 64
D_SEL_COUNT = 16
D_WINDOW = 512
D_FF = 2816

A_Q = A_HEADS * HEAD_DIM
A_KV = A_KV_HEADS * HEAD_DIM
B_W = B_HEADS * HEAD_DIM
C_QK = C_HEADS * C_QK_DIM
C_V = C_HEADS * C_V_DIM
D_Q = D_HEADS * HEAD_DIM
D_KV = D_KV_HEADS * HEAD_DIM
ODD_IN = 2 * C_QK + 2 * C_V + D_Q + 6 * D_KV + 3 * D_HEADS

SCORE_SCALE = HEAD_DIM ** -0.5
LANES = 128
VMEM_LIMIT = 56 * 1024 * 1024

EVEN_SLAB = 2560
EV_QA, EV_QB, EV_KB, EV_VB, EV_KA, EV_VA = 0, 512, 1024, 1536, 2048, 2176
ODD_SLAB = 8192
OD_QC, OD_KC, OD_VC, OD_GC, OD_QD = 0, 1024, 2048, 4096, 6144
OD_KCMP, OD_VCMP, OD_KSLC, OD_VSLC, OD_KWIN, OD_VWIN, OD_GD = 7168, 7296, 7424, 7552, 7680, 7808, 7936

TM_IN_PROJ = 1024
TN_EVEN = EVEN_SLAB // 2
TN_ODD = ODD_SLAB // 4
TM_OUT_PROJ = 512
TM_FFN = 512
TM_MIX = 1024
TM_GATE = 512

FORCED_SCORE = 1e9
INVALID_SCORE = -1e9
ROW_MAX_FLOOR = 0.1 * NEG_INF
CMP_PER_BLK = BLK // D_CMP_STRIDE

SEL_FAR = 25
SEL_NONE = 26
CMP_WIN = 1024
CMP_WIDTHS = (256, 512, 768, 1024)
REL_BLOCKS = 256


def _cparams(sem):
    return pltpu.CompilerParams(dimension_semantics=sem, vmem_limit_bytes=VMEM_LIMIT)


def _t5_bucket(dist):
    max_exact = REL_BUCKETS // 2
    d = jnp.maximum(dist, 0)
    df = jnp.maximum(d, 1).astype(jnp.float32)
    large = max_exact + (jnp.log(df / max_exact) / math.log(REL_MAX_DIST / max_exact)
                         * (REL_BUCKETS - max_exact)).astype(jnp.int32)
    large = jnp.minimum(large, REL_BUCKETS - 1)
    return jnp.where(d < max_exact, d, large)


def _bias_lookup(table, dist):
    bucket = _t5_bucket(dist)[None]
    tab = table.astype(F32)
    expand = (slice(None),) + (None,) * dist.ndim
    out = jnp.zeros((tab.shape[1],) + dist.shape, F32)
    for b in range(REL_BUCKETS):
        out = jnp.where(bucket == b, tab[b][expand], out)
    return out


def _rms(x, w, eps=1e-6):
    return x * lax.rsqrt(jnp.mean(x * x, axis=-1, keepdims=True) + eps) * w


def _norm_matmul_kernel(h_ref, nw_ref, w_ref, o_ref, *rest, dilations):
    @pl.when(pl.program_id(1) == 0)
    def _():
        rest[-1][...] = _rms(h_ref[...], nw_ref[...]).astype(BF16)

    acc = jnp.dot(rest[-1][...], w_ref[...], preferred_element_type=F32)
    o_ref[...] = acc.astype(o_ref.dtype)
    if dilations:
        acc_ref = rest[-2]
        n_chunks = acc_ref.shape[0]
        for c in range(n_chunks):
            acc_ref[c] = acc[:, c * LANES:(c + 1) * LANES]
        for d, ref in zip(dilations, rest):
            rows = acc_ref.shape[1] // d
            for r in range(d):
                ref[0, r] = jnp.concatenate(
                    [acc_ref[c, pl.ds(r, rows, stride=d), :] for c in range(n_chunks)], axis=1).astype(ref.dtype)


def _norm_matmul(h, nw, w, tm, tn, dilations=(), seq=None):
    m, d_model = h.shape
    n = w.shape[1]
    out_specs = [pl.BlockSpec((tm, tn), lambda i, j: (i, j))]
    out_shape = [jax.ShapeDtypeStruct((m, n), BF16)]
    scratch = []
    if dilations:
        tiles_per_seq = seq // tm
        for d in dilations:
            out_specs.append(pl.BlockSpec((1, d, tm // d, tn),
                                          lambda i, j: (i // tiles_per_seq, 0, i % tiles_per_seq, j)))
            out_shape.append(jax.ShapeDtypeStruct((m // seq, d, seq // d, n), BF16))
        scratch.append(pltpu.VMEM((tn // LANES, tm, LANES), F32))
    scratch.append(pltpu.VMEM((tm, d_model), BF16))
    res = pl.pallas_call(
        functools.partial(_norm_matmul_kernel, dilations=tuple(dilations)),
        grid=(m // tm, n // tn),
        in_specs=[pl.BlockSpec((tm, d_model), lambda i, j: (i, 0)),
                  pl.BlockSpec((1, d_model), lambda i, j: (0, 0)),
                  pl.BlockSpec((d_model, tn), lambda i, j: (0, j))],
        out_specs=out_specs,
        out_shape=out_shape,
        scratch_shapes=scratch,
        compiler_params=_cparams(("parallel", "arbitrary")),
        name="norm_matmul",
    )(h, nw.reshape(1, d_model), w)
    return res if dilations else res[0]


def _proj_residual_kernel(*refs, n_in):
    h_ref, out_ref = refs[0], refs[-1]
    acc = h_ref[...]
    for o_ref, w_ref in zip(refs[1:1 + n_in], refs[1 + n_in:1 + 2 * n_in]):
        acc = acc + jnp.dot(o_ref[...], w_ref[...], preferred_element_type=F32)
    out_ref[...] = acc


def _proj_residual(h, outs, ws, tm):
    m, d = h.shape
    n_in = len(outs)
    in_specs = [pl.BlockSpec((tm, d), lambda i: (i, 0))]
    in_specs += [pl.BlockSpec((tm, o.shape[1]), lambda i: (i, 0)) for o in outs]
    in_specs += [pl.BlockSpec(w.shape, lambda i: (0, 0)) for w in ws]
    return pl.pallas_call(
        functools.partial(_proj_residual_kernel, n_in=n_in),
        grid=(m // tm,),
        in_specs=in_specs,
        out_specs=pl.BlockSpec((tm, d), lambda i: (i, 0)),
        out_shape=jax.ShapeDtypeStruct((m, d), F32),
        compiler_params=_cparams(("parallel",)),
        name="proj_residual",
    )(h, *outs, *ws)


def _ffn_kernel(h_ref, nw_ref, wg_ref, wu_ref, wd_ref, fw_ref, o_ref, *, final):
    h = h_ref[...]
    hn = _rms(h, nw_ref[...]).astype(BF16)
    g = jnp.dot(hn, wg_ref[...], preferred_element_type=F32)
    u = jnp.dot(hn, wu_ref[...], preferred_element_type=F32)
    a = (jax.nn.silu(g) * u).astype(BF16)
    y = h + jnp.dot(a, wd_ref[...], preferred_element_type=F32)
    if final:
        y = _rms(y, fw_ref[...])
    o_ref[...] = y


def _ffn(h, nw, wg, wu, wd, fw, final, tm):
    m, d = h.shape
    ff = wg.shape[1]
    resident = dict(pipeline_mode=pl.Buffered(1))
    return pl.pallas_call(
        functools.partial(_ffn_kernel, final=final),
        grid=(m // tm,),
        in_specs=[pl.BlockSpec((tm, d), lambda i: (i, 0)),
                  pl.BlockSpec((1, d), lambda i: (0, 0)),
                  pl.BlockSpec((d, ff), lambda i: (0, 0), **resident),
                  pl.BlockSpec((d, ff), lambda i: (0, 0), **resident),
                  pl.BlockSpec((ff, d), lambda i: (0, 0), **resident),
                  pl.BlockSpec((1, d), lambda i: (0, 0))],
        out_specs=pl.BlockSpec((tm, d), lambda i: (i, 0)),
        out_shape=jax.ShapeDtypeStruct((m, d), F32),
        compiler_params=_cparams(("parallel",)),
        name="ffn",
    )(h, nw.reshape(1, d), wg, wu, wd, fw.reshape(1, d))


def _band_bias(table, max_dist, dist_scale, nb):
    kw = (nb + 1) * BLK
    rel = jnp.arange(BLK)[:, None] + nb * BLK - jnp.arange(kw)[None, :]
    band = (rel >= 0) & (rel <= max_dist)
    return jnp.where(band[None], _bias_lookup(table, rel * dist_scale), NEG_INF)


def _swap_halves(x):
    return jnp.concatenate([x[:, HEAD_DIM:], x[:, :HEAD_DIM]], axis=1)


def _banded_kernel(*refs, n_heads, n_groups, nb, has_sinks, want_lse):
    pos = 0
    if has_sinks:
        sink_ref = refs[0]
        pos = 1
    q_ref = refs[pos]
    k_refs = refs[pos + 1:pos + 2 + nb]
    v_refs = refs[pos + 2 + nb:pos + 3 + 2 * nb]
    bias_ref = refs[pos + 3 + 2 * nb]
    o_ref = refs[pos + 4 + 2 * nb]
    lse_ref = refs[pos + 5 + 2 * nb] if want_lse else None

    i = pl.program_id(1)
    n_batch = q_ref.shape[0]
    kw = (nb + 1) * BLK
    hpg = n_heads // n_groups
    lane = lax.broadcasted_iota(jnp.int32, (1, LANES), 1)
    lo = lane < HEAD_DIM
    half_mask = (jnp.where(lo, SCORE_SCALE, 0.0).astype(BF16), jnp.where(lo, 0.0, SCORE_SCALE).astype(BF16))
    col = lax.broadcasted_iota(jnp.int32, (1, kw), 1)
    edge = jnp.where(col < (nb - i) * BLK, NEG_INF, 0.0).astype(F32)

    operands = []
    for bi in range(n_batch):
        kcat = jnp.concatenate([k_refs[nb - jj][bi] for jj in range(nb + 1)], axis=0)
        vcat = jnp.concatenate([v_refs[nb - jj][bi] for jj in range(nb + 1)], axis=0)
        if hpg == 1:
            for p in range(n_heads // 2):
                operands.append((bi, kcat[:, p * LANES:(p + 1) * LANES], vcat[:, p * LANES:(p + 1) * LANES],
                                 [2 * p, 2 * p + 1]))
        else:
            k_sw = _swap_halves(kcat)
            v_sw = _swap_halves(vcat)
            for g in range(n_groups):
                for par in range(2):
                    heads = [h for h in range(g * hpg, (g + 1) * hpg) if h % 2 == par]
                    operands.append((bi, kcat if g == par else k_sw, vcat if g == par else v_sw, heads))

    scores = []
    for bi, kh, _, heads in operands:
        qz = jnp.concatenate([q_ref[bi, :, (h // 2) * LANES:(h // 2 + 1) * LANES] * half_mask[h % 2]
                              for h in heads], axis=0)
        scores.append(lax.dot_general(qz, kh, (((1,), (1,)), ((), ())), preferred_element_type=F32))
    ms, dens, probs = {}, {}, []
    for (bi, _, _, heads), s_all in zip(operands, scores):
        ps = []
        for r, h in enumerate(heads):
            s = s_all[r * BLK:(r + 1) * BLK] + bias_ref[h] + edge
            m = jnp.max(s, axis=-1, keepdims=True)
            if has_sinks:
                m = jnp.maximum(m, sink_ref[h])
            p = jnp.exp(s - m)
            den = jnp.sum(p, axis=-1, keepdims=True)
            if has_sinks:
                den = den + jnp.exp(sink_ref[h] - m)
            ps.append(p.astype(BF16))
            ms[bi, h], dens[bi, h] = m, den
        probs.append(jnp.concatenate(ps, axis=0))
    outs = {}
    for (bi, _, vh, heads), p_all in zip(operands, probs):
        o_all = jnp.dot(p_all, vh, preferred_element_type=F32)
        for r, h in enumerate(heads):
            outs[bi, h] = o_all[r * BLK:(r + 1) * BLK]
    for bi in range(n_batch):
        for pair in range(n_heads // 2):
            h0, h1 = (bi, 2 * pair), (bi, 2 * pair + 1)
            inv = jnp.where(lo, 1.0 / dens[h0], 1.0 / dens[h1])
            o_pair = jnp.where(lo, outs[h0], outs[h1]) * inv
            o_ref[bi, :, pair * LANES:(pair + 1) * LANES] = o_pair.astype(o_ref.dtype)
            if want_lse:
                lse_ref[bi, :, pair * LANES:(pair + 1) * LANES] = jnp.where(
                    lo, ms[h0] + jnp.log(dens[h0]), ms[h1] + jnp.log(dens[h1]))


def _banded_attention(slab, *, q_off, k_off, v_off, n_heads, n_groups,
                      max_dist, table, sinks=None, want_lse=False, out_dtype=BF16):
    b, dil, length, width = slab.shape
    n_blk = length // BLK
    nb = -(-max_dist // BLK)
    kw = (nb + 1) * BLK
    hd = n_heads * HEAD_DIM
    gd = n_groups * HEAD_DIM
    view = slab.reshape(b, dil * length, width)
    bias = _band_bias(table, max_dist, dil, nb)

    def q_map(r, i):
        return (0, r * n_blk + i, q_off // hd)

    def kv_map(off, j):
        return lambda r, i: (0, r * n_blk + jnp.maximum(i - j, 0), off // gd)

    in_specs, args = [], []
    if sinks is not None:
        in_specs.append(pl.BlockSpec(memory_space=pltpu.SMEM))
        args.append(sinks.astype(F32))
    in_specs.append(pl.BlockSpec((b, BLK, hd), q_map))
    args.append(view)
    for off in (k_off, v_off):
        for j in range(nb + 1):
            in_specs.append(pl.BlockSpec((b, BLK, gd), kv_map(off, j)))
            args.append(view)
    in_specs.append(pl.BlockSpec((n_heads, BLK, kw), lambda r, i: (0, 0, 0)))
    args.append(bias)

    out_spec = pl.BlockSpec((b, BLK, hd), lambda r, i: (0, r * n_blk + i, 0))
    out_shape = [jax.ShapeDtypeStruct((b, dil * length, hd), out_dtype)]
    out_specs = [out_spec]
    if want_lse:
        out_shape.append(jax.ShapeDtypeStruct((b, dil * length, hd), F32))
        out_specs.append(out_spec)

    res = pl.pallas_call(
        functools.partial(_banded_kernel, n_heads=n_heads, n_groups=n_groups, nb=nb,
                          has_sinks=sinks is not None, want_lse=want_lse),
        grid=(dil, n_blk),
        in_specs=in_specs,
        out_specs=out_specs,
        out_shape=out_shape,
        compiler_params=_cparams(("parallel", "arbitrary")),
        name="banded_attention",
    )(*args)
    return [r.reshape(b, dil, length, hd) for r in res]


def _dilated_mix_kernel(*refs, n_pat):
    o_refs, l_refs = refs[:n_pat], refs[n_pat:2 * n_pat]
    out_ref = refs[2 * n_pat]
    scratch = refs[2 * n_pat + 1:]

    def natural(ref, buf):
        d = ref.shape[1]
        if d == 1:
            return ref[0, 0]
        rows = ref.shape[2]
        n_chunks = buf.shape[0]
        for r in range(d):
            for c in range(n_chunks):
                buf[c, pl.ds(r, rows, stride=d), :] = ref[0, r, :, c * LANES:(c + 1) * LANES]
        return jnp.concatenate([buf[c] for c in range(n_chunks)], axis=1)

    os_ = [natural(ref, scratch[2 * p]) for p, ref in enumerate(o_refs)]
    ls = [natural(ref, scratch[2 * p + 1]) for p, ref in enumerate(l_refs)]
    m = functools.reduce(jnp.maximum, ls)
    es = [jnp.exp(l - m) for l in ls]
    den = functools.reduce(jnp.add, es)
    acc = functools.reduce(jnp.add, [(e / den) * o for e, o in zip(es, os_)])
    out_ref[...] = acc.astype(out_ref.dtype)


def _dilated_mix(outs, lses, tm):
    b, _, _, c = outs[0].shape
    seq = outs[0].shape[1] * outs[0].shape[2]
    tiles_per_seq = seq // tm

    def spec(a):
        d = a.shape[1]
        return pl.BlockSpec((1, d, tm // d, c), lambda i: (i // tiles_per_seq, 0, i % tiles_per_seq, 0))

    return pl.pallas_call(
        functools.partial(_dilated_mix_kernel, n_pat=len(outs)),
        grid=(b * tiles_per_seq,),
        in_specs=[spec(a) for a in outs] + [spec(a) for a in lses],
        out_specs=pl.BlockSpec((tm, c), lambda i: (i, 0)),
        out_shape=jax.ShapeDtypeStruct((b * seq, c), BF16),
        scratch_shapes=[pltpu.VMEM((c // LANES, tm, LANES), F32) for _ in range(2 * len(outs))],
        compiler_params=_cparams(("parallel",)),
        name="dilated_mix",
    )(*outs, *lses)


def _retention_kernel(q_ref, k_ref, v_ref, g_ref, cos_ref, sin_ref, dmask_ref, qdec_ref, kdec_ref,
                      cdec_ref, gn_ref, o_ref, state_ref):
    @pl.when(pl.program_id(0) == 0)
    def _():
        state_ref[...] = jnp.zeros_like(state_ref)

    cos = cos_ref[...]
    sin = sin_ref[...]
    half = C_QK_DIM // 2

    def rot(x):
        x1, x2 = x[:, :half], x[:, half:]
        return jnp.concatenate([x1 * cos - x2 * sin, x1 * sin + x2 * cos], axis=1)

    for hd in range(C_HEADS):
        qk_cols = slice(hd * C_QK_DIM, (hd + 1) * C_QK_DIM)
        v_cols = slice(hd * C_V_DIM, (hd + 1) * C_V_DIM)
        for bi in range(q_ref.shape[0]):
            q = rot(q_ref[bi, :, qk_cols].astype(F32))
            k = rot(k_ref[bi, :, qk_cols].astype(F32)) * (C_QK_DIM ** -0.5)
            v = v_ref[bi, :, v_cols]
            qb = q.astype(BF16)
            inner = lax.dot_general(qb, k.astype(BF16), (((1,), (1,)), ((), ())),
                                    preferred_element_type=F32) * dmask_ref[hd]
            state = state_ref[bi, hd]
            o = jnp.dot(inner.astype(BF16), v, preferred_element_type=F32)
            o = o + jnp.dot(qb, state.astype(BF16), preferred_element_type=F32) * qdec_ref[hd]
            kd_t = jnp.transpose(k * kdec_ref[hd]).astype(BF16)
            state_ref[bi, hd] = state * cdec_ref[hd] + jnp.dot(kd_t, v, preferred_element_type=F32)

            mu = jnp.mean(o, axis=-1, keepdims=True)
            oc = o - mu
            var = jnp.mean(oc * oc, axis=-1, keepdims=True)
            on = oc * lax.rsqrt(var + 1e-5)
            o_ref[bi, :, v_cols] = (on * gn_ref[:, v_cols]
                                    * jax.nn.silu(g_ref[bi, :, v_cols].astype(F32))).astype(o_ref.dtype)


def _retention(slab, gn, seq):
    b = slab.shape[0]
    n_chunks = seq // C_CHUNK
    half = C_QK_DIM // 2
    pos = jnp.arange(seq, dtype=F32)
    inv = 1.0 / (10000.0 ** (jnp.arange(0, C_QK_DIM, 2, dtype=F32) / C_QK_DIM))
    ang = pos[:, None] * inv[None, :]
    cos, sin = jnp.cos(ang), jnp.sin(ang)
    log_g = jnp.log(1.0 - 2.0 ** (-5.0 - jnp.arange(C_HEADS, dtype=F32)))
    j = jnp.arange(C_CHUNK, dtype=F32)
    diff = j[:, None] - j[None, :]
    dmask = jnp.where(diff >= 0, jnp.exp(diff[None] * log_g[:, None, None]), 0.0)
    q_dec = jnp.exp((j[None, :] + 1.0) * log_g[:, None])[:, :, None]
    k_dec = jnp.exp((C_CHUNK - 1.0 - j[None, :]) * log_g[:, None])[:, :, None]
    chunk_dec = jnp.exp(C_CHUNK * log_g)

    def col(off, w):
        return lambda c: (0, c, off // w)

    const3 = lambda c: (0, 0, 0)
    return pl.pallas_call(
        _retention_kernel,
        grid=(n_chunks,),
        in_specs=[pl.BlockSpec((b, C_CHUNK, C_QK), col(OD_QC, C_QK)),
                  pl.BlockSpec((b, C_CHUNK, C_QK), col(OD_KC, C_QK)),
                  pl.BlockSpec((b, C_CHUNK, C_V), col(OD_VC, C_V)),
                  pl.BlockSpec((b, C_CHUNK, C_V), col(OD_GC, C_V)),
                  pl.BlockSpec((C_CHUNK, half), lambda c: (c, 0)),
                  pl.BlockSpec((C_CHUNK, half), lambda c: (c, 0)),
                  pl.BlockSpec((C_HEADS, C_CHUNK, C_CHUNK), const3),
                  pl.BlockSpec((C_HEADS, C_CHUNK, 1), const3),
                  pl.BlockSpec((C_HEADS, C_CHUNK, 1), const3),
                  pl.BlockSpec(memory_space=pltpu.SMEM),
                  pl.BlockSpec((1, C_V), lambda c: (0, 0))],
        out_specs=pl.BlockSpec((b, C_CHUNK, C_V), lambda c: (0, c, 0)),
        out_shape=jax.ShapeDtypeStruct((b, seq, C_V), BF16),
        scratch_shapes=[pltpu.VMEM((b, C_HEADS, C_QK_DIM, C_V_DIM), F32)],
        compiler_params=_cparams(("arbitrary",)),
        name="retention",
    )(slab, slab, slab, slab, cos, sin, dmask, q_dec, k_dec, chunk_dec, gn.reshape(1, C_V).astype(F32))


def _compress_kernel(x_ref, pos_ref, w1_ref, w2_ref, o_ref):
    x = (x_ref[0].astype(F32) + pos_ref[0]).astype(BF16)
    hid = jax.nn.gelu(jnp.dot(x, w1_ref[0], preferred_element_type=F32))
    o_ref[0] = jnp.dot(hid.astype(BF16), w2_ref[0], preferred_element_type=F32)


def _compress(flat, pos, w1, w2):
    _, bg, rows, width = flat.shape
    tr = min(rows, 256)
    return pl.pallas_call(
        _compress_kernel,
        grid=(2, bg, rows // tr),
        in_specs=[pl.BlockSpec((None, 1, tr, width), lambda s, i, r: (s, i, r, 0)),
                  pl.BlockSpec((1, 1, width), lambda s, i, r: (s, 0, 0)),
                  pl.BlockSpec((1, width, D_CMP_HIDDEN), lambda s, i, r: (s, 0, 0)),
                  pl.BlockSpec((1, D_CMP_HIDDEN, HEAD_DIM), lambda s, i, r: (s, 0, 0))],
        out_specs=pl.BlockSpec((None, 1, tr, HEAD_DIM), lambda s, i, r: (s, i, r, 0)),
        out_shape=jax.ShapeDtypeStruct((2, bg, rows, HEAD_DIM), F32),
        compiler_params=_cparams(("parallel", "parallel", "parallel")),
        name="nsa_compress",
    )(flat, pos, w1, w2)


def _cmp_bias(table):
    tt = jnp.arange(BLK)[:, None]
    m = jnp.arange(CMP_WIN)[None, :] + (BLK // D_CMP_STRIDE) - CMP_WIN
    dist = tt - D_CMP_STRIDE * m - (D_CMP_LEN - 1)
    return jnp.where((dist >= 0)[None], _bias_lookup(table, dist), NEG_INF)


def _sel_matrix():
    c_rel = np.arange(CMP_WIN)[:, None] + (BLK // D_CMP_STRIDE) - CMP_WIN
    j_rel = np.arange(REL_BLOCKS)[None, :] - (REL_BLOCKS - 2)
    return ((c_rel >= 4 * j_rel - 1) & (c_rel <= 4 * j_rel + 3)).astype(np.float32)


def _cmp_attn_kernel(q_ref, kc_ref, vc_ref, bias_ref, sel_ref, o_ref, idx_ref, qs_ref):
    i = pl.program_id(1)
    lane = lax.broadcasted_iota(jnp.int32, (1, LANES), 1)
    lo = lane < HEAD_DIM

    hpg = D_HEADS // D_KV_HEADS
    for pair in range(D_HEADS // 2):
        x = q_ref[0, :, pair * LANES:(pair + 1) * LANES].astype(F32)
        r = pltpu.roll(x, HEAD_DIM, axis=1)
        for par, val in ((0, jnp.where(lo, x, r)), (1, jnp.where(lo, r, x))):
            h = 2 * pair + par
            qs_ref[0, h // hpg, pl.ds(h % hpg, BLK, stride=hpg), :] = val
    half_mask = (jnp.where(lo, SCORE_SCALE, 0.0).astype(BF16), jnp.where(lo, 0.0, SCORE_SCALE).astype(BF16))
    row = lax.broadcasted_iota(jnp.int32, (BLK, REL_BLOCKS), 0)
    jj = lax.broadcasted_iota(jnp.int32, (BLK, REL_BLOCKS), 1)
    cur = (REL_BLOCKS - 2) + (row >= D_SEL_LEN).astype(jnp.int32)
    first = (REL_BLOCKS - 2) - 2 * i
    exists = jj >= first
    forced = exists & ((jj == first) | (jj == cur) | (jj == cur - 1))
    valid = exists & (jj <= cur)
    jjf = jj.astype(F32)
    out_lane = lax.broadcasted_iota(jnp.int32, (BLK, LANES), 1)
    out_row = lax.broadcasted_iota(jnp.int32, (BLK, LANES), 0)
    cur_abs = 2 * i + (out_row >= D_SEL_LEN).astype(jnp.int32)
    tt_in_blk = out_row % D_SEL_LEN

    n_keys = CMP_PER_BLK * (i + 1)
    for width in CMP_WIDTHS:
        lo_w = width - CMP_WIDTHS[0] if width > CMP_WIDTHS[0] else -1

        @pl.when((n_keys <= width) & (n_keys > lo_w))
        def _(width=width):
            _cmp_attn_body(q_ref, kc_ref, vc_ref, bias_ref, sel_ref, o_ref, idx_ref, i, width, lo, half_mask,
                           forced, valid, jjf, first, out_lane, cur_abs, tt_in_blk)


def _cmp_attn_body(q_ref, kc_ref, vc_ref, bias_ref, sel_ref, o_ref, idx_ref, i, width, lo, half_mask,
                   forced, valid, jjf, first, out_lane, cur_abs, tt_in_blk):
    hpg = D_HEADS // D_KV_HEADS
    off = CMP_WIN - width
    n_keys = CMP_PER_BLK * (i + 1)
    start = pl.multiple_of(n_keys + off, CMP_PER_BLK)
    ucol = lax.broadcasted_iota(jnp.int32, (1, width), 1) + off
    edge = jnp.where(ucol < CMP_WIN - n_keys, NEG_INF, 0.0).astype(F32)

    for g in range(D_KV_HEADS):
        kwin = kc_ref[0, g, pl.ds(start, width), :].astype(BF16)
        vwin = vc_ref[0, g, pl.ds(start, width), :].astype(BF16)
        qs = []
        for hh in range(hpg):
            h = g * hpg + hh
            qs.append(q_ref[0, :, (h // 2) * LANES:(h // 2 + 1) * LANES] * half_mask[h % 2])
        s_all = lax.dot_general(jnp.concatenate(qs, axis=0), kwin, (((1,), (1,)), ((), ())),
                                preferred_element_type=F32)
        imp = jnp.zeros((BLK, width), F32)
        pns = []
        for hh in range(hpg):
            h = g * hpg + hh
            s = s_all[hh * BLK:(hh + 1) * BLK] + bias_ref[h, :, off:] + edge
            m = jnp.maximum(jnp.max(s, axis=-1, keepdims=True), ROW_MAX_FLOOR)
            p = jnp.exp(s - m)
            den = jnp.maximum(jnp.sum(p, axis=-1, keepdims=True), 1e-30)
            pn = p * (1.0 / den)
            imp = imp + pn
            pns.append(pn.astype(BF16))
        o_all = jnp.dot(jnp.concatenate(pns, axis=0), vwin, preferred_element_type=F32)
        for pr in range(hpg // 2):
            pair = (g * hpg) // 2 + pr
            o_ref[0, :, pair * LANES:(pair + 1) * LANES] = jnp.where(
                lo, o_all[2 * pr * BLK:(2 * pr + 1) * BLK], o_all[(2 * pr + 1) * BLK:(2 * pr + 2) * BLK]
            ).astype(o_ref.dtype)

        sel = sel_ref[off:, :]
        hi = imp.astype(BF16)
        r1 = imp - hi.astype(F32)
        mid = r1.astype(BF16)
        low = (r1 - mid.astype(F32)).astype(BF16)
        imp_sel = (jnp.dot(hi, sel, preferred_element_type=F32)
                   + jnp.dot(mid, sel, preferred_element_type=F32)
                   + jnp.dot(low, sel, preferred_element_type=F32))
        score = jnp.where(forced, FORCED_SCORE, jnp.where(valid, imp_sel, INVALID_SCORE))
        picked = jnp.zeros((BLK, LANES), jnp.int32)
        for r in range(D_SEL_COUNT):
            m = jnp.max(score, axis=-1, keepdims=True)
            am = jnp.min(jnp.where(score == m, jjf, float(REL_BLOCKS)), axis=-1, keepdims=True)
            none = m < 0.5 * INVALID_SCORE
            blk = am.astype(jnp.int32) - first
            key_blk = jnp.where(none, 0, blk)
            bias_row = jnp.where(none, SEL_NONE, jnp.minimum(cur_abs - blk, SEL_FAR)) * D_SEL_LEN + tt_in_blk
            picked = jnp.where(out_lane == r, key_blk, picked)
            picked = jnp.where(out_lane == D_SEL_COUNT + r, bias_row, picked)
            score = jnp.where(jjf == am, -jnp.inf, score)
        idx_ref[0, g] = picked


def _cmp_attention(slab, kc_pad, vc_pad, table, seq):
    b = slab.shape[0]
    n_blk = seq // BLK
    rows = kc_pad.shape[2]
    hpg = D_HEADS // D_KV_HEADS
    bias = _cmp_bias(table)
    sel = jnp.asarray(_sel_matrix(), BF16)
    return pl.pallas_call(
        _cmp_attn_kernel,
        grid=(b, n_blk),
        in_specs=[pl.BlockSpec((1, BLK, D_Q), lambda bi, i: (bi, i, OD_QD // D_Q)),
                  pl.BlockSpec((1, D_KV_HEADS, rows, LANES), lambda bi, i: (bi, 0, 0, 0)),
                  pl.BlockSpec((1, D_KV_HEADS, rows, LANES), lambda bi, i: (bi, 0, 0, 0)),
                  pl.BlockSpec((D_HEADS, BLK, CMP_WIN), lambda bi, i: (0, 0, 0)),
                  pl.BlockSpec((CMP_WIN, REL_BLOCKS), lambda bi, i: (0, 0))],
        out_specs=[pl.BlockSpec((1, BLK, D_Q), lambda bi, i: (bi, i, 0)),
                   pl.BlockSpec((1, D_KV_HEADS, BLK, LANES), lambda bi, i: (bi, 0, i, 0)),
                   pl.BlockSpec((1, D_KV_HEADS, BLK * hpg, LANES), lambda bi, i: (bi, 0, i, 0))],
        out_shape=[jax.ShapeDtypeStruct((b, seq, D_Q), BF16),
                   jax.ShapeDtypeStruct((b, D_KV_HEADS, seq, LANES), jnp.int32),
                   jax.ShapeDtypeStruct((b, D_KV_HEADS, seq * hpg, LANES), F32)],
        compiler_params=_cparams(("parallel", "arbitrary")),
        name="nsa_cmp_attention",
    )(slab, kc_pad, vc_pad, bias, sel)


def _sel_bias(table):
    hpg = D_HEADS // D_KV_HEADS
    delta = jnp.arange(SEL_FAR)[:, None, None]
    tt = jnp.arange(D_SEL_LEN)[None, :, None]
    l = jnp.arange(D_SEL_LEN)[None, None, :]
    dist = D_SEL_LEN * delta + tt - l
    near = jnp.where((dist >= 0)[None], _bias_lookup(table, dist), NEG_INF)
    far = jnp.broadcast_to(table.astype(F32)[REL_BUCKETS - 1][:, None, None, None],
                           (D_HEADS, 1, D_SEL_LEN, D_SEL_LEN))
    none = jnp.full((D_HEADS, 1, D_SEL_LEN, D_SEL_LEN), NEG_INF, F32)
    rows = jnp.concatenate([near, far, none], axis=1)
    half = D_SEL_LEN // 2
    rows = rows.reshape(D_KV_HEADS, hpg, SEL_NONE + 1, D_SEL_LEN, half, 2)
    rows = jnp.transpose(rows, (0, 2, 3, 5, 1, 4))
    rows = rows.reshape(D_KV_HEADS, (SEL_NONE + 1) * D_SEL_LEN, 2 * hpg, half)
    return jnp.tile(rows, (1, 1, 1, LANES // half))


SEL_IDX = 2 * D_SEL_COUNT
SEL_UNROLL = 16


def _sel_attn_kernel(idx_hbm, q_ref, kv_ref, bias_ref, o_ref, idx_smem, sem):
    n_g, n_i = pl.num_programs(1), pl.num_programs(2)
    step = (pl.program_id(0) * n_g + pl.program_id(1)) * n_i + pl.program_id(2)
    total = pl.num_programs(0) * n_g * n_i
    slot = step % 2

    tile_words = BLK * SEL_IDX

    def idx_copy(s, sl):
        dst = idx_smem.at[pl.ds(pl.multiple_of(sl * tile_words, tile_words), tile_words)]
        return pltpu.make_async_copy(idx_hbm.at[s], dst, sem.at[sl])

    @pl.when(step == 0)
    def _():
        idx_copy(0, 0).start()

    @pl.when(step + 1 < total)
    def _():
        idx_copy(step + 1, 1 - slot).start()

    idx_copy(step, slot).wait()

    lane = lax.broadcasted_iota(jnp.int32, (1, LANES), 1)
    lo = lane < HEAD_DIM
    hpg = D_HEADS // D_KV_HEADS
    quarter = D_SEL_LEN // 2
    def token_scores(tl):
        picks = idx_smem.at[pl.ds(slot * tile_words + tl * SEL_IDX, SEL_IDX)]
        qq = q_ref[0, 0, tl] * SCORE_SCALE
        qbd = jnp.concatenate([jnp.where(lo, qq, 0.0), jnp.where(lo, 0.0, qq)], axis=0).astype(BF16)
        ks, vs, bs = [], [], []
        for n in range(D_SEL_COUNT):
            kv = kv_ref[0, 0, picks[n]]
            ks.append(kv[:quarter])
            vs.append(kv[quarter:])
            bs.append(bias_ref[0, picks[D_SEL_COUNT + n]])
        s = lax.dot_general(qbd, jnp.concatenate(ks, axis=0), (((1,), (1,)), ((), ())),
                            preferred_element_type=F32)
        bias = jnp.concatenate(
            [jnp.where(lane < quarter, bs[c],
                       jnp.where(lane < 2 * quarter, bs[c + 1],
                                 jnp.where(lane < 3 * quarter, bs[c + 2], bs[c + 3])))
             for c in range(0, D_SEL_COUNT, 4)], axis=1)
        return s + bias, jnp.concatenate(vs, axis=0)

    def fold(x16):
        return x16[:hpg], x16[hpg:]

    def body(it, carry):
        t0 = it * SEL_UNROLL
        sv = [token_scores(t0 + u) for u in range(SEL_UNROLL)]
        ps, dens = [], []
        for s, _ in sv:
            m = jnp.maximum(*fold(jnp.max(s, axis=-1, keepdims=True)))
            p = jnp.exp(s - jnp.concatenate([m, m], axis=0))
            dens.append(sum(fold(jnp.sum(p, axis=-1, keepdims=True))))
            ps.append(p.astype(BF16))
        outs = []
        for p, den, (_, vall) in zip(ps, dens, sv):
            o_top, o_bot = fold(jnp.dot(p, vall, preferred_element_type=F32))
            outs.append(jnp.where(lo, o_top, o_bot) * (1.0 / den))
        for u in range(SEL_UNROLL):
            o_ref[0, 0, t0 + u] = outs[u]
        return carry

    lax.fori_loop(0, BLK // SEL_UNROLL, body, 0)


def _sel_attention(idx, q_sel, kv_sel, table, seq):
    b = q_sel.shape[0]
    hpg = D_HEADS // D_KV_HEADS
    n_blk = seq // BLK
    n_sb = seq // D_SEL_LEN
    bias = _sel_bias(table)
    kv_spec = pl.BlockSpec((1, 1, n_sb, D_SEL_LEN, LANES), lambda bi, g, i: (bi, g, 0, 0, 0))
    return pl.pallas_call(
        _sel_attn_kernel,
        grid=(b, D_KV_HEADS, n_blk),
        in_specs=[pl.BlockSpec(memory_space=pl.ANY),
                  pl.BlockSpec((1, 1, BLK, hpg, LANES), lambda bi, g, i: (bi, g, i, 0, 0)),
                  kv_spec,
                  pl.BlockSpec((1, (SEL_NONE + 1) * D_SEL_LEN, 2 * hpg, LANES), lambda bi, g, i: (g, 0, 0, 0))],
        out_specs=pl.BlockSpec((1, 1, BLK, hpg, LANES), lambda bi, g, i: (bi, g, i, 0, 0)),
        out_shape=jax.ShapeDtypeStruct((b, D_KV_HEADS, seq, hpg, LANES), F32),
        scratch_shapes=[pltpu.SMEM((2 * BLK * SEL_IDX,), jnp.int32), pltpu.SemaphoreType.DMA((2,))],
        compiler_params=_cparams(("arbitrary", "arbitrary", "arbitrary")),
        name="nsa_sel_attention",
    )(idx, q_sel, kv_sel, bias)


def _gate_expand():
    e = np.zeros((3, LANES, D_Q), np.float32)
    for h in range(D_HEADS):
        for c in range(3):
            e[c, 3 * h + c, h * HEAD_DIM:(h + 1) * HEAD_DIM] = 1.0
    return e


def _head_place():
    hpg = D_HEADS // D_KV_HEADS
    p = np.zeros((hpg * LANES, hpg * HEAD_DIM), np.float32)
    for hh in range(hpg):
        for lane in range(LANES):
            p[hh * LANES + lane, hh * HEAD_DIM + lane % HEAD_DIM] = 1.0
    return p


def _nsa_gate_kernel(gd_ref, e_ref, place_ref, oc_ref, os_ref, ow_ref, out_ref):
    hpg = D_HEADS // D_KV_HEADS
    tm = out_ref.shape[0]
    groups = []
    for g in range(D_KV_HEADS):
        rows = jnp.concatenate([os_ref[0, g, pl.ds(hh, tm, stride=hpg), :].astype(BF16) for hh in range(hpg)],
                               axis=1)
        groups.append(jnp.dot(rows, place_ref[...], preferred_element_type=F32))
    o_s = jnp.concatenate(groups, axis=1)

    sg = jax.nn.sigmoid(gd_ref[...].astype(F32))
    hi = sg.astype(BF16)
    low = (sg - hi.astype(F32)).astype(BF16)
    acc = None
    for c, branch in enumerate((oc_ref[...].astype(F32), o_s, ow_ref[...].astype(F32))):
        gate = (jnp.dot(hi, e_ref[c], preferred_element_type=F32)
                + jnp.dot(low, e_ref[c], preferred_element_type=F32))
        acc = gate * branch if acc is None else acc + gate * branch
    out_ref[...] = acc.astype(out_ref.dtype)


def _nsa_gate(slab2d, o_c, o_s_raw, o_w, tm):
    m = slab2d.shape[0]
    b, g_kv, seq, hpg, _ = o_s_raw.shape
    tiles_per_seq = seq // tm
    e = jnp.asarray(_gate_expand(), BF16)
    place = jnp.asarray(_head_place(), BF16)
    spec = pl.BlockSpec((tm, D_Q), lambda i: (i, 0))
    return pl.pallas_call(
        _nsa_gate_kernel,
        grid=(m // tm,),
        in_specs=[pl.BlockSpec((tm, LANES), lambda i: (i, OD_GD // LANES)),
                  pl.BlockSpec((3, LANES, D_Q), lambda i: (0, 0, 0)),
                  pl.BlockSpec((hpg * LANES, hpg * HEAD_DIM), lambda i: (0, 0)),
                  spec,
                  pl.BlockSpec((1, g_kv, tm * hpg, LANES),
                               lambda i: (i // tiles_per_seq, 0, i % tiles_per_seq, 0)),
                  spec],
        out_specs=spec,
        out_shape=jax.ShapeDtypeStruct((m, D_Q), BF16),
        compiler_params=_cparams(("parallel",)),
        name="nsa_gate",
    )(slab2d, e, place, o_c, o_s_raw.reshape(b, g_kv, seq * hpg, LANES), o_w)


def _even_mixer(h, nw, w_in, sinks, w_out, rel_table, b, seq):
    qa, ka, va, qb, kb, vb = jnp.split(w_in, [int(c) for c in np.cumsum([A_Q, A_KV, A_KV, B_W, B_W])], axis=1)
    pad = jnp.zeros((D_MODEL, EVEN_SLAB - w_in.shape[1]), w_in.dtype)
    w_slab = jnp.concatenate([qa, qb, kb, vb, ka, va, pad], axis=1).astype(BF16)
    dilations = [dil for _, dil in B_PATTERNS if dil > 1]
    slabs = _norm_matmul(h, nw, w_slab, TM_IN_PROJ, TN_EVEN, dilations, seq)
    by_dil = {1: slabs[0].reshape(b, 1, seq, EVEN_SLAB)}
    by_dil.update(zip(dilations, slabs[1:]))

    (oa,) = _banded_attention(by_dil[1], q_off=EV_QA, k_off=EV_KA, v_off=EV_VA,
                              n_heads=A_HEADS, n_groups=A_KV_HEADS, max_dist=A_WINDOW - 1,
                              table=rel_table[:, :A_HEADS], sinks=sinks)
    outs, lses = [], []
    for window, dil in B_PATTERNS:
        o, lse = _banded_attention(by_dil[dil], q_off=EV_QB, k_off=EV_KB, v_off=EV_VB,
                                   n_heads=B_HEADS, n_groups=B_HEADS, max_dist=window // dil,
                                   table=rel_table[:, A_HEADS:A_HEADS + B_HEADS], want_lse=True, out_dtype=F32)
        outs.append(o)
        lses.append(lse)
    ob = _dilated_mix(outs, lses, TM_MIX)
    w_out = w_out.astype(BF16)
    return _proj_residual(h, [oa.reshape(b * seq, A_Q), ob], [w_out[:A_Q], w_out[A_Q:]], TM_OUT_PROJ)


def _nsa(slab, pos_k, pos_v, k_w1, k_w2, v_w1, v_w2, rel_table, b, seq):
    g_kv = D_KV_HEADS
    hpg = D_HEADS // g_kv
    slab2d = slab.reshape(b * seq, ODD_SLAB)
    n_rows = seq // D_CMP_STRIDE

    def rows16(off):
        a = slab[:, :, off:off + D_KV].reshape(b, n_rows, D_CMP_STRIDE, g_kv, HEAD_DIM)
        a = jnp.transpose(a, (0, 3, 1, 2, 4)).reshape(b * g_kv, n_rows, D_CMP_STRIDE * HEAD_DIM)
        nxt = jnp.concatenate([a[:, 1:], jnp.zeros_like(a[:, :1])], axis=1)
        return jnp.concatenate([a, nxt], axis=-1)

    flat = jnp.stack([rows16(OD_KCMP), rows16(OD_VCMP)])
    pos = jnp.stack([pos_k.reshape(1, -1), pos_v.reshape(1, -1)]).astype(F32)
    w1 = jnp.stack([k_w1, v_w1]).astype(BF16)
    w2 = jnp.stack([k_w2, v_w2]).astype(BF16)
    cmp = _compress(flat, pos, w1, w2).reshape(2, b, g_kv, n_rows, HEAD_DIM)
    cmp = jnp.pad(cmp, ((0, 0), (0, 0), (0, 0), (CMP_WIN, 0), (0, 0)))
    cmp = jnp.concatenate([cmp, cmp], axis=-1)
    o_c, idx, q_sel = _cmp_attention(slab, cmp[0], cmp[1], rel_table, seq)
    q_sel = q_sel.reshape(b, g_kv, seq, hpg, LANES)

    idx = idx[..., :SEL_IDX].reshape(b * g_kv * (seq // BLK), BLK * SEL_IDX)

    def per_group(off):
        return slab[:, :, off:off + D_KV].reshape(b, seq, g_kv, HEAD_DIM)

    def two_per_row(off):
        a = jnp.transpose(per_group(off), (0, 2, 1, 3))
        return a.reshape(b, g_kv, seq // D_SEL_LEN, D_SEL_LEN // 2, LANES)

    kv_sel = jnp.concatenate([two_per_row(OD_KSLC), two_per_row(OD_VSLC)], axis=3)
    o_s = _sel_attention(idx, q_sel, kv_sel, rel_table, seq)

    (o_w,) = _banded_attention(slab.reshape(b, 1, seq, ODD_SLAB), q_off=OD_QD, k_off=OD_KWIN, v_off=OD_VWIN,
                               n_heads=D_HEADS, n_groups=D_KV_HEADS, max_dist=D_WINDOW - 1, table=rel_table)
    return _nsa_gate(slab2d, o_c.reshape(b * seq, D_Q), o_s, o_w.reshape(b * seq, D_Q), TM_GATE)


def _odd_mixer(h, nw, w_in, ret_gn, pos_k, pos_v, k_w1, k_w2, v_w1, v_w2, w_out, rel_table, b, seq):
    pad = jnp.zeros((D_MODEL, ODD_SLAB - ODD_IN), w_in.dtype)
    w_slab = jnp.concatenate([w_in, pad], axis=1).astype(BF16)
    slab = _norm_matmul(h, nw, w_slab, TM_IN_PROJ, TN_ODD).reshape(b, seq, ODD_SLAB)
    oc = _retention(slab, ret_gn, seq)
    od = _nsa(slab, pos_k, pos_v, k_w1, k_w2, v_w1, v_w2, rel_table, b, seq)
    w_out = w_out.astype(BF16)
    return _proj_residual(h, [oc.reshape(b * seq, C_V), od], [w_out[:C_V], w_out[C_V:]], TM_OUT_PROJ)


def kernel(x, rel_table, norm_mix, norm_ffn, norm_final, even_w_in, even_sinks, even_w_out, odd_w_in, odd_ret_gn, odd_cmp_pos_k, odd_cmp_pos_v, odd_cmp_k_w1, odd_cmp_k_w2, odd_cmp_v_w1, odd_cmp_v_w2, odd_w_out, ffn_w_gate, ffn_w_up, ffn_w_down):
    b, seq, d = x.shape
    h = x.reshape(b * seq, d)
    for layer in range(DEPTH):
        li = layer // 2
        if layer % 2 == 0:
            h = _even_mixer(h, norm_mix[layer], even_w_in[li], even_sinks[li], even_w_out[li], rel_table, b, seq)
        else:
            h = _odd_mixer(h, norm_mix[layer], odd_w_in[li], odd_ret_gn[li], odd_cmp_pos_k[li],
                           odd_cmp_pos_v[li], odd_cmp_k_w1[li], odd_cmp_k_w2[li], odd_cmp_v_w1[li],
                           odd_cmp_v_w2[li], odd_w_out[li], rel_table, b, seq)
        h = _ffn(h, norm_ffn[layer], ffn_w_gate[layer].astype(BF16), ffn_w_up[layer].astype(BF16),
                 ffn_w_down[layer].astype(BF16), norm_final, layer == DEPTH - 1, TM_FFN)
    return h.reshape(b, seq, d)
```

```python
import functools
import math

import numpy as np
import jax
import jax.numpy as jnp
from jax import lax
from jax.experimental import pallas as pl
from jax.experimental.pallas import tpu as pltpu

F32 = jnp.float32
BF16 = jnp.bfloat16

D_MODEL = 1024
DEPTH = 4
HEAD_DIM = 64
BLK = 128
NEG_INF = -1e30
REL_BUCKETS = 32
REL_MAX_DIST = 2048
A_HEADS = 8
A_KV_HEADS = 2
A_WINDOW = 128
B_HEADS = 8
B_PATTERNS = ((128, 1), (512, 4), (2048, 16))
C_HEADS = 4
C_QK_DIM = 256
C_V_DIM = 512
C_CHUNK = 128
D_HEADS = 16
D_KV_HEADS = 2
D_CMP_LEN = 32
D_CMP_STRIDE = 16
D_CMP_HIDDEN = 128
D_SEL_LEN = 64
D_SEL_COUNT = 16
D_WINDOW = 512
D_FF = 2816

A_Q = A_HEADS * HEAD_DIM
A_KV = A_KV_HEADS * HEAD_DIM
B_W = B_HEADS * HEAD_DIM
C_QK = C_HEADS * C_QK_DIM
C_V = C_HEADS * C_V_DIM
D_Q = D_HEADS * HEAD_DIM
D_KV = D_KV_HEADS * HEAD_DIM
ODD_IN = 2 * C_QK + 2 * C_V + D_Q + 6 * D_KV + 3 * D_HEADS

SCORE_SCALE = HEAD_DIM ** -0.5
LANES = 128
VMEM_LIMIT = 56 * 1024 * 1024

EVEN_SLAB = 2560
EV_QA, EV_QB, EV_KB, EV_VB, EV_KA, EV_VA = 0, 512, 1024, 1536, 2048, 2176
ODD_SLAB = 8192
OD_QC, OD_KC, OD_VC, OD_GC, OD_QD = 0, 1024, 2048, 4096, 6144
OD_KCMP, OD_VCMP, OD_KSLC, OD_VSLC, OD_KWIN, OD_VWIN, OD_GD = 7168, 7296, 7424, 7552, 7680, 7808, 7936

TM_IN_PROJ = 1024
TN_EVEN = EVEN_SLAB // 2
TN_ODD = ODD_SLAB // 4
TM_OUT_PROJ = 512
TM_FFN = 512
TM_MIX = 1024
TM_GATE = 512

FORCED_SCORE = 1e9
INVALID_SCORE = -1e9
ROW_MAX_FLOOR = 0.1 * NEG_INF
CMP_PER_BLK = BLK // D_CMP_STRIDE

SEL_FAR = 25
SEL_NONE = 26
CMP_WIN = 1024
CMP_WIDTHS = (256, 512, 768, 1024)
REL_BLOCKS = 256


def _cparams(sem):
    return pltpu.CompilerParams(dimension_semantics=sem, vmem_limit_bytes=VMEM_LIMIT)


def _t5_bucket(dist):
    max_exact = REL_BUCKETS // 2
    d = jnp.maximum(dist, 0)
    df = jnp.maximum(d, 1).astype(jnp.float32)
    large = max_exact + (jnp.log(df / max_exact) / math.log(REL_MAX_DIST / max_exact)
                         * (REL_BUCKETS - max_exact)).astype(jnp.int32)
    large = jnp.minimum(large, REL_BUCKETS - 1)
    return jnp.where(d < max_exact, d, large)


def _bias_lookup(table, dist):
    bucket = _t5_bucket(dist)[None]
    tab = table.astype(F32)
    expand = (slice(None),) + (None,) * dist.ndim
    out = jnp.zeros((tab.shape[1],) + dist.shape, F32)
    for b in range(REL_BUCKETS):
        out = jnp.where(bucket == b, tab[b][expand], out)
    return out


def _rms(x, w, eps=1e-6):
    return x * lax.rsqrt(jnp.mean(x * x, axis=-1, keepdims=True) + eps) * w


def _norm_matmul_kernel(h_ref, nw_ref, w_ref, o_ref, *rest, dilations):
    @pl.when(pl.program_id(1) == 0)
    def _():
        rest[-1][...] = _rms(h_ref[...], nw_ref[...]).astype(BF16)

    acc = jnp.dot(rest[-1][...], w_ref[...], preferred_element_type=F32)
    o_ref[...] = acc.astype(o_ref.dtype)
    if dilations:
        acc_ref = rest[-2]
        n_chunks = acc_ref.shape[0]
        for c in range(n_chunks):
            acc_ref[c] = acc[:, c * LANES:(c + 1) * LANES]
        for d, ref in zip(dilations, rest):
            rows = acc_ref.shape[1] // d
            for r in range(d):
                ref[0, r] = jnp.concatenate(
                    [acc_ref[c, pl.ds(r, rows, stride=d), :] for c in range(n_chunks)], axis=1).astype(ref.dtype)


def _norm_matmul(h, nw, w, tm, tn, dilations=(), seq=None):
    m, d_model = h.shape
    n = w.shape[1]
    out_specs = [pl.BlockSpec((tm, tn), lambda i, j: (i, j))]
    out_shape = [jax.ShapeDtypeStruct((m, n), BF16)]
    scratch = []
    if dilations:
        tiles_per_seq = seq // tm
        for d in dilations:
            out_specs.append(pl.BlockSpec((1, d, tm // d, tn),
                                          lambda i, j: (i // tiles_per_seq, 0, i % tiles_per_seq, j)))
            out_shape.append(jax.ShapeDtypeStruct((m // seq, d, seq // d, n), BF16))
        scratch.append(pltpu.VMEM((tn // LANES, tm, LANES), F32))
    scratch.append(pltpu.VMEM((tm, d_model), BF16))
    res = pl.pallas_call(
        functools.partial(_norm_matmul_kernel, dilations=tuple(dilations)),
        grid=(m // tm, n // tn),
        in_specs=[pl.BlockSpec((tm, d_model), lambda i, j: (i, 0)),
                  pl.BlockSpec((1, d_model), lambda i, j: (0, 0)),
                  pl.BlockSpec((d_model, tn), lambda i, j: (0, j))],
        out_specs=out_specs,
        out_shape=out_shape,
        scratch_shapes=scratch,
        compiler_params=_cparams(("parallel", "arbitrary")),
        name="norm_matmul",
    )(h, nw.reshape(1, d_model), w)
    return res if dilations else res[0]


def _proj_residual_kernel(*refs, n_in):
    h_ref, out_ref = refs[0], refs[-1]
    acc = h_ref[...]
    for o_ref, w_ref in zip(refs[1:1 + n_in], refs[1 + n_in:1 + 2 * n_in]):
        acc = acc + jnp.dot(o_ref[...], w_ref[...], preferred_element_type=F32)
    out_ref[...] = acc


def _proj_residual(h, outs, ws, tm):
    m, d = h.shape
    n_in = len(outs)
    in_specs = [pl.BlockSpec((tm, d), lambda i: (i, 0))]
    in_specs += [pl.BlockSpec((tm, o.shape[1]), lambda i: (i, 0)) for o in outs]
    in_specs += [pl.BlockSpec(w.shape, lambda i: (0, 0)) for w in ws]
    return pl.pallas_call(
        functools.partial(_proj_residual_kernel, n_in=n_in),
        grid=(m // tm,),
        in_specs=in_specs,
        out_specs=pl.BlockSpec((tm, d), lambda i: (i, 0)),
        out_shape=jax.ShapeDtypeStruct((m, d), F32),
        compiler_params=_cparams(("parallel",)),
        name="proj_residual",
    )(h, *outs, *ws)


def _ffn_kernel(h_ref, nw_ref, wg_ref, wu_ref, wd_ref, fw_ref, o_ref, *, final):
    h = h_ref[...]
    hn = _rms(h, nw_ref[...]).astype(BF16)
    g = jnp.dot(hn, wg_ref[...], preferred_element_type=F32)
    u = jnp.dot(hn, wu_ref[...], preferred_element_type=F32)
    a = (jax.nn.silu(g) * u).astype(BF16)
    y = h + jnp.dot(a, wd_ref[...], preferred_element_type=F32)
    if final:
        y = _rms(y, fw_ref[...])
    o_ref[...] = y


def _ffn(h, nw, wg, wu, wd, fw, final, tm):
    m, d = h.shape
    ff = wg.shape[1]
    resident = dict(pipeline_mode=pl.Buffered(1))
    return pl.pallas_call(
        functools.partial(_ffn_kernel, final=final),
        grid=(m // tm,),
        in_specs=[pl.BlockSpec((tm, d), lambda i: (i, 0)),
                  pl.BlockSpec((1, d), lambda i: (0, 0)),
                  pl.BlockSpec((d, ff), lambda i: (0, 0), **resident),
                  pl.BlockSpec((d, ff), lambda i: (0, 0), **resident),
                  pl.BlockSpec((ff, d), lambda i: (0, 0), **resident),
                  pl.BlockSpec((1, d), lambda i: (0, 0))],
        out_specs=pl.BlockSpec((tm, d), lambda i: (i, 0)),
        out_shape=jax.ShapeDtypeStruct((m, d), F32),
        compiler_params=_cparams(("parallel",)),
        name="ffn",
    )(h, nw.reshape(1, d), wg, wu, wd, fw.reshape(1, d))


def _band_bias(table, max_dist, dist_scale, nb):
    kw = (nb + 1) * BLK
    rel = jnp.arange(BLK)[:, None] + nb * BLK - jnp.arange(kw)[None, :]
    band = (rel >= 0) & (rel <= max_dist)
    return jnp.where(band[None], _bias_lookup(table, rel * dist_scale), NEG_INF)


def _swap_halves(x):
    return jnp.concatenate([x[:, HEAD_DIM:], x[:, :HEAD_DIM]], axis=1)


def _banded_kernel(*refs, n_heads, n_groups, nb, has_sinks, want_lse):
    pos = 0
    if has_sinks:
        sink_ref = refs[0]
        pos = 1
    q_ref = refs[pos]
    k_refs = refs[pos + 1:pos + 2 + nb]
    v_refs = refs[pos + 2 + nb:pos + 3 + 2 * nb]
    bias_ref = refs[pos + 3 + 2 * nb]
    o_ref = refs[pos + 4 + 2 * nb]
    lse_ref = refs[pos + 5 + 2 * nb] if want_lse else None

    i = pl.program_id(1)
    n_batch = q_ref.shape[0]
    kw = (nb + 1) * BLK
    hpg = n_heads // n_groups
    lane = lax.broadcasted_iota(jnp.int32, (1, LANES), 1)
    lo = lane < HEAD_DIM
    half_mask = (jnp.where(lo, SCORE_SCALE, 0.0).astype(BF16), jnp.where(lo, 0.0, SCORE_SCALE).astype(BF16))
    col = lax.broadcasted_iota(jnp.int32, (1, kw), 1)
    edge = jnp.where(col < (nb - i) * BLK, NEG_INF, 0.0).astype(F32)

    operands = []
    for bi in range(n_batch):
        kcat = jnp.concatenate([k_refs[nb - jj][bi] for jj in range(nb + 1)], axis=0)
        vcat = jnp.concatenate([v_refs[nb - jj][bi] for jj in range(nb + 1)], axis=0)
        if hpg == 1:
            for p in range(n_heads // 2):
                operands.append((bi, kcat[:, p * LANES:(p + 1) * LANES], vcat[:, p * LANES:(p + 1) * LANES],
                                 [2 * p, 2 * p + 1]))
        else:
            k_sw = _swap_halves(kcat)
            v_sw = _swap_halves(vcat)
            for g in range(n_groups):
                for par in range(2):
                    heads = [h for h in range(g * hpg, (g + 1) * hpg) if h % 2 == par]
                    operands.append((bi, kcat if g == par else k_sw, vcat if g == par else v_sw, heads))

    scores = []
    for bi, kh, _, heads in operands:
        qz = jnp.concatenate([q_ref[bi, :, (h // 2) * LANES:(h // 2 + 1) * LANES] * half_mask[h % 2]
                              for h in heads], axis=0)
        scores.append(lax.dot_general(qz, kh, (((1,), (1,)), ((), ())), preferred_element_type=F32))
    ms, dens, probs = {}, {}, []
    for (bi, _, _, heads), s_all in zip(operands, scores):
        ps = []
        for r, h in enumerate(heads):
            s = s_all[r * BLK:(r + 1) * BLK] + bias_ref[h] + edge
            m = jnp.max(s, axis=-1, keepdims=True)
            if has_sinks:
                m = jnp.maximum(m, sink_ref[h])
            p = jnp.exp(s - m)
            den = jnp.sum(p, axis=-1, keepdims=True)
            if has_sinks:
                den = den + jnp.exp(sink_ref[h] - m)
            ps.append(p.astype(BF16))
            ms[bi, h], dens[bi, h] = m, den
        probs.append(jnp.concatenate(ps, axis=0))
    outs = {}
    for (bi, _, vh, heads), p_all in zip(operands, probs):
        o_all = jnp.dot(p_all, vh, preferred_element_type=F32)
        for r, h in enumerate(heads):
            outs[bi, h] = o_all[r * BLK:(r + 1) * BLK]
    for bi in range(n_batch):
        for pair in range(n_heads // 2):
            h0, h1 = (bi, 2 * pair), (bi, 2 * pair + 1)
            inv = jnp.where(lo, 1.0 / dens[h0], 1.0 / dens[h1])
            o_pair = jnp.where(lo, outs[h0], outs[h1]) * inv
            o_ref[bi, :, pair * LANES:(pair + 1) * LANES] = o_pair.astype(o_ref.dtype)
            if want_lse:
                lse_ref[bi, :, pair * LANES:(pair + 1) * LANES] = jnp.where(
                    lo, ms[h0] + jnp.log(dens[h0]), ms[h1] + jnp.log(dens[h1]))


def _banded_attention(slab, *, q_off, k_off, v_off, n_heads, n_groups,
                      max_dist, table, sinks=None, want_lse=False, out_dtype=BF16):
    b, dil, length, width = slab.shape
    n_blk = length // BLK
    nb = -(-max_dist // BLK)
    kw = (nb + 1) * BLK
    hd = n_heads * HEAD_DIM
    gd = n_groups * HEAD_DIM
    view = slab.reshape(b, dil * length, width)
    bias = _band_bias(table, max_dist, dil, nb)

    def q_map(r, i):
        return (0, r * n_blk + i, q_off // hd)

    def kv_map(off, j):
        return lambda r, i: (0, r * n_blk + jnp.maximum(i - j, 0), off // gd)

    in_specs, args = [], []
    if sinks is not None:
        in_specs.append(pl.BlockSpec(memory_space=pltpu.SMEM))
        args.append(sinks.astype(F32))
    in_specs.append(pl.BlockSpec((b, BLK, hd), q_map))
    args.append(view)
    for off in (k_off, v_off):
        for j in range(nb + 1):
            in_specs.append(pl.BlockSpec((b, BLK, gd), kv_map(off, j)))
            args.append(view)
    in_specs.append(pl.BlockSpec((n_heads, BLK, kw), lambda r, i: (0, 0, 0)))
    args.append(bias)

    out_spec = pl.BlockSpec((b, BLK, hd), lambda r, i: (0, r * n_blk + i, 0))
    out_shape = [jax.ShapeDtypeStruct((b, dil * length, hd), out_dtype)]
    out_specs = [out_spec]
    if want_lse:
        out_shape.append(jax.ShapeDtypeStruct((b, dil * length, hd), F32))
        out_specs.append(out_spec)

    res = pl.pallas_call(
        functools.partial(_banded_kernel, n_heads=n_heads, n_groups=n_groups, nb=nb,
                          has_sinks=sinks is not None, want_lse=want_lse),
        grid=(dil, n_blk),
        in_specs=in_specs,
        out_specs=out_specs,
        out_shape=out_shape,
        compiler_params=_cparams(("parallel", "arbitrary")),
        name="banded_attention",
    )(*args)
    return [r.reshape(b, dil, length, hd) for r in res]


def _dilated_mix_kernel(*refs, n_pat):
    o_refs, l_refs = refs[:n_pat], refs[n_pat:2 * n_pat]
    out_ref = refs[2 * n_pat]
    scratch = refs[2 * n_pat + 1:]

    def natural(ref, buf):
        d = ref.shape[1]
        if d == 1:
            return ref[0, 0]
        rows = ref.shape[2]
        n_chunks = buf.shape[0]
        for r in range(d):
            for c in range(n_chunks):
                buf[c, pl.ds(r, rows, stride=d), :] = ref[0, r, :, c * LANES:(c + 1) * LANES]
        return jnp.concatenate([buf[c] for c in range(n_chunks)], axis=1)

    os_ = [natural(ref, scratch[2 * p]) for p, ref in enumerate(o_refs)]
    ls = [natural(ref, scratch[2 * p + 1]) for p, ref in enumerate(l_refs)]
    m = functools.reduce(jnp.maximum, ls)
    es = [jnp.exp(l - m) for l in ls]
    den = functools.reduce(jnp.add, es)
    acc = functools.reduce(jnp.add, [(e / den) * o for e, o in zip(es, os_)])
    out_ref[...] = acc.astype(out_ref.dtype)


def _dilated_mix(outs, lses, tm):
    b, _, _, c = outs[0].shape
    seq = outs[0].shape[1] * outs[0].shape[2]
    tiles_per_seq = seq // tm

    def spec(a):
        d = a.shape[1]
        return pl.BlockSpec((1, d, tm // d, c), lambda i: (i // tiles_per_seq, 0, i % tiles_per_seq, 0))

    return pl.pallas_call(
        functools.partial(_dilated_mix_kernel, n_pat=len(outs)),
        grid=(b * tiles_per_seq,),
        in_specs=[spec(a) for a in outs] + [spec(a) for a in lses],
        out_specs=pl.BlockSpec((tm, c), lambda i: (i, 0)),
        out_shape=jax.ShapeDtypeStruct((b * seq, c), BF16),
        scratch_shapes=[pltpu.VMEM((c // LANES, tm, LANES), F32) for _ in range(2 * len(outs))],
        compiler_params=_cparams(("parallel",)),
        name="dilated_mix",
    )(*outs, *lses)


def _retention_kernel(q_ref, k_ref, v_ref, g_ref, cos_ref, sin_ref, dmask_ref, qdec_ref, kdec_ref,
                      cdec_ref, gn_ref, o_ref, state_ref):
    @pl.when(pl.program_id(0) == 0)
    def _():
        state_ref[...] = jnp.zeros_like(state_ref)

    cos = cos_ref[...]
    sin = sin_ref[...]
    half = C_QK_DIM // 2

    def rot(x):
        x1, x2 = x[:, :half], x[:, half:]
        return jnp.concatenate([x1 * cos - x2 * sin, x1 * sin + x2 * cos], axis=1)

    for hd in range(C_HEADS):
        qk_cols = slice(hd * C_QK_DIM, (hd + 1) * C_QK_DIM)
        v_cols = slice(hd * C_V_DIM, (hd + 1) * C_V_DIM)
        for bi in range(q_ref.shape[0]):
            q = rot(q_ref[bi, :, qk_cols].astype(F32))
            k = rot(k_ref[bi, :, qk_cols].astype(F32)) * (C_QK_DIM ** -0.5)
            v = v_ref[bi, :, v_cols]
            qb = q.astype(BF16)
            inner = lax.dot_general(qb, k.astype(BF16), (((1,), (1,)), ((), ())),
                                    preferred_element_type=F32) * dmask_ref[hd]
            state = state_ref[bi, hd]
            o = jnp.dot(inner.astype(BF16), v, preferred_element_type=F32)
            o = o + jnp.dot(qb, state.astype(BF16), preferred_element_type=F32) * qdec_ref[hd]
            kd_t = jnp.transpose(k * kdec_ref[hd]).astype(BF16)
            state_ref[bi, hd] = state * cdec_ref[hd] + jnp.dot(kd_t, v, preferred_element_type=F32)

            mu = jnp.mean(o, axis=-1, keepdims=True)
            oc = o - mu
            var = jnp.mean(oc * oc, axis=-1, keepdims=True)
            on = oc * lax.rsqrt(var + 1e-5)
            o_ref[bi, :, v_cols] = (on * gn_ref[:, v_cols]
                                    * jax.nn.silu(g_ref[bi, :, v_cols].astype(F32))).astype(o_ref.dtype)


def _retention(slab, gn, seq):
    b = slab.shape[0]
    n_chunks = seq // C_CHUNK
    half = C_QK_DIM // 2
    pos = jnp.arange(seq, dtype=F32)
    inv = 1.0 / (10000.0 ** (jnp.arange(0, C_QK_DIM, 2, dtype=F32) / C_QK_DIM))
    ang = pos[:, None] * inv[None, :]
    cos, sin = jnp.cos(ang), jnp.sin(ang)
    log_g = jnp.log(1.0 - 2.0 ** (-5.0 - jnp.arange(C_HEADS, dtype=F32)))
    j = jnp.arange(C_CHUNK, dtype=F32)
    diff = j[:, None] - j[None, :]
    dmask = jnp.where(diff >= 0, jnp.exp(diff[None] * log_g[:, None, None]), 0.0)
    q_dec = jnp.exp((j[None, :] + 1.0) * log_g[:, None])[:, :, None]
    k_dec = jnp.exp((C_CHUNK - 1.0 - j[None, :]) * log_g[:, None])[:, :, None]
    chunk_dec = jnp.exp(C_CHUNK * log_g)

    def col(off, w):
        return lambda c: (0, c, off // w)

    const3 = lambda c: (0, 0, 0)
    return pl.pallas_call(
        _retention_kernel,
        grid=(n_chunks,),
        in_specs=[pl.BlockSpec((b, C_CHUNK, C_QK), col(OD_QC, C_QK)),
                  pl.BlockSpec((b, C_CHUNK, C_QK), col(OD_KC, C_QK)),
                  pl.BlockSpec((b, C_CHUNK, C_V), col(OD_VC, C_V)),
                  pl.BlockSpec((b, C_CHUNK, C_V), col(OD_GC, C_V)),
                  pl.BlockSpec((C_CHUNK, half), lambda c: (c, 0)),
                  pl.BlockSpec((C_CHUNK, half), lambda c: (c, 0)),
                  pl.BlockSpec((C_HEADS, C_CHUNK, C_CHUNK), const3),
                  pl.BlockSpec((C_HEADS, C_CHUNK, 1), const3),
                  pl.BlockSpec((C_HEADS, C_CHUNK, 1), const3),
                  pl.BlockSpec(memory_space=pltpu.SMEM),
                  pl.BlockSpec((1, C_V), lambda c: (0, 0))],
        out_specs=pl.BlockSpec((b, C_CHUNK, C_V), lambda c: (0, c, 0)),
        out_shape=jax.ShapeDtypeStruct((b, seq, C_V), BF16),
        scratch_shapes=[pltpu.VMEM((b, C_HEADS, C_QK_DIM, C_V_DIM), F32)],
        compiler_params=_cparams(("arbitrary",)),
        name="retention",
    )(slab, slab, slab, slab, cos, sin, dmask, q_dec, k_dec, chunk_dec, gn.reshape(1, C_V).astype(F32))


def _compress_kernel(x_ref, pos_ref, w1_ref, w2_ref, o_ref):
    x = (x_ref[0].astype(F32) + pos_ref[0]).astype(BF16)
    hid = jax.nn.gelu(jnp.dot(x, w1_ref[0], preferred_element_type=F32))
    o_ref[0] = jnp.dot(hid.astype(BF16), w2_ref[0], preferred_element_type=F32)


def _compress(flat, pos, w1, w2):
    _, bg, rows, width = flat.shape
    tr = min(rows, 256)
    return pl.pallas_call(
        _compress_kernel,
        grid=(2, bg, rows // tr),
        in_specs=[pl.BlockSpec((None, 1, tr, width), lambda s, i, r: (s, i, r, 0)),
                  pl.BlockSpec((1, 1, width), lambda s, i, r: (s, 0, 0)),
                  pl.BlockSpec((1, width, D_CMP_HIDDEN), lambda s, i, r: (s, 0, 0)),
                  pl.BlockSpec((1, D_CMP_HIDDEN, HEAD_DIM), lambda s, i, r: (s, 0, 0))],
        out_specs=pl.BlockSpec((None, 1, tr, HEAD_DIM), lambda s, i, r: (s, i, r, 0)),
        out_shape=jax.ShapeDtypeStruct((2, bg, rows, HEAD_DIM), F32),
        compiler_params=_cparams(("parallel", "parallel", "parallel")),
        name="nsa_compress",
    )(flat, pos, w1, w2)


def _cmp_bias(table):
    tt = jnp.arange(BLK)[:, None]
    m = jnp.arange(CMP_WIN)[None, :] + (BLK // D_CMP_STRIDE) - CMP_WIN
    dist = tt - D_CMP_STRIDE * m - (D_CMP_LEN - 1)
    return jnp.where((dist >= 0)[None], _bias_lookup(table, dist), NEG_INF)


def _sel_matrix():
    c_rel = np.arange(CMP_WIN)[:, None] + (BLK // D_CMP_STRIDE) - CMP_WIN
    j_rel = np.arange(REL_BLOCKS)[None, :] - (REL_BLOCKS - 2)
    return ((c_rel >= 4 * j_rel - 1) & (c_rel <= 4 * j_rel + 3)).astype(np.float32)


def _cmp_attn_kernel(q_ref, kc_ref, vc_ref, bias_ref, sel_ref, o_ref, idx_ref, qs_ref):
    i = pl.program_id(1)
    lane = lax.broadcasted_iota(jnp.int32, (1, LANES), 1)
    lo = lane < HEAD_DIM

    hpg = D_HEADS // D_KV_HEADS
    for pair in range(D_HEADS // 2):
        x = q_ref[0, :, pair * LANES:(pair + 1) * LANES].astype(F32)
        r = pltpu.roll(x, HEAD_DIM, axis=1)
        for par, val in ((0, jnp.where(lo, x, r)), (1, jnp.where(lo, r, x))):
            h = 2 * pair + par
            qs_ref[0, h // hpg, pl.ds(h % hpg, BLK, stride=hpg), :] = val
    half_mask = (jnp.where(lo, SCORE_SCALE, 0.0).astype(BF16), jnp.where(lo, 0.0, SCORE_SCALE).astype(BF16))
    row = lax.broadcasted_iota(jnp.int32, (BLK, REL_BLOCKS), 0)
    jj = lax.broadcasted_iota(jnp.int32, (BLK, REL_BLOCKS), 1)
    cur = (REL_BLOCKS - 2) + (row >= D_SEL_LEN).astype(jnp.int32)
    first = (REL_BLOCKS - 2) - 2 * i
    exists = jj >= first
    forced = exists & ((jj == first) | (jj == cur) | (jj == cur - 1))
    valid = exists & (jj <= cur)
    jjf = jj.astype(F32)
    out_lane = lax.broadcasted_iota(jnp.int32, (BLK, LANES), 1)
    out_row = lax.broadcasted_iota(jnp.int32, (BLK, LANES), 0)
    cur_abs = 2 * i + (out_row >= D_SEL_LEN).astype(jnp.int32)
    tt_in_blk = out_row % D_SEL_LEN

    n_keys = CMP_PER_BLK * (i + 1)
    for width in CMP_WIDTHS:
        lo_w = width - CMP_WIDTHS[0] if width > CMP_WIDTHS[0] else -1

        @pl.when((n_keys <= width) & (n_keys > lo_w))
        def _(width=width):
            _cmp_attn_body(q_ref, kc_ref, vc_ref, bias_ref, sel_ref, o_ref, idx_ref, i, width, lo, half_mask,
                           forced, valid, jjf, first, out_lane, cur_abs, tt_in_blk)


def _cmp_attn_body(q_ref, kc_ref, vc_ref, bias_ref, sel_ref, o_ref, idx_ref, i, width, lo, half_mask,
                   forced, valid, jjf, first, out_lane, cur_abs, tt_in_blk):
    hpg = D_HEADS // D_KV_HEADS
    off = CMP_WIN - width
    n_keys = CMP_PER_BLK * (i + 1)
    start = pl.multiple_of(n_keys + off, CMP_PER_BLK)
    edge_w = CMP_WIDTHS[0]
    ucol = lax.broadcasted_iota(jnp.int32, (1, edge_w), 1) + off
    edge = jnp.where(ucol < CMP_WIN - n_keys, NEG_INF, 0.0).astype(F32)

    for g in range(D_KV_HEADS):
        kwin = kc_ref[0, g, pl.ds(start, width), :].astype(BF16)
        vwin = vc_ref[0, g, pl.ds(start, width), :].astype(BF16)
        qs = []
        for hh in range(hpg):
            h = g * hpg + hh
            qs.append(q_ref[0, :, (h // 2) * LANES:(h // 2 + 1) * LANES] * half_mask[h % 2])
        s_all = lax.dot_general(jnp.concatenate(qs, axis=0), kwin, (((1,), (1,)), ((), ())),
                                preferred_element_type=F32)
        imp = jnp.zeros((BLK, width), F32)
        pns = []
        for hh in range(hpg):
            h = g * hpg + hh
            s = s_all[hh * BLK:(hh + 1) * BLK] + bias_ref[h, :, off:]
            s = jnp.concatenate([s[:, :edge_w] + edge, s[:, edge_w:]], axis=1) if width > edge_w else s + edge
            m = jnp.maximum(jnp.max(s, axis=-1, keepdims=True), ROW_MAX_FLOOR)
            p = jnp.exp(s - m)
            den = jnp.maximum(jnp.sum(p, axis=-1, keepdims=True), 1e-30)
            pn = p * (1.0 / den)
            imp = imp + pn
            pns.append(pn.astype(BF16))
        o_all = jnp.dot(jnp.concatenate(pns, axis=0), vwin, preferred_element_type=F32)
        for pr in range(hpg // 2):
            pair = (g * hpg) // 2 + pr
            o_ref[0, :, pair * LANES:(pair + 1) * LANES] = jnp.where(
                lo, o_all[2 * pr * BLK:(2 * pr + 1) * BLK], o_all[(2 * pr + 1) * BLK:(2 * pr + 2) * BLK]
            ).astype(o_ref.dtype)

        sel = sel_ref[off:, :]
        hi = imp.astype(BF16)
        r1 = imp - hi.astype(F32)
        mid = r1.astype(BF16)
        low = (r1 - mid.astype(F32)).astype(BF16)
        imp_sel = (jnp.dot(hi, sel, preferred_element_type=F32)
                   + jnp.dot(mid, sel, preferred_element_type=F32)
                   + jnp.dot(low, sel, preferred_element_type=F32))
        score = jnp.where(forced, FORCED_SCORE, jnp.where(valid, imp_sel, INVALID_SCORE))
        picked = jnp.zeros((BLK, LANES), jnp.int32)
        for r in range(D_SEL_COUNT):
            m = jnp.max(score, axis=-1, keepdims=True)
            am = jnp.min(jnp.where(score == m, jjf, float(REL_BLOCKS)), axis=-1, keepdims=True)
            none = m < 0.5 * INVALID_SCORE
            blk = am.astype(jnp.int32) - first
            key_blk = jnp.where(none, 0, blk)
            bias_row = jnp.where(none, SEL_NONE, jnp.minimum(cur_abs - blk, SEL_FAR)) * D_SEL_LEN + tt_in_blk
            picked = jnp.where(out_lane == r, key_blk, picked)
            picked = jnp.where(out_lane == D_SEL_COUNT + r, bias_row, picked)
            score = jnp.where(jjf == am, -jnp.inf, score)
        idx_ref[0, g] = picked


def _cmp_attention(slab, kc_pad, vc_pad, table, seq):
    b = slab.shape[0]
    n_blk = seq // BLK
    rows = kc_pad.shape[2]
    hpg = D_HEADS // D_KV_HEADS
    bias = _cmp_bias(table)
    sel = jnp.asarray(_sel_matrix(), BF16)
    return pl.pallas_call(
        _cmp_attn_kernel,
        grid=(b, n_blk),
        in_specs=[pl.BlockSpec((1, BLK, D_Q), lambda bi, i: (bi, i, OD_QD // D_Q)),
                  pl.BlockSpec((1, D_KV_HEADS, rows, LANES), lambda bi, i: (bi, 0, 0, 0)),
                  pl.BlockSpec((1, D_KV_HEADS, rows, LANES), lambda bi, i: (bi, 0, 0, 0)),
                  pl.BlockSpec((D_HEADS, BLK, CMP_WIN), lambda bi, i: (0, 0, 0)),
                  pl.BlockSpec((CMP_WIN, REL_BLOCKS), lambda bi, i: (0, 0))],
        out_specs=[pl.BlockSpec((1, BLK, D_Q), lambda bi, i: (bi, i, 0)),
                   pl.BlockSpec((1, D_KV_HEADS, BLK, LANES), lambda bi, i: (bi, 0, i, 0)),
                   pl.BlockSpec((1, D_KV_HEADS, BLK * hpg, LANES), lambda bi, i: (bi, 0, i, 0))],
        out_shape=[jax.ShapeDtypeStruct((b, seq, D_Q), BF16),
                   jax.ShapeDtypeStruct((b, D_KV_HEADS, seq, LANES), jnp.int32),
                   jax.ShapeDtypeStruct((b, D_KV_HEADS, seq * hpg, LANES), F32)],
        compiler_params=_cparams(("parallel", "arbitrary")),
        name="nsa_cmp_attention",
    )(slab, kc_pad, vc_pad, bias, sel)


def _sel_bias(table):
    hpg = D_HEADS // D_KV_HEADS
    delta = jnp.arange(SEL_FAR)[:, None, None]
    tt = jnp.arange(D_SEL_LEN)[None, :, None]
    l = jnp.arange(D_SEL_LEN)[None, None, :]
    dist = D_SEL_LEN * delta + tt - l
    near = jnp.where((dist >= 0)[None], _bias_lookup(table, dist), NEG_INF)
    far = jnp.broadcast_to(table.astype(F32)[REL_BUCKETS - 1][:, None, None, None],
                           (D_HEADS, 1, D_SEL_LEN, D_SEL_LEN))
    none = jnp.full((D_HEADS, 1, D_SEL_LEN, D_SEL_LEN), NEG_INF, F32)
    rows = jnp.concatenate([near, far, none], axis=1)
    half = D_SEL_LEN // 2
    rows = rows.reshape(D_KV_HEADS, hpg, SEL_NONE + 1, D_SEL_LEN, half, 2)
    rows = jnp.transpose(rows, (0, 2, 3, 5, 1, 4))
    rows = rows.reshape(D_KV_HEADS, (SEL_NONE + 1) * D_SEL_LEN, 2 * hpg, half)
    return jnp.tile(rows, (1, 1, 1, LANES // half))


SEL_IDX = 2 * D_SEL_COUNT
SEL_UNROLL = 32


def _sel_attn_kernel(idx_hbm, q_ref, kv_ref, bias_ref, o_ref, idx_smem, sem):
    n_g, n_i = pl.num_programs(1), pl.num_programs(2)
    step = (pl.program_id(0) * n_g + pl.program_id(1)) * n_i + pl.program_id(2)
    total = pl.num_programs(0) * n_g * n_i
    slot = step % 2

    tile_words = BLK * SEL_IDX

    def idx_copy(s, sl):
        dst = idx_smem.at[pl.ds(pl.multiple_of(sl * tile_words, tile_words), tile_words)]
        return pltpu.make_async_copy(idx_hbm.at[s], dst, sem.at[sl])

    @pl.when(step == 0)
    def _():
        idx_copy(0, 0).start()

    @pl.when(step + 1 < total)
    def _():
        idx_copy(step + 1, 1 - slot).start()

    idx_copy(step, slot).wait()

    lane = lax.broadcasted_iota(jnp.int32, (1, LANES), 1)
    lo = lane < HEAD_DIM
    hpg = D_HEADS // D_KV_HEADS
    quarter = D_SEL_LEN // 2
    def token_scores(tl):
        picks = idx_smem.at[pl.ds(slot * tile_words + tl * SEL_IDX, SEL_IDX)]
        qq = q_ref[0, 0, tl] * SCORE_SCALE
        qbd = jnp.concatenate([jnp.where(lo, qq, 0.0), jnp.where(lo, 0.0, qq)], axis=0).astype(BF16)
        ks, vs, bs = [], [], []
        for n in range(D_SEL_COUNT):
            kv = kv_ref[0, 0, picks[n]]
            ks.append(kv[:quarter])
            vs.append(kv[quarter:])
            bs.append(bias_ref[0, picks[D_SEL_COUNT + n]])
        s = lax.dot_general(qbd, jnp.concatenate(ks, axis=0), (((1,), (1,)), ((), ())),
                            preferred_element_type=F32)
        bias = jnp.concatenate(
            [jnp.where(lane < quarter, bs[c],
                       jnp.where(lane < 2 * quarter, bs[c + 1],
                                 jnp.where(lane < 3 * quarter, bs[c + 2], bs[c + 3])))
             for c in range(0, D_SEL_COUNT, 4)], axis=1)
        return s + bias, jnp.concatenate(vs, axis=0)

    def fold(x16):
        return x16[:hpg], x16[hpg:]

    def body(it, carry):
        t0 = it * SEL_UNROLL
        sv = [token_scores(t0 + u) for u in range(SEL_UNROLL)]
        ps, dens = [], []
        for s, _ in sv:
            m = jnp.maximum(*fold(jnp.max(s, axis=-1, keepdims=True)))
            p = jnp.exp(s - jnp.concatenate([m, m], axis=0))
            dens.append(sum(fold(jnp.sum(p, axis=-1, keepdims=True))))
            ps.append(p.astype(BF16))
        outs = []
        for p, den, (_, vall) in zip(ps, dens, sv):
            o_top, o_bot = fold(jnp.dot(p, vall, preferred_element_type=F32))
            outs.append(jnp.where(lo, o_top, o_bot) * (1.0 / den))
        for u in range(SEL_UNROLL):
            o_ref[0, 0, t0 + u] = outs[u]
        return carry

    lax.fori_loop(0, BLK // SEL_UNROLL, body, 0)


def _sel_attention(idx, q_sel, kv_sel, table, seq):
    b = q_sel.shape[0]
    hpg = D_HEADS // D_KV_HEADS
    n_blk = seq // BLK
    n_sb = seq // D_SEL_LEN
    bias = _sel_bias(table)
    kv_spec = pl.BlockSpec((1, 1, n_sb, D_SEL_LEN, LANES), lambda bi, g, i: (bi, g, 0, 0, 0))
    return pl.pallas_call(
        _sel_attn_kernel,
        grid=(b, D_KV_HEADS, n_blk),
        in_specs=[pl.BlockSpec(memory_space=pl.ANY),
                  pl.BlockSpec((1, 1, BLK, hpg, LANES), lambda bi, g, i: (bi, g, i, 0, 0)),
                  kv_spec,
                  pl.BlockSpec((1, (SEL_NONE + 1) * D_SEL_LEN, 2 * hpg, LANES), lambda bi, g, i: (g, 0, 0, 0))],
        out_specs=pl.BlockSpec((1, 1, BLK, hpg, LANES), lambda bi, g, i: (bi, g, i, 0, 0)),
        out_shape=jax.ShapeDtypeStruct((b, D_KV_HEADS, seq, hpg, LANES), F32),
        scratch_shapes=[pltpu.SMEM((2 * BLK * SEL_IDX,), jnp.int32), pltpu.SemaphoreType.DMA((2,))],
        compiler_params=_cparams(("arbitrary", "arbitrary", "arbitrary")),
        name="nsa_sel_attention",
    )(idx, q_sel, kv_sel, bias)


def _gate_expand():
    e = np.zeros((3, LANES, D_Q), np.float32)
    for h in range(D_HEADS):
        for c in range(3):
            e[c, 3 * h + c, h * HEAD_DIM:(h + 1) * HEAD_DIM] = 1.0
    return e


def _head_place():
    hpg = D_HEADS // D_KV_HEADS
    p = np.zeros((hpg * LANES, hpg * HEAD_DIM), np.float32)
    for hh in range(hpg):
        for lane in range(LANES):
            p[hh * LANES + lane, hh * HEAD_DIM + lane % HEAD_DIM] = 1.0
    return p


def _nsa_gate_kernel(gd_ref, e_ref, place_ref, oc_ref, os_ref, ow_ref, out_ref):
    hpg = D_HEADS // D_KV_HEADS
    tm = out_ref.shape[0]
    groups = []
    for g in range(D_KV_HEADS):
        rows = jnp.concatenate([os_ref[0, g, pl.ds(hh, tm, stride=hpg), :].astype(BF16) for hh in range(hpg)],
                               axis=1)
        groups.append(jnp.dot(rows, place_ref[...], preferred_element_type=F32))
    o_s = jnp.concatenate(groups, axis=1)

    sg = jax.nn.sigmoid(gd_ref[...].astype(F32))
    hi = sg.astype(BF16)
    low = (sg - hi.astype(F32)).astype(BF16)
    acc = None
    for c, branch in enumerate((oc_ref[...].astype(F32), o_s, ow_ref[...].astype(F32))):
        gate = (jnp.dot(hi, e_ref[c], preferred_element_type=F32)
                + jnp.dot(low, e_ref[c], preferred_element_type=F32))
        acc = gate * branch if acc is None else acc + gate * branch
    out_ref[...] = acc.astype(out_ref.dtype)


def _nsa_gate(slab2d, o_c, o_s_raw, o_w, tm):
    m = slab2d.shape[0]
    b, g_kv, seq, hpg, _ = o_s_raw.shape
    tiles_per_seq = seq // tm
    e = jnp.asarray(_gate_expand(), BF16)
    place = jnp.asarray(_head_place(), BF16)
    spec = pl.BlockSpec((tm, D_Q), lambda i: (i, 0))
    return pl.pallas_call(
        _nsa_gate_kernel,
        grid=(m // tm,),
        in_specs=[pl.BlockSpec((tm, LANES), lambda i: (i, OD_GD // LANES)),
                  pl.BlockSpec((3, LANES, D_Q), lambda i: (0, 0, 0)),
                  pl.BlockSpec((hpg * LANES, hpg * HEAD_DIM), lambda i: (0, 0)),
                  spec,
                  pl.BlockSpec((1, g_kv, tm * hpg, LANES),
                               lambda i: (i // tiles_per_seq, 0, i % tiles_per_seq, 0)),
                  spec],
        out_specs=spec,
        out_shape=jax.ShapeDtypeStruct((m, D_Q), BF16),
        compiler_params=_cparams(("parallel",)),
        name="nsa_gate",
    )(slab2d, e, place, o_c, o_s_raw.reshape(b, g_kv, seq * hpg, LANES), o_w)


def _even_mixer(h, nw, w_in, sinks, w_out, rel_table, b, seq):
    qa, ka, va, qb, kb, vb = jnp.split(w_in, [int(c) for c in np.cumsum([A_Q, A_KV, A_KV, B_W, B_W])], axis=1)
    pad = jnp.zeros((D_MODEL, EVEN_SLAB - w_in.shape[1]), w_in.dtype)
    w_slab = jnp.concatenate([qa, qb, kb, vb, ka, va, pad], axis=1).astype(BF16)
    dilations = [dil for _, dil in B_PATTERNS if dil > 1]
    slabs = _norm_matmul(h, nw, w_slab, TM_IN_PROJ, TN_EVEN, dilations, seq)
    by_dil = {1: slabs[0].reshape(b, 1, seq, EVEN_SLAB)}
    by_dil.update(zip(dilations, slabs[1:]))

    (oa,) = _banded_attention(by_dil[1], q_off=EV_QA, k_off=EV_KA, v_off=EV_VA,
                              n_heads=A_HEADS, n_groups=A_KV_HEADS, max_dist=A_WINDOW - 1,
                              table=rel_table[:, :A_HEADS], sinks=sinks)
    outs, lses = [], []
    for window, dil in B_PATTERNS:
        o, lse = _banded_attention(by_dil[dil], q_off=EV_QB, k_off=EV_KB, v_off=EV_VB,
                                   n_heads=B_HEADS, n_groups=B_HEADS, max_dist=window // dil,
                                   table=rel_table[:, A_HEADS:A_HEADS + B_HEADS], want_lse=True, out_dtype=F32)
        outs.append(o)
        lses.append(lse)
    ob = _dilated_mix(outs, lses, TM_MIX)
    w_out = w_out.astype(BF16)
    return _proj_residual(h, [oa.reshape(b * seq, A_Q), ob], [w_out[:A_Q], w_out[A_Q:]], TM_OUT_PROJ)


def _nsa(slab, pos_k, pos_v, k_w1, k_w2, v_w1, v_w2, rel_table, b, seq):
    g_kv = D_KV_HEADS
    hpg = D_HEADS // g_kv
    slab2d = slab.reshape(b * seq, ODD_SLAB)
    n_rows = seq // D_CMP_STRIDE

    def rows16(off):
        a = slab[:, :, off:off + D_KV].reshape(b, n_rows, D_CMP_STRIDE, g_kv, HEAD_DIM)
        a = jnp.transpose(a, (0, 3, 1, 2, 4)).reshape(b * g_kv, n_rows, D_CMP_STRIDE * HEAD_DIM)
        nxt = jnp.concatenate([a[:, 1:], jnp.zeros_like(a[:, :1])], axis=1)
        return jnp.concatenate([a, nxt], axis=-1)

    flat = jnp.stack([rows16(OD_KCMP), rows16(OD_VCMP)])
    pos = jnp.stack([pos_k.reshape(1, -1), pos_v.reshape(1, -1)]).astype(F32)
    w1 = jnp.stack([k_w1, v_w1]).astype(BF16)
    w2 = jnp.stack([k_w2, v_w2]).astype(BF16)
    cmp = _compress(flat, pos, w1, w2).reshape(2, b, g_kv, n_rows, HEAD_DIM)
    cmp = jnp.pad(cmp, ((0, 0), (0, 0), (0, 0), (CMP_WIN, 0), (0, 0)))
    cmp = jnp.concatenate([cmp, cmp], axis=-1)
    o_c, idx, q_sel = _cmp_attention(slab, cmp[0], cmp[1], rel_table, seq)
    q_sel = q_sel.reshape(b, g_kv, seq, hpg, LANES)

    idx = idx[..., :SEL_IDX].reshape(b * g_kv * (seq // BLK), BLK * SEL_IDX)

    def per_group(off):
        return slab[:, :, off:off + D_KV].reshape(b, seq, g_kv, HEAD_DIM)

    def two_per_row(off):
        a = jnp.transpose(per_group(off), (0, 2, 1, 3))
        return a.reshape(b, g_kv, seq // D_SEL_LEN, D_SEL_LEN // 2, LANES)

    kv_sel = jnp.concatenate([two_per_row(OD_KSLC), two_per_row(OD_VSLC)], axis=3)
    o_s = _sel_attention(idx, q_sel, kv_sel, rel_table, seq)

    (o_w,) = _banded_attention(slab.reshape(b, 1, seq, ODD_SLAB), q_off=OD_QD, k_off=OD_KWIN, v_off=OD_VWIN,
                               n_heads=D_HEADS, n_groups=D_KV_HEADS, max_dist=D_WINDOW - 1, table=rel_table)
    return _nsa_gate(slab2d, o_c.reshape(b * seq, D_Q), o_s, o_w.reshape(b * seq, D_Q), TM_GATE)


def _odd_mixer(h, nw, w_in, ret_gn, pos_k, pos_v, k_w1, k_w2, v_w1, v_w2, w_out, rel_table, b, seq):
    pad = jnp.zeros((D_MODEL, ODD_SLAB - ODD_IN), w_in.dtype)
    w_slab = jnp.concatenate([w_in, pad], axis=1).astype(BF16)
    slab = _norm_matmul(h, nw, w_slab, TM_IN_PROJ, TN_ODD).reshape(b, seq, ODD_SLAB)
    oc = _retention(slab, ret_gn, seq)
    od = _nsa(slab, pos_k, pos_v, k_w1, k_w2, v_w1, v_w2, rel_table, b, seq)
    w_out = w_out.astype(BF16)
    return _proj_residual(h, [oc.reshape(b * seq, C_V), od], [w_out[:C_V], w_out[C_V:]], TM_OUT_PROJ)


def kernel(x, rel_table, norm_mix, norm_ffn, norm_final, even_w_in, even_sinks, even_w_out, odd_w_in, odd_ret_gn, odd_cmp_pos_k, odd_cmp_pos_v, odd_cmp_k_w1, odd_cmp_k_w2, odd_cmp_v_w1, odd_cmp_v_w2, odd_w_out, ffn_w_gate, ffn_w_up, ffn_w_down):
    b, seq, d = x.shape
    h = x.reshape(b * seq, d)
    for layer in range(DEPTH):
        li = layer // 2
        if layer % 2 == 0:
            h = _even_mixer(h, norm_mix[layer], even_w_in[li], even_sinks[li], even_w_out[li], rel_table, b, seq)
        else:
            h = _odd_mixer(h, norm_mix[layer], odd_w_in[li], odd_ret_gn[li], odd_cmp_pos_k[li],
                           odd_cmp_pos_v[li], odd_cmp_k_w1[li], odd_cmp_k_w2[li], odd_cmp_v_w1[li],
                           odd_cmp_v_w2[li], odd_w_out[li], rel_table, b, seq)
        h = _ffn(h, norm_ffn[layer], ffn_w_gate[layer].astype(BF16), ffn_w_up[layer].astype(BF16),
                 ffn_w_down[layer].astype(BF16), norm_final, layer == DEPTH - 1, TM_FFN)
    return h.reshape(b, seq, d)
```
